```python
import jax, jax.numpy as jnp
from jax import lax
import numpy as np

D_MODEL = 1024
BATCH = 4
SEQ = 4096
DEPTH = 4
DEC_BATCH = 128
DEC_SEQ = 4
PAST_LEN = 2048
PAGE_SIZE = 128

N_MIXERS = 2
N_POOL_LAYERS = (DEPTH + 1) // 2
N_ATTN_LAYERS = DEPTH // 2
POOL_WINDOWS = (2, 4, 8, 16)
N_POOL_GROUPS = len(POOL_WINDOWS)
POOL_CH = D_MODEL // N_POOL_GROUPS
POOL_MAXW = max(POOL_WINDOWS)
POOL_STATE = POOL_MAXW - 1
ATTN_GROUPS = ((128, 1), (512, 4), (2048, 16))
N_ATTN_GROUPS = len(ATTN_GROUPS)
HEADS_PER_GROUP = 8
HEAD_DIM = 64
BAND = 128
QKV_WIDTH = 3 * N_ATTN_GROUPS * HEADS_PER_GROUP * HEAD_DIM
ATTN_OUT = HEADS_PER_GROUP * HEAD_DIM
D_FF = 4 * D_MODEL
N_BUCKETS = 32
MAX_EXACT = N_BUCKETS // 2
REL_MAX_DIST = 2048
RMS_EPS = 1e-6
NEG_INF = -1e30

kernel_name = "hybrid_pool_dilated_attn_decode_step"


def _rmsnorm(x, g):
    x32 = x.astype(jnp.float32)
    y = x32 * lax.rsqrt(jnp.mean(x32 * x32, axis=-1, keepdims=True) + RMS_EPS)
    return (y * g.astype(jnp.float32)).astype(x.dtype)


def _t5_bucket(dist):
    n = np.maximum(np.asarray(dist), 0)
    large = MAX_EXACT + (np.log(np.maximum(n, 1) / MAX_EXACT) / np.log(REL_MAX_DIST / MAX_EXACT)
                         * (N_BUCKETS - MAX_EXACT)).astype(np.int64)
    large = np.minimum(large, N_BUCKETS - 1)
    return np.where(n < MAX_EXACT, n, large).astype(np.int32)


def _pool_mixer(h, prefix, pos0, w_pool_l, scale_l):
    T = h.shape[1]
    h32 = h.astype(jnp.float32)
    hp = jnp.concatenate([prefix.astype(jnp.float32), h32], axis=1)
    cs = jnp.cumsum(hp, axis=1)
    pos = pos0 + np.arange(T) + 1
    outs = []
    for g, w in enumerate(POOL_WINDOWS):
        sl = slice(g * POOL_CH, (g + 1) * POOL_CH)
        win_sum = cs[:, POOL_MAXW:POOL_MAXW + T, sl] - cs[:, POOL_MAXW - w:POOL_MAXW - w + T, sl]
        cnt = np.minimum(w, pos).astype(np.float32)[None, :, None]
        outs.append(win_sum / cnt - h32[..., sl])
    p = jnp.stack(outs, axis=2).astype(h.dtype)
    y = jnp.einsum('btgc,gcf->btgf', p, w_pool_l).reshape(h.shape) * scale_l
    new_state = hp[:, -POOL_STATE:].astype(h.dtype)
    return y.astype(h.dtype), new_state


def _softmax_stats(s, axis):
    mx = jnp.max(s, axis=axis, keepdims=True)
    e = jnp.exp(s - mx)
    den = jnp.sum(e, axis=axis, keepdims=True)
    lse = jnp.squeeze(mx + jnp.log(den), axis)
    return e / den, lse


def _attn_prompt_group(q, k, v, dil, bias_tab):
    B, S, H, E = q.shape
    L = S // dil
    nb = -(-L // BAND)
    Lp = nb * BAND

    def to_blocks(t):
        t = t.reshape(B, L, dil, H, E).transpose(0, 2, 1, 3, 4)
        t = jnp.pad(t, ((0, 0), (0, 0), (0, Lp - L), (0, 0), (0, 0)))
        return t.reshape(B, dil, nb, BAND, H, E)

    def with_prev(t):
        prev = jnp.concatenate([jnp.zeros_like(t[:, :, :1]), t[:, :, :-1]], axis=2)
        return jnp.concatenate([prev, t], axis=3)

    qb, kb, vb = to_blocks(q), to_blocks(k), to_blocks(v)
    kk, vv = with_prev(kb), with_prev(vb)
    s = jnp.einsum('brnqhe,brnkhe->brnhqk', qb, kk, preferred_element_type=jnp.float32)
    m = BAND + np.arange(BAND)[:, None] - np.arange(2 * BAND)[None, :]
    band = (m >= 0) & (m <= BAND)
    valid = band[None] & ~((np.arange(nb)[:, None, None] == 0) & (np.arange(2 * BAND)[None, None, :] < BAND))
    bias = bias_tab.astype(jnp.float32)[_t5_bucket(np.clip(m, 0, BAND) * dil)]
    s = s + jnp.transpose(bias, (2, 0, 1))
    s = jnp.where(valid[:, None], s, NEG_INF)
    p, lse = _softmax_stats(s, -1)
    o = jnp.einsum('brnhqk,brnkhe->brnqhe', p, vv.astype(jnp.float32))
    o = o.reshape(B, dil, Lp, H, E)[:, :, :L].transpose(0, 2, 1, 3, 4).reshape(B, S, H, E)
    lse = lse.transpose(0, 1, 2, 4, 3).reshape(B, dil, Lp, H)[:, :, :L].transpose(0, 2, 1, 3).reshape(B, S, H)
    return o, lse


def _attn_sample_group(q, k, v, buf, dil, bias_tab):
    N, T, H, E = q.shape
    Wb = buf.shape[1]
    kf = jnp.concatenate([buf[:, :, 0].astype(k.dtype), k], axis=1)
    vf = jnp.concatenate([buf[:, :, 1].astype(v.dtype), v], axis=1)
    dist = np.arange(BAND + 1) * dil
    idx = Wb + np.arange(T)[:, None] - dist[None, :]
    valid = idx >= 0
    idx = np.maximum(idx, 0).astype(np.int32)
    kg = jnp.take(kf, idx, axis=1)
    vg = jnp.take(vf, idx, axis=1)
    s = jnp.einsum('nthe,ntkhe->nhtk', q, kg, preferred_element_type=jnp.float32)
    bias = bias_tab.astype(jnp.float32)[_t5_bucket(dist)]
    s = s + bias.T[:, None, :]
    s = jnp.where(valid[None, None], s, NEG_INF)
    p, lse = _softmax_stats(s, -1)
    o = jnp.einsum('nhtk,ntkhe->nthe', p, vg.astype(jnp.float32))
    return o, jnp.transpose(lse, (0, 2, 1))


def _dilated_attention(h, w_qkv_l, w_o_l, rel_bias, bufs):
    B, T, _ = h.shape
    qkv = (h @ w_qkv_l).reshape(B, T, 3, N_ATTN_GROUPS, HEADS_PER_GROUP, HEAD_DIM)
    outs, lses, new_kv = [], [], []
    for g, (win, dil) in enumerate(ATTN_GROUPS):
        q = qkv[:, :, 0, g] * (HEAD_DIM ** -0.5)
        k = qkv[:, :, 1, g]
        v = qkv[:, :, 2, g]
        kv = jnp.stack([k, v], axis=2)
        if bufs is None:
            o, l = _attn_prompt_group(q, k, v, dil, rel_bias[:, g])
            new_kv.append(kv[:, -min(win, T):])
        else:
            o, l = _attn_sample_group(q, k, v, bufs[g], dil, rel_bias[:, g])
            new_kv.append(kv)
        outs.append(o)
        lses.append(l)
    wts = jax.nn.softmax(jnp.stack(lses), axis=0)
    o = jnp.sum(wts[..., None] * jnp.stack(outs), axis=0)
    y = o.reshape(B, T, ATTN_OUT).astype(h.dtype) @ w_o_l
    return y, new_kv


def _sqrelu_mlp(h, w_up_l, w_down_l):
    u = h @ w_up_l
    return jnp.square(jax.nn.relu(u)) @ w_down_l


def setup_inputs(seed: int = 0) -> dict:
    key = jax.random.key(seed)
    ks = jax.random.split(key, 18)
    f32 = jnp.float32
    nrm = lambda k, shp: jax.random.normal(k, shp, f32)
    inp = {}
    inp["x_prompt"] = nrm(ks[0], (BATCH, SEQ, D_MODEL))
    inp["x_sample"] = nrm(ks[1], (DEC_BATCH, DEC_SEQ, D_MODEL))
    inp["state_pool"] = nrm(ks[2], (N_POOL_LAYERS, DEC_BATCH, POOL_STATE, D_MODEL))
    for i, (win, _) in enumerate(ATTN_GROUPS):
        inp["cache_kv_w%d" % win] = nrm(ks[3 + i], (N_ATTN_LAYERS, DEC_BATCH, min(win, PAST_LEN), 2, HEADS_PER_GROUP, HEAD_DIM))
    inp["rel_bias"] = 0.5 * nrm(ks[6], (N_BUCKETS, N_ATTN_GROUPS, HEADS_PER_GROUP))
    inp["norm_mix"] = 1.0 + 0.02 * nrm(ks[7], (DEPTH, D_MODEL))
    inp["norm_ffn"] = 1.0 + 0.02 * nrm(ks[8], (DEPTH, D_MODEL))
    inp["norm_final"] = 1.0 + 0.02 * nrm(ks[9], (D_MODEL,))
    inp["w_pool"] = nrm(ks[10], (N_POOL_LAYERS, N_POOL_GROUPS, POOL_CH, POOL_CH)) * POOL_CH ** -0.5
    inp["pool_scale"] = 1.0 + 0.02 * nrm(ks[11], (N_POOL_LAYERS, D_MODEL))
    inp["w_qkv"] = nrm(ks[12], (N_ATTN_LAYERS, D_MODEL, QKV_WIDTH)) * D_MODEL ** -0.5
    inp["w_o"] = nrm(ks[13], (N_ATTN_LAYERS, ATTN_OUT, D_MODEL)) * ATTN_OUT ** -0.5
    inp["w_up"] = nrm(ks[14], (DEPTH, D_MODEL, D_FF)) * D_MODEL ** -0.5
    inp["w_down"] = nrm(ks[15], (DEPTH, D_FF, D_MODEL)) * D_FF ** -0.5
    return inp


def reference(x_prompt, x_sample, state_pool, cache_kv_w128, cache_kv_w512, cache_kv_w2048,
              rel_bias, norm_mix, norm_ffn, norm_final, w_pool, pool_scale, w_qkv, w_o, w_up, w_down):
    xp, xs = x_prompt, x_sample
    pool_p, pool_s = [], []
    kv_p = [[] for _ in ATTN_GROUPS]
    kv_s = [[] for _ in ATTN_GROUPS]
    for i in range(DEPTH):
        li = i // N_MIXERS
        hp = _rmsnorm(xp, norm_mix[i])
        hs = _rmsnorm(xs, norm_mix[i])
        if i % N_MIXERS == 0:
            prefix_p = jnp.zeros((xp.shape[0], POOL_MAXW, D_MODEL), xp.dtype)
            prefix_s = jnp.concatenate([jnp.zeros((xs.shape[0], POOL_MAXW - POOL_STATE, D_MODEL), xs.dtype),
                                        state_pool[li].astype(xs.dtype)], axis=1)
            yp, sp = _pool_mixer(hp, prefix_p, 0, w_pool[li], pool_scale[li])
            ys, ss = _pool_mixer(hs, prefix_s, PAST_LEN, w_pool[li], pool_scale[li])
            pool_p.append(sp)
            pool_s.append(ss)
        else:
            bufs = (cache_kv_w128[li], cache_kv_w512[li], cache_kv_w2048[li])
            yp, nkp = _dilated_attention(hp, w_qkv[li], w_o[li], rel_bias, None)
            ys, nks = _dilated_attention(hs, w_qkv[li], w_o[li], rel_bias, bufs)
            for g in range(N_ATTN_GROUPS):
                kv_p[g].append(nkp[g])
                kv_s[g].append(nks[g])
        xp = xp + yp
        xs = xs + ys
        xp = xp + _sqrelu_mlp(_rmsnorm(xp, norm_ffn[i]), w_up[i], w_down[i])
        xs = xs + _sqrelu_mlp(_rmsnorm(xs, norm_ffn[i]), w_up[i], w_down[i])
    y_prompt = _rmsnorm(xp, norm_final)
    y_sample = _rmsnorm(xs, norm_final)
    return (y_prompt, y_sample,
            jnp.stack(pool_p), jnp.stack(pool_s),
            jnp.stack(kv_p[0]), jnp.stack(kv_s[0]),
            jnp.stack(kv_p[1]), jnp.stack(kv_s[1]),
            jnp.stack(kv_p[2]), jnp.stack(kv_s[2]))
```

```python
import functools

import numpy as np
import jax
import jax.numpy as jnp
from jax import lax
from jax.experimental import pallas as pl
from jax.experimental.pallas import tpu as pltpu

F32 = jnp.float32
BF16 = jnp.bfloat16

D_MODEL = 1024
D_FF = 4 * D_MODEL
POOL_WINDOWS = (2, 4, 8, 16)
POOL_CH = D_MODEL // len(POOL_WINDOWS)
POOL_STATE = max(POOL_WINDOWS) - 1
ATTN_GROUPS = ((128, 1), (512, 4), (2048, 16))
N_GROUPS = len(ATTN_GROUPS)
HEADS = 8
HEAD_DIM = 64
ATTN_OUT = HEADS * HEAD_DIM
BAND = 128
N_BUCKETS = 32
MAX_EXACT = N_BUCKETS // 2
REL_MAX_DIST = 2048
PAST_LEN = 2048
RMS_EPS = 1e-6
NEG_INF = -1e30

VMEM_LIMIT_BYTES = 56 * 1024 * 1024
ROW_TILE = 512
FF_CHUNK = 1024


def _cparams(*sem):
    return pltpu.CompilerParams(dimension_semantics=sem, vmem_limit_bytes=VMEM_LIMIT_BYTES)


def _rms(x, g):
    ms = jnp.mean(x * x, axis=-1, keepdims=True)
    return x * lax.rsqrt(ms + RMS_EPS) * g


def _t5_bucket(dist):
    n = np.maximum(np.asarray(dist), 0)
    large = MAX_EXACT + (np.log(np.maximum(n, 1) / MAX_EXACT) / np.log(REL_MAX_DIST / MAX_EXACT)
                         * (N_BUCKETS - MAX_EXACT)).astype(np.int64)
    large = np.minimum(large, N_BUCKETS - 1)
    return np.where(n < MAX_EXACT, n, large).astype(np.int32)


def _const_spec(shape):
    nd = len(shape)
    return pl.BlockSpec(shape, lambda *_: (0,) * nd, pipeline_mode=pl.Buffered(1))


def _mlp_kernel(x_ref, g_ref, wu_ref, wd_ref, gf_ref, o_ref, *, final_norm):
    x = x_ref[...]
    h = _rms(x, g_ref[...]).astype(BF16)
    acc = x
    for c in range(D_FF // FF_CHUNK):
        cols = slice(c * FF_CHUNK, (c + 1) * FF_CHUNK)
        u = jnp.dot(h, wu_ref[:, cols], preferred_element_type=F32)
        a = jnp.square(jnp.maximum(u, 0.0)).astype(BF16)
        acc = acc + jnp.dot(a, wd_ref[cols, :], preferred_element_type=F32)
    if final_norm:
        acc = _rms(acc, gf_ref[...])
    o_ref[...] = acc


def _mlp(x, g, w_up, w_down, g_final, final_norm):
    m = x.shape[0]
    tm = min(ROW_TILE, m)
    return pl.pallas_call(
        functools.partial(_mlp_kernel, final_norm=final_norm),
        grid=(m // tm,),
        in_specs=[
            pl.BlockSpec((tm, D_MODEL), lambda i: (i, 0)),
            _const_spec((1, D_MODEL)),
            _const_spec((D_MODEL, D_FF)),
            _const_spec((D_FF, D_MODEL)),
            _const_spec((1, D_MODEL)),
        ],
        out_specs=pl.BlockSpec((tm, D_MODEL), lambda i: (i, 0)),
        out_shape=jax.ShapeDtypeStruct((m, D_MODEL), F32),
        compiler_params=_cparams("parallel"),
        name="mlp",
    )(x, g.reshape(1, D_MODEL), w_up, w_down, g_final.reshape(1, D_MODEL))


Q_COLS = N_GROUPS * ATTN_OUT
KV_COLS = 2 * N_GROUPS * ATTN_OUT


def _qkv_kernel(x_ref, g_ref, w_ref, q_ref, kvb_ref, kvf_ref):
    h = _rms(x_ref[...], g_ref[...]).astype(BF16)
    q = jnp.dot(h, w_ref[:, :Q_COLS], preferred_element_type=F32)
    q_ref[...] = (q * (HEAD_DIM ** -0.5)).astype(BF16)
    for g in range(N_GROUPS):
        cols = slice(g * 2 * ATTN_OUT, (g + 1) * 2 * ATTN_OUT)
        kv = jnp.dot(h, w_ref[:, Q_COLS + cols.start:Q_COLS + cols.stop], preferred_element_type=F32)
        kvf_ref[:, cols] = kv
        kvb_ref[:, cols] = kv.astype(BF16)


def _qkv(x, g, w):
    m = x.shape[0]
    tm = min(ROW_TILE, m)
    row = lambda i: (i, 0)
    return pl.pallas_call(
        _qkv_kernel,
        grid=(m // tm,),
        in_specs=[
            pl.BlockSpec((tm, D_MODEL), row),
            _const_spec((1, D_MODEL)),
            _const_spec((D_MODEL, Q_COLS + KV_COLS)),
        ],
        out_specs=[
            pl.BlockSpec((tm, Q_COLS), row),
            pl.BlockSpec((tm, KV_COLS), row),
            pl.BlockSpec((tm, KV_COLS), row),
        ],
        out_shape=[
            jax.ShapeDtypeStruct((m, Q_COLS), BF16),
            jax.ShapeDtypeStruct((m, KV_COLS), BF16),
            jax.ShapeDtypeStruct((m, KV_COLS), F32),
        ],
        compiler_params=_cparams("parallel"),
        name="qkv",
    )(x, g.reshape(1, D_MODEL), w)


def _attn_prompt_kernel(q_ref, kvp_ref, kvc_ref, bias_ref, o_ref, lse_ref):
    n = pl.program_id(2)
    q = q_ref[0]
    kv = jnp.concatenate([kvp_ref[0], kvc_ref[0]], axis=0)
    qi = lax.broadcasted_iota(jnp.int32, (BAND, 2 * BAND), 0)
    kj = lax.broadcasted_iota(jnp.int32, (BAND, 2 * BAND), 1)
    m = BAND + qi - kj
    first_key = jnp.where(n > 0, 0, BAND)
    valid = (m >= 0) & (m <= BAND) & (kj >= first_key)
    for h in range(HEADS):
        lo, hi = h * HEAD_DIM, (h + 1) * HEAD_DIM
        s = lax.dot_general(q[:, lo:hi], kv[:, lo:hi], (((1,), (1,)), ((), ())),
                            preferred_element_type=F32)
        s = jnp.where(valid, s + bias_ref[h], NEG_INF)
        mx = jnp.max(s, axis=-1, keepdims=True)
        e = jnp.exp(s - mx)
        den = jnp.sum(e, axis=-1, keepdims=True)
        o = jnp.dot(e.astype(BF16), kv[:, ATTN_OUT + lo:ATTN_OUT + hi], preferred_element_type=F32)
        o_ref[0, :, lo:hi] = o / den
        lse_ref[0, 0, :, h:h + 1] = mx + jnp.log(den)


def _attn_prompt(q, kvb, bias, g, dil, batch, seq):
    sub = seq // dil
    nb = sub // BAND
    qv = q.reshape(batch, sub, dil * Q_COLS)
    kvv = kvb.reshape(batch, sub, dil * KV_COLS)
    o, lse = pl.pallas_call(
        _attn_prompt_kernel,
        grid=(batch, dil, nb),
        in_specs=[
            pl.BlockSpec((1, BAND, ATTN_OUT), lambda b, r, n: (b, n, r * N_GROUPS + g)),
            pl.BlockSpec((1, BAND, 2 * ATTN_OUT), lambda b, r, n: (b, jnp.maximum(n - 1, 0), r * N_GROUPS + g)),
            pl.BlockSpec((1, BAND, 2 * ATTN_OUT), lambda b, r, n: (b, n, r * N_GROUPS + g)),
            _const_spec((HEADS, BAND, 2 * BAND)),
        ],
        out_specs=[
            pl.BlockSpec((1, BAND, ATTN_OUT), lambda b, r, n: (b, n, r)),
            pl.BlockSpec((1, 1, BAND, HEADS), lambda b, r, n: (b, r, n, 0)),
        ],
        out_shape=[
            jax.ShapeDtypeStruct((batch, sub, dil * ATTN_OUT), F32),
            jax.ShapeDtypeStruct((batch, dil, sub, HEADS), F32),
        ],
        compiler_params=_cparams("parallel", "parallel", "arbitrary"),
        name="attn_prompt_d%d" % dil,
    )(qv, kvv, kvv, bias)
    o = o.reshape(batch * seq, ATTN_OUT)
    lse = jnp.transpose(lse, (0, 2, 1, 3)).reshape(batch * seq, HEADS)
    return o, lse


def _prompt_bias(rel_bias_g, dil):
    m = BAND + np.arange(BAND)[:, None] - np.arange(2 * BAND)[None, :]
    bias = rel_bias_g.astype(F32)[_t5_bucket(np.clip(m, 0, BAND) * dil)]
    return jnp.transpose(bias, (2, 0, 1))


def _wo_merge_kernel(o0_ref, o1_ref, o2_ref, l0_ref, l1_ref, l2_ref, e_ref, w_ref, x_ref, out_ref):
    ls = (l0_ref[...], l1_ref[...], l2_ref[...])
    mx = jnp.maximum(jnp.maximum(ls[0], ls[1]), ls[2])
    es = [jnp.exp(l - mx) for l in ls]
    inv = 1.0 / (es[0] + es[1] + es[2])
    acc = None
    for e, o_ref in zip(es, (o0_ref, o1_ref, o2_ref)):
        wt = e * inv
        hi = wt.astype(BF16)
        lo = (wt - hi.astype(F32)).astype(BF16)
        wexp = (jnp.dot(hi, e_ref[...], preferred_element_type=F32)
                + jnp.dot(lo, e_ref[...], preferred_element_type=F32))
        term = wexp * o_ref[...]
        acc = term if acc is None else acc + term
    out_ref[...] = x_ref[...] + jnp.dot(acc.astype(BF16), w_ref[...], preferred_element_type=F32)


def _wo_merge(os, lses, w_o, x):
    m = x.shape[0]
    tm = min(ROW_TILE, m)
    row = lambda i: (i, 0)
    expand = jnp.asarray(np.repeat(np.eye(HEADS, dtype=np.float32), HEAD_DIM, axis=1), BF16)
    return pl.pallas_call(
        _wo_merge_kernel,
        grid=(m // tm,),
        in_specs=[pl.BlockSpec((tm, ATTN_OUT), row)] * 3 + [pl.BlockSpec((tm, HEADS), row)] * 3 + [
            _const_spec((HEADS, ATTN_OUT)),
            _const_spec((ATTN_OUT, D_MODEL)),
            pl.BlockSpec((tm, D_MODEL), row),
        ],
        out_specs=pl.BlockSpec((tm, D_MODEL), row),
        out_shape=jax.ShapeDtypeStruct((m, D_MODEL), F32),
        compiler_params=_cparams("parallel"),
        name="wo_merge",
    )(*os, *lses, expand, w_o, x)


def _wo_kernel(o_ref, w_ref, x_ref, out_ref):
    out_ref[...] = x_ref[...] + jnp.dot(o_ref[...].astype(BF16), w_ref[...], preferred_element_type=F32)


def _wo(o, w_o, x):
    m = x.shape[0]
    tm = min(ROW_TILE, m)
    row = lambda i: (i, 0)
    return pl.pallas_call(
        _wo_kernel,
        grid=(m // tm,),
        in_specs=[pl.BlockSpec((tm, ATTN_OUT), row), _const_spec((ATTN_OUT, D_MODEL)),
                  pl.BlockSpec((tm, D_MODEL), row)],
        out_specs=pl.BlockSpec((tm, D_MODEL), row),
        out_shape=jax.ShapeDtypeStruct((m, D_MODEL), F32),
        compiler_params=_cparams("parallel"),
        name="wo",
    )(o, w_o, x)


NEW_PAD = 16


def _sample_tables(rel_bias, dec_seq, buf_rows):
    t_of_row = np.repeat(np.arange(dec_seq), HEADS)
    h_of_row = np.tile(np.arange(HEADS), dec_seq)
    biases, valids = [], []
    nb, nv = [], []
    for g, (_, dil) in enumerate(ATTN_GROUPS):
        wb = buf_rows[g]
        qpos = wb + t_of_row
        if g == 2:
            r = np.repeat(np.arange(dec_seq), wb // 16)
            l = np.tile(np.arange(wb // 16), dec_seq)
            kpos = 16 * l + r
        else:
            kpos = np.arange(wb)
        for kp, bl, vl in ((kpos, biases, valids), (wb + np.arange(NEW_PAD), nb, nv)):
            delta = qpos[:, None] - kp[None, :]
            ok = (delta >= 0) & (delta % dil == 0) & (delta // dil <= BAND)
            if kp is not kpos:
                ok &= (np.arange(NEW_PAD) < dec_seq)[None, :]
            bucket = _t5_bucket(np.maximum(delta, 0))
            bl.append(rel_bias[:, g].astype(F32)[bucket, h_of_row[:, None]])
            vl.append(jnp.asarray(ok.astype(np.float32)))
    return (jnp.concatenate(biases, axis=1), jnp.concatenate(valids, axis=1),
            jnp.concatenate(nb, axis=1), jnp.concatenate(nv, axis=1))


def _attn_sample_kernel(q_ref, kvn_ref, b0_ref, b1_ref, b2_ref, bias_ref, valid_ref, nbias_ref, nvalid_ref,
                        o_ref, *, dec_seq, buf_rows):
    rows = dec_seq * HEADS
    lane_head = lax.broadcasted_iota(jnp.int32, (HEADS, ATTN_OUT), 1) // HEAD_DIM
    sub_head = lax.broadcasted_iota(jnp.int32, (HEADS, ATTN_OUT), 0)
    head_mask = lane_head == sub_head
    q = q_ref[0].astype(F32)
    kvn = kvn_ref[0].astype(F32)
    pad = jnp.zeros((NEW_PAD - dec_seq, 2 * ATTN_OUT), F32)

    parts = []
    off = 0
    for g in range(N_GROUPS):
        qg = q[:, g * ATTN_OUT:(g + 1) * ATTN_OUT]
        qe = jnp.concatenate(
            [jnp.where(head_mask, jnp.broadcast_to(qg[t:t + 1, :], (HEADS, ATTN_OUT)), 0.0)
             for t in range(dec_seq)], axis=0).astype(BF16)
        if g == 0:
            buf = b0_ref[...]
        elif g == 1:
            buf = b1_ref[...]
        else:
            wide = b2_ref[...]
            buf = jnp.concatenate([wide[:, r * 2 * ATTN_OUT:(r + 1) * 2 * ATTN_OUT]
                                   for r in range(dec_seq)], axis=0)
        buf = buf.astype(BF16)
        nk = buf.shape[0]
        new = jnp.concatenate([kvn[:, g * 2 * ATTN_OUT:(g + 1) * 2 * ATTN_OUT], pad], axis=0).astype(BF16)
        for kv, bias, valid in (
                (buf, bias_ref[:, off:off + nk], valid_ref[:, off:off + nk]),
                (new, nbias_ref[:, g * NEW_PAD:(g + 1) * NEW_PAD], nvalid_ref[:, g * NEW_PAD:(g + 1) * NEW_PAD])):
            s = lax.dot_general(qe, kv[:, :ATTN_OUT], (((1,), (1,)), ((), ())), preferred_element_type=F32)
            s = jnp.where(valid > 0.0, s + bias, NEG_INF)
            parts.append((s, kv[:, ATTN_OUT:]))
        off += nk

    mx = None
    for s, _ in parts:
        pm = jnp.max(s, axis=-1, keepdims=True)
        mx = pm if mx is None else jnp.maximum(mx, pm)
    den = jnp.zeros((rows, 1), F32)
    acc = jnp.zeros((rows, ATTN_OUT), F32)
    for s, v in parts:
        e = jnp.exp(s - mx)
        den = den + jnp.sum(e, axis=-1, keepdims=True)
        acc = acc + jnp.dot(e.astype(BF16), v, preferred_element_type=F32)
    acc = acc / den
    out = [jnp.sum(jnp.where(head_mask, acc[t * HEADS:(t + 1) * HEADS, :], 0.0), axis=0, keepdims=True)
           for t in range(dec_seq)]
    o_ref[0] = jnp.concatenate(out, axis=0)


def _attn_sample(q, kvb, caches, li, tables, n, dec_seq):
    buf_rows = tuple(c.shape[2] for c in caches)
    kvw = 2 * ATTN_OUT
    c0 = caches[0].reshape(caches[0].shape[0], n, buf_rows[0], kvw)
    c1 = caches[1].reshape(caches[1].shape[0], n, buf_rows[1], kvw)
    c2 = caches[2].reshape(caches[2].shape[0], n, buf_rows[2] // 16, 16 * kvw)
    bias, valid, nbias, nvalid = tables
    rows = dec_seq * HEADS
    o = pl.pallas_call(
        functools.partial(_attn_sample_kernel, dec_seq=dec_seq, buf_rows=buf_rows),
        grid=(n,),
        in_specs=[
            pl.BlockSpec((1, dec_seq, Q_COLS), lambda i: (i, 0, 0)),
            pl.BlockSpec((1, dec_seq, KV_COLS), lambda i: (i, 0, 0)),
            pl.BlockSpec((None, None, buf_rows[0], kvw), lambda i: (li, i, 0, 0)),
            pl.BlockSpec((None, None, buf_rows[1], kvw), lambda i: (li, i, 0, 0)),
            pl.BlockSpec((None, None, buf_rows[2] // 16, dec_seq * kvw), lambda i: (li, i, 0, 0)),
            _const_spec(bias.shape), _const_spec(valid.shape),
            _const_spec(nbias.shape), _const_spec(nvalid.shape),
        ],
        out_specs=pl.BlockSpec((1, dec_seq, ATTN_OUT), lambda i: (i, 0, 0)),
        out_shape=jax.ShapeDtypeStruct((n, dec_seq, ATTN_OUT), F32),
        compiler_params=_cparams("parallel"),
        name="attn_sample",
    )(q.reshape(n, dec_seq, Q_COLS), kvb.reshape(n, dec_seq, KV_COLS), c0, c1, c2, bias, valid, nbias, nvalid)
    return o.reshape(n * dec_seq, ATTN_OUT)


HALO = 16


def _pool_prompt_kernel(x_ref, g_ref, w_ref, sc_ref, o_ref, st_ref, hp_ref, *, tt):
    i = pl.program_id(1)

    @pl.when(i == 0)
    def _():
        hp_ref[0:HALO, :] = jnp.zeros((HALO, D_MODEL), F32)

    x = x_ref[0]
    h = _rms(x, g_ref[...])
    hp_ref[HALO:HALO + tt, :] = h
    pos1 = (i * tt + 1 + lax.broadcasted_iota(jnp.int32, (tt, 1), 0)).astype(F32)
    ys = []
    for g, w in enumerate(POOL_WINDOWS):
        cols = slice(g * POOL_CH, (g + 1) * POOL_CH)
        hg = h[:, cols]
        win = hg
        for j in range(1, w):
            win = win + hp_ref[HALO - j:HALO - j + tt, cols]
        p = win / jnp.minimum(float(w), pos1) - hg
        ys.append(jnp.dot(p.astype(BF16), w_ref[g], preferred_element_type=F32))
    o_ref[0] = x + jnp.concatenate(ys, axis=1) * sc_ref[...]
    tail = hp_ref[tt:tt + HALO, :]
    hp_ref[0:HALO, :] = tail

    @pl.when(i == pl.num_programs(1) - 1)
    def _():
        st_ref[0] = tail


def _pool_prompt(x, g, w_pool, scale, batch, seq):
    tt = min(ROW_TILE, seq)
    out, st = pl.pallas_call(
        functools.partial(_pool_prompt_kernel, tt=tt),
        grid=(batch, seq // tt),
        in_specs=[
            pl.BlockSpec((1, tt, D_MODEL), lambda b, i: (b, i, 0)),
            _const_spec((1, D_MODEL)),
            _const_spec(w_pool.shape),
            _const_spec((1, D_MODEL)),
        ],
        out_specs=[
            pl.BlockSpec((1, tt, D_MODEL), lambda b, i: (b, i, 0)),
            pl.BlockSpec((1, HALO, D_MODEL), lambda b, i: (b, 0, 0)),
        ],
        out_shape=[
            jax.ShapeDtypeStruct((batch, seq, D_MODEL), F32),
            jax.ShapeDtypeStruct((batch, HALO, D_MODEL), F32),
        ],
        scratch_shapes=[pltpu.VMEM((HALO + tt, D_MODEL), F32)],
        compiler_params=_cparams("parallel", "arbitrary"),
        name="pool_prompt",
    )(x.reshape(batch, seq, D_MODEL), g.reshape(1, D_MODEL), w_pool, scale.reshape(1, D_MODEL))
    return out.reshape(batch * seq, D_MODEL), st[:, HALO - POOL_STATE:, :]


def _pool_sample_kernel(x_ref, st_ref, g_ref, w_ref, sc_ref, o_ref, nst_ref, *, dec_seq, past_len):
    chunk = lambda ref, k: ref[:, k * D_MODEL:(k + 1) * D_MODEL]
    xs = [chunk(x_ref, t) for t in range(dec_seq)]
    hs = [_rms(x, g_ref[...]) for x in xs]
    rows = [chunk(st_ref, k) for k in range(POOL_STATE)] + hs
    for t in range(dec_seq):
        ys = []
        for g, w in enumerate(POOL_WINDOWS):
            cols = slice(g * POOL_CH, (g + 1) * POOL_CH)
            win = rows[POOL_STATE + t][:, cols]
            for j in range(1, w):
                win = win + rows[POOL_STATE + t - j][:, cols]
            p = win / float(min(w, past_len + t + 1)) - hs[t][:, cols]
            ys.append(jnp.dot(p.astype(BF16), w_ref[g], preferred_element_type=F32))
        o_ref[:, t * D_MODEL:(t + 1) * D_MODEL] = xs[t] + jnp.concatenate(ys, axis=1) * sc_ref[...]
    new_rows = rows[-POOL_STATE:]
    for k in range(POOL_STATE):
        nst_ref[:, k * D_MODEL:(k + 1) * D_MODEL] = new_rows[k]


def _pool_sample(x, state_pool, li, g, w_pool, scale, n, dec_seq, past_len):
    bn = min(32, n)
    st = state_pool.reshape(state_pool.shape[0], n, POOL_STATE * D_MODEL)
    out, nst = pl.pallas_call(
        functools.partial(_pool_sample_kernel, dec_seq=dec_seq, past_len=past_len),
        grid=(n // bn,),
        in_specs=[
            pl.BlockSpec((bn, dec_seq * D_MODEL), lambda i: (i, 0)),
            pl.BlockSpec((None, bn, POOL_STATE * D_MODEL), lambda i: (li, i, 0)),
            _const_spec((1, D_MODEL)),
            _const_spec(w_pool.shape),
            _const_spec((1, D_MODEL)),
        ],
        out_specs=[
            pl.BlockSpec((bn, dec_seq * D_MODEL), lambda i: (i, 0)),
            pl.BlockSpec((bn, POOL_STATE * D_MODEL), lambda i: (i, 0)),
        ],
        out_shape=[
            jax.ShapeDtypeStruct((n, dec_seq * D_MODEL), F32),
            jax.ShapeDtypeStruct((n, POOL_STATE * D_MODEL), F32),
        ],
        compiler_params=_cparams("parallel"),
        name="pool_sample",
    )(x.reshape(n, dec_seq * D_MODEL), st, g.reshape(1, D_MODEL), w_pool, scale.reshape(1, D_MODEL))
    return out.reshape(n * dec_seq, D_MODEL), nst.reshape(n, POOL_STATE, D_MODEL)


def _permute_qkv_weight(w):
    w = w.reshape(D_MODEL, 3, N_GROUPS, ATTN_OUT)
    q = w[:, 0].reshape(D_MODEL, Q_COLS)
    kv = jnp.stack([w[:, 1], w[:, 2]], axis=2).reshape(D_MODEL, KV_COLS)
    return jnp.concatenate([q, kv], axis=1)


def kernel(x_prompt, x_sample, state_pool, cache_kv_w128, cache_kv_w512, cache_kv_w2048, rel_bias, norm_mix,
           norm_ffn, norm_final, w_pool, pool_scale, w_qkv, w_o, w_up, w_down):
    batch, seq, _ = x_prompt.shape
    n, dec_seq, _ = x_sample.shape
    depth = norm_mix.shape[0]
    caches = (cache_kv_w128, cache_kv_w512, cache_kv_w2048)
    past_len = PAST_LEN

    xp = x_prompt.reshape(batch * seq, D_MODEL)
    xs = x_sample.reshape(n * dec_seq, D_MODEL)
    w_up_b = w_up.astype(BF16)
    w_down_b = w_down.astype(BF16)
    w_pool_b = w_pool.astype(BF16)
    w_o_b = w_o.astype(BF16)
    w_qkv_b = jax.vmap(_permute_qkv_weight)(w_qkv).astype(BF16)
    prompt_bias = [_prompt_bias(rel_bias[:, g], dil) for g, (_, dil) in enumerate(ATTN_GROUPS)]
    sample_tables = _sample_tables(rel_bias, dec_seq, tuple(c.shape[2] for c in caches))

    pool_p, pool_s = [], []
    kv_p = [[] for _ in ATTN_GROUPS]
    kv_s = [[] for _ in ATTN_GROUPS]
    for i in range(depth):
        li = i // 2
        if i % 2 == 0:
            xp, sp = _pool_prompt(xp, norm_mix[i], w_pool_b[li], pool_scale[li], batch, seq)
            xs, ss = _pool_sample(xs, state_pool, li, norm_mix[i], w_pool_b[li], pool_scale[li], n, dec_seq,
                                  past_len)
            pool_p.append(sp)
            pool_s.append(ss)
        else:
            qp, kvbp, kvfp = _qkv(xp, norm_mix[i], w_qkv_b[li])
            qs, kvbs, kvfs = _qkv(xs, norm_mix[i], w_qkv_b[li])
            os, lses = [], []
            for g, (_, dil) in enumerate(ATTN_GROUPS):
                o, lse = _attn_prompt(qp, kvbp, prompt_bias[g], g, dil, batch, seq)
                os.append(o)
                lses.append(lse)
            xp = _wo_merge(os, lses, w_o_b[li], xp)
            o_s = _attn_sample(qs, kvbs, caches, li, sample_tables, n, dec_seq)
            xs = _wo(o_s, w_o_b[li], xs)
            kvfp = kvfp.reshape(batch, seq, N_GROUPS, 2, HEADS, HEAD_DIM)
            kvfs = kvfs.reshape(n, dec_seq, N_GROUPS, 2, HEADS, HEAD_DIM)
            for g, (win, _) in enumerate(ATTN_GROUPS):
                kv_p[g].append(kvfp[:, seq - min(win, seq):, g])
                kv_s[g].append(kvfs[:, :, g])
        last = i == depth - 1
        xp = _mlp(xp, norm_ffn[i], w_up_b[i], w_down_b[i], norm_final, last)
        xs = _mlp(xs, norm_ffn[i], w_up_b[i], w_down_b[i], norm_final, last)
    return (xp.reshape(batch, seq, D_MODEL), xs.reshape(n, dec_seq, D_MODEL),
            jnp.stack(pool_p), jnp.stack(pool_s),
            jnp.stack(kv_p[0]), jnp.stack(kv_s[0]),
            jnp.stack(kv_p[1]), jnp.stack(kv_s[1]),
            jnp.stack(kv_p[2]), jnp.stack(kv_s[2]))
```

```python
import functools

import numpy as np
import jax
import jax.numpy as jnp
from jax import lax
from jax.experimental import pallas as pl
from jax.experimental.pallas import tpu as pltpu

F32 = jnp.float32
BF16 = jnp.bfloat16

D_MODEL = 1024
D_FF = 4 * D_MODEL
POOL_WINDOWS = (2, 4, 8, 16)
POOL_CH = D_MODEL // len(POOL_WINDOWS)
POOL_STATE = max(POOL_WINDOWS) - 1
ATTN_GROUPS = ((128, 1), (512, 4), (2048, 16))
N_GROUPS = len(ATTN_GROUPS)
HEADS = 8
HEAD_DIM = 64
ATTN_OUT = HEADS * HEAD_DIM
BAND = 128
N_BUCKETS = 32
MAX_EXACT = N_BUCKETS // 2
REL_MAX_DIST = 2048
PAST_LEN = 2048
RMS_EPS = 1e-6
NEG_INF = -1e30

VMEM_LIMIT_BYTES = 56 * 1024 * 1024
ROW_TILE = 512
FF_CHUNK = 1024


def _cparams(*sem):
    return pltpu.CompilerParams(dimension_semantics=sem, vmem_limit_bytes=VMEM_LIMIT_BYTES)


def _rms(x, g):
    ms = jnp.mean(x * x, axis=-1, keepdims=True)
    return x * lax.rsqrt(ms + RMS_EPS) * g


def _t5_bucket(dist):
    n = np.maximum(np.asarray(dist), 0)
    large = MAX_EXACT + (np.log(np.maximum(n, 1) / MAX_EXACT) / np.log(REL_MAX_DIST / MAX_EXACT)
                         * (N_BUCKETS - MAX_EXACT)).astype(np.int64)
    large = np.minimum(large, N_BUCKETS - 1)
    return np.where(n < MAX_EXACT, n, large).astype(np.int32)


def _const_spec(shape):
    nd = len(shape)
    return pl.BlockSpec(shape, lambda *_: (0,) * nd, pipeline_mode=pl.Buffered(1))


def _mlp_kernel(x_ref, g_ref, wu_ref, wd_ref, gf_ref, o_ref, *, final_norm):
    x = x_ref[...]
    h = _rms(x, g_ref[...]).astype(BF16)
    acc = x
    for c in range(D_FF // FF_CHUNK):
        cols = slice(c * FF_CHUNK, (c + 1) * FF_CHUNK)
        u = jnp.dot(h, wu_ref[:, cols], preferred_element_type=F32)
        a = jnp.square(jnp.maximum(u, 0.0)).astype(BF16)
        acc = acc + jnp.dot(a, wd_ref[cols, :], preferred_element_type=F32)
    if final_norm:
        acc = _rms(acc, gf_ref[...])
    o_ref[...] = acc


def _mlp(x, g, w_up, w_down, g_final, final_norm):
    m = x.shape[0]
    tm = min(ROW_TILE, m)
    return pl.pallas_call(
        functools.partial(_mlp_kernel, final_norm=final_norm),
        grid=(m // tm,),
        in_specs=[
            pl.BlockSpec((tm, D_MODEL), lambda i: (i, 0)),
            _const_spec((1, D_MODEL)),
            _const_spec((D_MODEL, D_FF)),
            _const_spec((D_FF, D_MODEL)),
            _const_spec((1, D_MODEL)),
        ],
        out_specs=pl.BlockSpec((tm, D_MODEL), lambda i: (i, 0)),
        out_shape=jax.ShapeDtypeStruct((m, D_MODEL), F32),
        compiler_params=_cparams("parallel"),
        name="mlp",
    )(x, g.reshape(1, D_MODEL), w_up, w_down, g_final.reshape(1, D_MODEL))


Q_COLS = N_GROUPS * ATTN_OUT
KV_COLS = 2 * N_GROUPS * ATTN_OUT


def _qkv_kernel(x_ref, g_ref, w_ref, q_ref, kvb_ref, kvf_ref):
    h = _rms(x_ref[...], g_ref[...]).astype(BF16)
    q = jnp.dot(h, w_ref[:, :Q_COLS], preferred_element_type=F32)
    q_ref[...] = (q * (HEAD_DIM ** -0.5)).astype(BF16)
    for g in range(N_GROUPS):
        cols = slice(g * 2 * ATTN_OUT, (g + 1) * 2 * ATTN_OUT)
        kv = jnp.dot(h, w_ref[:, Q_COLS + cols.start:Q_COLS + cols.stop], preferred_element_type=F32)
        kvf_ref[:, cols] = kv
        kvb_ref[:, cols] = kv.astype(BF16)


def _qkv(x, g, w):
    m = x.shape[0]
    tm = min(ROW_TILE, m)
    row = lambda i: (i, 0)
    return pl.pallas_call(
        _qkv_kernel,
        grid=(m // tm,),
        in_specs=[
            pl.BlockSpec((tm, D_MODEL), row),
            _const_spec((1, D_MODEL)),
            _const_spec((D_MODEL, Q_COLS + KV_COLS)),
        ],
        out_specs=[
            pl.BlockSpec((tm, Q_COLS), row),
            pl.BlockSpec((tm, KV_COLS), row),
            pl.BlockSpec((tm, KV_COLS), row),
        ],
        out_shape=[
            jax.ShapeDtypeStruct((m, Q_COLS), BF16),
            jax.ShapeDtypeStruct((m, KV_COLS), BF16),
            jax.ShapeDtypeStruct((m, KV_COLS), F32),
        ],
        compiler_params=_cparams("parallel"),
        name="qkv",
    )(x, g.reshape(1, D_MODEL), w)


def _attn_prompt_kernel(q_ref, kvp_ref, kvc_ref, bias_ref, o_ref, lse_ref):
    n = pl.program_id(2)
    q = q_ref[0]
    kv = jnp.concatenate([kvp_ref[0], kvc_ref[0]], axis=0)
    qi = lax.broadcasted_iota(jnp.int32, (BAND, 2 * BAND), 0)
    kj = lax.broadcasted_iota(jnp.int32, (BAND, 2 * BAND), 1)
    m = BAND + qi - kj
    first_key = jnp.where(n > 0, 0, BAND)
    valid = (m >= 0) & (m <= BAND) & (kj >= first_key)
    for h in range(HEADS):
        lo, hi = h * HEAD_DIM, (h + 1) * HEAD_DIM
        s = lax.dot_general(q[:, lo:hi], kv[:, lo:hi], (((1,), (1,)), ((), ())),
                            preferred_element_type=F32)
        s = jnp.where(valid, s + bias_ref[h], NEG_INF)
        mx = jnp.max(s, axis=-1, keepdims=True)
        e = jnp.exp(s - mx)
        den = jnp.sum(e, axis=-1, keepdims=True)
        o = jnp.dot(e.astype(BF16), kv[:, ATTN_OUT + lo:ATTN_OUT + hi], preferred_element_type=F32)
        o_ref[0, :, lo:hi] = o / den
        lse_ref[0, 0, :, h:h + 1] = mx + jnp.log(den)


def _attn_prompt(q, kvb, bias, g, dil, batch, seq):
    sub = seq // dil
    nb = sub // BAND
    qv = q.reshape(batch, sub, dil * Q_COLS)
    kvv = kvb.reshape(batch, sub, dil * KV_COLS)
    o, lse = pl.pallas_call(
        _attn_prompt_kernel,
        grid=(batch, dil, nb),
        in_specs=[
            pl.BlockSpec((1, BAND, ATTN_OUT), lambda b, r, n: (b, n, r * N_GROUPS + g)),
            pl.BlockSpec((1, BAND, 2 * ATTN_OUT), lambda b, r, n: (b, jnp.maximum(n - 1, 0), r * N_GROUPS + g)),
            pl.BlockSpec((1, BAND, 2 * ATTN_OUT), lambda b, r, n: (b, n, r * N_GROUPS + g)),
            _const_spec((HEADS, BAND, 2 * BAND)),
        ],
        out_specs=[
            pl.BlockSpec((1, BAND, ATTN_OUT), lambda b, r, n: (b, n, r)),
            pl.BlockSpec((1, 1, BAND, HEADS), lambda b, r, n: (b, r, n, 0)),
        ],
        out_shape=[
            jax.ShapeDtypeStruct((batch, sub, dil * ATTN_OUT), F32),
            jax.ShapeDtypeStruct((batch, dil, sub, HEADS), F32),
        ],
        compiler_params=_cparams("parallel", "parallel", "arbitrary"),
        name="attn_prompt_d%d" % dil,
    )(qv, kvv, kvv, bias)
    o = o.reshape(batch * seq, ATTN_OUT)
    lse = jnp.transpose(lse, (0, 2, 1, 3)).reshape(batch * seq, HEADS)
    return o, lse


def _prompt_bias(rel_bias_g, dil):
    m = BAND + np.arange(BAND)[:, None] - np.arange(2 * BAND)[None, :]
    bias = rel_bias_g.astype(F32)[_t5_bucket(np.clip(m, 0, BAND) * dil)]
    return jnp.transpose(bias, (2, 0, 1))


def _wo_merge_kernel(o0_ref, o1_ref, o2_ref, l0_ref, l1_ref, l2_ref, e_ref, w_ref, x_ref, out_ref):
    ls = (l0_ref[...], l1_ref[...], l2_ref[...])
    mx = jnp.maximum(jnp.maximum(ls[0], ls[1]), ls[2])
    es = [jnp.exp(l - mx) for l in ls]
    inv = 1.0 / (es[0] + es[1] + es[2])
    acc = None
    for e, o_ref in zip(es, (o0_ref, o1_ref, o2_ref)):
        wt = e * inv
        hi = wt.astype(BF16)
        lo = (wt - hi.astype(F32)).astype(BF16)
        wexp = (jnp.dot(hi, e_ref[...], preferred_element_type=F32)
                + jnp.dot(lo, e_ref[...], preferred_element_type=F32))
        term = wexp * o_ref[...]
        acc = term if acc is None else acc + term
    out_ref[...] = x_ref[...] + jnp.dot(acc.astype(BF16), w_ref[...], preferred_element_type=F32)


def _wo_merge(os, lses, w_o, x):
    m = x.shape[0]
    tm = min(ROW_TILE, m)
    row = lambda i: (i, 0)
    expand = jnp.asarray(np.repeat(np.eye(HEADS, dtype=np.float32), HEAD_DIM, axis=1), BF16)
    return pl.pallas_call(
        _wo_merge_kernel,
        grid=(m // tm,),
        in_specs=[pl.BlockSpec((tm, ATTN_OUT), row)] * 3 + [pl.BlockSpec((tm, HEADS), row)] * 3 + [
            _const_spec((HEADS, ATTN_OUT)),
            _const_spec((ATTN_OUT, D_MODEL)),
            pl.BlockSpec((tm, D_MODEL), row),
        ],
        out_specs=pl.BlockSpec((tm, D_MODEL), row),
        out_shape=jax.ShapeDtypeStruct((m, D_MODEL), F32),
        compiler_params=_cparams("parallel"),
        name="wo_merge",
    )(*os, *lses, expand, w_o, x)


def _wo_kernel(o_ref, w_ref, x_ref, out_ref):
    out_ref[...] = x_ref[...] + jnp.dot(o_ref[...].astype(BF16), w_ref[...], preferred_element_type=F32)


def _wo(o, w_o, x):
    m = x.shape[0]
    tm = min(ROW_TILE, m)
    row = lambda i: (i, 0)
    return pl.pallas_call(
        _wo_kernel,
        grid=(m // tm,),
        in_specs=[pl.BlockSpec((tm, ATTN_OUT), row), _const_spec((ATTN_OUT, D_MODEL)),
                  pl.BlockSpec((tm, D_MODEL), row)],
        out_specs=pl.BlockSpec((tm, D_MODEL), row),
        out_shape=jax.ShapeDtypeStruct((m, D_MODEL), F32),
        compiler_params=_cparams("parallel"),
        name="wo",
    )(o, w_o, x)


WIDE_DIL = ATTN_GROUPS[2][1]


def _sample_tables(rel_bias, dec_seq, buf_rows):
    t_of_row = np.repeat(np.arange(dec_seq), HEADS)
    h_of_row = np.tile(np.arange(HEADS), dec_seq)
    biases, valids = [], []
    nb, nv = [], []
    for g, (_, dil) in enumerate(ATTN_GROUPS):
        wb = buf_rows[g]
        qpos = wb + t_of_row
        if g == 2:
            kk = np.arange(wb // WIDE_DIL * dec_seq)
            kpos = WIDE_DIL * (kk // dec_seq) + kk % dec_seq
        else:
            kpos = np.arange(wb)
        for kp, bl, vl in ((kpos, biases, valids), (wb + np.arange(dec_seq), nb, nv)):
            kp_col = np.repeat(kp, HEADS)
            h_col = np.tile(np.arange(HEADS), kp.shape[0])
            delta = qpos[:, None] - kp_col[None, :]
            ok = (delta >= 0) & (delta % dil == 0) & (delta // dil <= BAND)
            ok &= h_of_row[:, None] == h_col[None, :]
            bucket = _t5_bucket(np.maximum(delta, 0))
            bl.append(rel_bias[:, g].astype(F32)[bucket, h_of_row[:, None]])
            vl.append(jnp.asarray(ok.astype(np.float32)))
    return (jnp.concatenate(biases, axis=1), jnp.concatenate(valids, axis=1),
            jnp.concatenate(nb, axis=1), jnp.concatenate(nv, axis=1))


def _attn_sample_kernel(q_ref, kvn_ref, b0_ref, b1_ref, b2_ref, bias_ref, valid_ref, nbias_ref, nvalid_ref,
                        o_ref, *, dec_seq, buf_rows):
    rows = dec_seq * HEADS
    parts = []
    off = 0
    for g, b_ref in enumerate((b0_ref, b1_ref, b2_ref)):
        qg = q_ref[0, g]
        if g == 2:
            k = b_ref[:, :, 0]
            v = b_ref[:, :, 1]
        else:
            k = b_ref[:, 0]
            v = b_ref[:, 1]
        k = k.reshape(-1, HEAD_DIM).astype(BF16)
        v = v.reshape(-1, HEAD_DIM).astype(BF16)
        nk = k.shape[0]
        for kk, vv, bias, valid in (
                (k, v, bias_ref[:, off:off + nk], valid_ref[:, off:off + nk]),
                (kvn_ref[0, g, 0], kvn_ref[0, g, 1],
                 nbias_ref[:, g * rows:(g + 1) * rows], nvalid_ref[:, g * rows:(g + 1) * rows])):
            s = lax.dot_general(qg, kk, (((1,), (1,)), ((), ())), preferred_element_type=F32)
            s = jnp.where(valid > 0.0, s + bias, NEG_INF)
            parts.append((s, vv))
        off += nk

    mx = None
    for s, _ in parts:
        pm = jnp.max(s, axis=-1, keepdims=True)
        mx = pm if mx is None else jnp.maximum(mx, pm)
    den = jnp.zeros((rows, 1), F32)
    acc = jnp.zeros((rows, HEAD_DIM), F32)
    for s, v in parts:
        e = jnp.exp(s - mx)
        den = den + jnp.sum(e, axis=-1, keepdims=True)
        acc = acc + jnp.dot(e.astype(BF16), v, preferred_element_type=F32)
    o_ref[0] = acc / den


def _attn_sample(q, kvb, caches, li, tables, n, dec_seq):
    buf_rows = tuple(c.shape[2] for c in caches)
    assert buf_rows[2] % WIDE_DIL == 0 and dec_seq <= WIDE_DIL
    rows = dec_seq * HEADS
    q4 = q.reshape(n, dec_seq, N_GROUPS, HEADS, HEAD_DIM).transpose(0, 2, 1, 3, 4)
    q4 = q4.reshape(n, N_GROUPS, rows, HEAD_DIM)
    kvn = kvb.reshape(n, dec_seq, N_GROUPS, 2, HEADS, HEAD_DIM).transpose(0, 2, 3, 1, 4, 5)
    kvn = kvn.reshape(n, N_GROUPS, 2, rows, HEAD_DIM)
    c2 = caches[2].reshape(caches[2].shape[:2] + (buf_rows[2] // WIDE_DIL, WIDE_DIL, 2, HEADS, HEAD_DIM))
    bias, valid, nbias, nvalid = tables
    o = pl.pallas_call(
        functools.partial(_attn_sample_kernel, dec_seq=dec_seq, buf_rows=buf_rows),
        grid=(n,),
        in_specs=[
            pl.BlockSpec((1, N_GROUPS, rows, HEAD_DIM), lambda i: (i, 0, 0, 0)),
            pl.BlockSpec((1, N_GROUPS, 2, rows, HEAD_DIM), lambda i: (i, 0, 0, 0, 0)),
            pl.BlockSpec((None, None, buf_rows[0], 2, HEADS, HEAD_DIM), lambda i: (li, i, 0, 0, 0, 0)),
            pl.BlockSpec((None, None, buf_rows[1], 2, HEADS, HEAD_DIM), lambda i: (li, i, 0, 0, 0, 0)),
            pl.BlockSpec((None, None, buf_rows[2] // WIDE_DIL, dec_seq, 2, HEADS, HEAD_DIM),
                         lambda i: (li, i, 0, 0, 0, 0, 0)),
            _const_spec(bias.shape), _const_spec(valid.shape),
            _const_spec(nbias.shape), _const_spec(nvalid.shape),
        ],
        out_specs=pl.BlockSpec((1, rows, HEAD_DIM), lambda i: (i, 0, 0)),
        out_shape=jax.ShapeDtypeStruct((n, rows, HEAD_DIM), F32),
        compiler_params=_cparams("parallel"),
        name="attn_sample",
    )(q4, kvn, caches[0], caches[1], c2, bias, valid, nbias, nvalid)
    return o.reshape(n * dec_seq, ATTN_OUT)


HALO = 16


def _pool_prompt_kernel(x_ref, g_ref, w_ref, sc_ref, o_ref, st_ref, hp_ref, *, tt):
    i = pl.program_id(1)

    @pl.when(i == 0)
    def _():
        hp_ref[0:HALO, :] = jnp.zeros((HALO, D_MODEL), F32)

    x = x_ref[0]
    h = _rms(x, g_ref[...])
    hp_ref[HALO:HALO + tt, :] = h
    pos1 = (i * tt + 1 + lax.broadcasted_iota(jnp.int32, (tt, 1), 0)).astype(F32)
    ys = []
    for g, w in enumerate(POOL_WINDOWS):
        cols = slice(g * POOL_CH, (g + 1) * POOL_CH)
        hg = h[:, cols]
        win = hg
        for j in range(1, w):
            win = win + hp_ref[HALO - j:HALO - j + tt, cols]
        p = win / jnp.minimum(float(w), pos1) - hg
        ys.append(jnp.dot(p.astype(BF16), w_ref[g], preferred_element_type=F32))
    o_ref[0] = x + jnp.concatenate(ys, axis=1) * sc_ref[...]
    tail = hp_ref[tt:tt + HALO, :]
    hp_ref[0:HALO, :] = tail

    @pl.when(i == pl.num_programs(1) - 1)
    def _():
        st_ref[0] = tail


def _pool_prompt(x, g, w_pool, scale, batch, seq):
    tt = min(ROW_TILE, seq)
    out, st = pl.pallas_call(
        functools.partial(_pool_prompt_kernel, tt=tt),
        grid=(batch, seq // tt),
        in_specs=[
            pl.BlockSpec((1, tt, D_MODEL), lambda b, i: (b, i, 0)),
            _const_spec((1, D_MODEL)),
            _const_spec(w_pool.shape),
            _const_spec((1, D_MODEL)),
        ],
        out_specs=[
            pl.BlockSpec((1, tt, D_MODEL), lambda b, i: (b, i, 0)),
            pl.BlockSpec((1, HALO, D_MODEL), lambda b, i: (b, 0, 0)),
        ],
        out_shape=[
            jax.ShapeDtypeStruct((batch, seq, D_MODEL), F32),
            jax.ShapeDtypeStruct((batch, HALO, D_MODEL), F32),
        ],
        scratch_shapes=[pltpu.VMEM((HALO + tt, D_MODEL), F32)],
        compiler_params=_cparams("parallel", "arbitrary"),
        name="pool_prompt",
    )(x.reshape(batch, seq, D_MODEL), g.reshape(1, D_MODEL), w_pool, scale.reshape(1, D_MODEL))
    return out.reshape(batch * seq, D_MODEL), st[:, HALO - POOL_STATE:, :]


def _pool_sample_kernel(x_ref, st_ref, g_ref, w_ref, sc_ref, o_ref, nst_ref, *, dec_seq, past_len):
    chunk = lambda ref, k: ref[:, k * D_MODEL:(k + 1) * D_MODEL]
    xs = [chunk(x_ref, t) for t in range(dec_seq)]
    hs = [_rms(x, g_ref[...]) for x in xs]
    rows = [chunk(st_ref, k) for k in range(POOL_STATE)] + hs
    for t in range(dec_seq):
        ys = []
        for g, w in enumerate(POOL_WINDOWS):
            cols = slice(g * POOL_CH, (g + 1) * POOL_CH)
            win = rows[POOL_STATE + t][:, cols]
            for j in range(1, w):
                win = win + rows[POOL_STATE + t - j][:, cols]
            p = win / float(min(w, past_len + t + 1)) - hs[t][:, cols]
            ys.append(jnp.dot(p.astype(BF16), w_ref[g], preferred_element_type=F32))
        o_ref[:, t * D_MODEL:(t + 1) * D_MODEL] = xs[t] + jnp.concatenate(ys, axis=1) * sc_ref[...]
    new_rows = rows[-POOL_STATE:]
    for k in range(POOL_STATE):
        nst_ref[:, k * D_MODEL:(k + 1) * D_MODEL] = new_rows[k]


def _pool_sample(x, state_pool, li, g, w_pool, scale, n, dec_seq, past_len):
    bn = min(32, n)
    st = state_pool.reshape(state_pool.shape[0], n, POOL_STATE * D_MODEL)
    out, nst = pl.pallas_call(
        functools.partial(_pool_sample_kernel, dec_seq=dec_seq, past_len=past_len),
        grid=(n // bn,),
        in_specs=[
            pl.BlockSpec((bn, dec_seq * D_MODEL), lambda i: (i, 0)),
            pl.BlockSpec((None, bn, POOL_STATE * D_MODEL), lambda i: (li, i, 0)),
            _const_spec((1, D_MODEL)),
            _const_spec(w_pool.shape),
            _const_spec((1, D_MODEL)),
        ],
        out_specs=[
            pl.BlockSpec((bn, dec_seq * D_MODEL), lambda i: (i, 0)),
            pl.BlockSpec((bn, POOL_STATE * D_MODEL), lambda i: (i, 0)),
        ],
        out_shape=[
            jax.ShapeDtypeStruct((n, dec_seq * D_MODEL), F32),
            jax.ShapeDtypeStruct((n, POOL_STATE * D_MODEL), F32),
        ],
        compiler_params=_cparams("parallel"),
        name="pool_sample",
    )(x.reshape(n, dec_seq * D_MODEL), st, g.reshape(1, D_MODEL), w_pool, scale.reshape(1, D_MODEL))
    return out.reshape(n * dec_seq, D_MODEL), nst.reshape(n, POOL_STATE, D_MODEL)


def _permute_qkv_weight(w):
    w = w.reshape(D_MODEL, 3, N_GROUPS, ATTN_OUT)
    q = w[:, 0].reshape(D_MODEL, Q_COLS)
    kv = jnp.stack([w[:, 1], w[:, 2]], axis=2).reshape(D_MODEL, KV_COLS)
    return jnp.concatenate([q, kv], axis=1)


def kernel(x_prompt, x_sample, state_pool, cache_kv_w128, cache_kv_w512, cache_kv_w2048, rel_bias, norm_mix,
           norm_ffn, norm_final, w_pool, pool_scale, w_qkv, w_o, w_up, w_down):
    batch, seq, _ = x_prompt.shape
    n, dec_seq, _ = x_sample.shape
    depth = norm_mix.shape[0]
    caches = (cache_kv_w128, cache_kv_w512, cache_kv_w2048)
    past_len = PAST_LEN

    xp = x_prompt.reshape(batch * seq, D_MODEL)
    xs = x_sample.reshape(n * dec_seq, D_MODEL)
    w_up_b = w_up.astype(BF16)
    w_down_b = w_down.astype(BF16)
    w_pool_b = w_pool.astype(BF16)
    w_o_b = w_o.astype(BF16)
    w_qkv_b = jax.vmap(_permute_qkv_weight)(w_qkv).astype(BF16)
    prompt_bias = [_prompt_bias(rel_bias[:, g], dil) for g, (_, dil) in enumerate(ATTN_GROUPS)]
    sample_tables = _sample_tables(rel_bias, dec_seq, tuple(c.shape[2] for c in caches))

    pool_p, pool_s = [], []
    kv_p = [[] for _ in ATTN_GROUPS]
    kv_s = [[] for _ in ATTN_GROUPS]
    for i in range(depth):
        li = i // 2
        if i % 2 == 0:
            xp, sp = _pool_prompt(xp, norm_mix[i], w_pool_b[li], pool_scale[li], batch, seq)
            xs, ss = _pool_sample(xs, state_pool, li, norm_mix[i], w_pool_b[li], pool_scale[li], n, dec_seq,
                                  past_len)
            pool_p.append(sp)
            pool_s.append(ss)
        else:
            qp, kvbp, kvfp = _qkv(xp, norm_mix[i], w_qkv_b[li])
            qs, kvbs, kvfs = _qkv(xs, norm_mix[i], w_qkv_b[li])
            os, lses = [], []
            for g, (_, dil) in enumerate(ATTN_GROUPS):
                o, lse = _attn_prompt(qp, kvbp, prompt_bias[g], g, dil, batch, seq)
                os.append(o)
                lses.append(lse)
            xp = _wo_merge(os, lses, w_o_b[li], xp)
            o_s = _attn_sample(qs, kvbs, caches, li, sample_tables, n, dec_seq)
            xs = _wo(o_s, w_o_b[li], xs)
            kvfp = kvfp.reshape(batch, seq, N_GROUPS, 2, HEADS, HEAD_DIM)
            kvfs = kvfs.reshape(n, dec_seq, N_GROUPS, 2, HEADS, HEAD_DIM)
            for g, (win, _) in enumerate(ATTN_GROUPS):
                kv_p[g].append(kvfp[:, seq - min(win, seq):, g])
                kv_s[g].append(kvfs[:, :, g])
        last = i == depth - 1
        xp = _mlp(xp, norm_ffn[i], w_up_b[i], w_down_b[i], norm_final, last)
        xs = _mlp(xs, norm_ffn[i], w_up_b[i], w_down_b[i], norm_final, last)
    return (xp.reshape(batch, seq, D_MODEL), xs.reshape(n, dec_seq, D_MODEL),
            jnp.stack(pool_p), jnp.stack(pool_s),
            jnp.stack(kv_p[0]), jnp.stack(kv_s[0]),
            jnp.stack(kv_p[1]), jnp.stack(kv_s[1]),
            jnp.stack(kv_p[2]), jnp.stack(kv_s[2]))
```

```python
import functools

import numpy as np
import jax
import jax.numpy as jnp
from jax import lax
from jax.experimental import pallas as pl
from jax.experimental.pallas import tpu as pltpu

F32 = jnp.float32
BF16 = jnp.bfloat16

D_MODEL = 1024
D_FF = 4 * D_MODEL
POOL_WINDOWS = (2, 4, 8, 16)
POOL_CH = D_MODEL // len(POOL_WINDOWS)
POOL_STATE = max(POOL_WINDOWS) - 1
ATTN_GROUPS = ((128, 1), (512, 4), (2048, 16))
N_GROUPS = len(ATTN_GROUPS)
HEADS = 8
HEAD_DIM = 64
ATTN_OUT = HEADS * HEAD_DIM
BAND = 128
N_BUCKETS = 32
MAX_EXACT = N_BUCKETS // 2
REL_MAX_DIST = 2048
PAST_LEN = 2048
RMS_EPS = 1e-6
NEG_INF = -1e30

VMEM_LIMIT_BYTES = 56 * 1024 * 1024
ROW_TILE = 512
FF_CHUNK = 1024


def _cparams(*sem):
    return pltpu.CompilerParams(dimension_semantics=sem, vmem_limit_bytes=VMEM_LIMIT_BYTES)


def _rms(x, g):
    ms = jnp.mean(x * x, axis=-1, keepdims=True)
    return x * lax.rsqrt(ms + RMS_EPS) * g


def _t5_bucket(dist):
    n = np.maximum(np.asarray(dist), 0)
    large = MAX_EXACT + (np.log(np.maximum(n, 1) / MAX_EXACT) / np.log(REL_MAX_DIST / MAX_EXACT)
                         * (N_BUCKETS - MAX_EXACT)).astype(np.int64)
    large = np.minimum(large, N_BUCKETS - 1)
    return np.where(n < MAX_EXACT, n, large).astype(np.int32)


def _const_spec(shape):
    nd = len(shape)
    return pl.BlockSpec(shape, lambda *_: (0,) * nd, pipeline_mode=pl.Buffered(1))


def _mlp_kernel(x_ref, g_ref, wu_ref, wd_ref, gf_ref, o_ref, *, final_norm):
    x = x_ref[...]
    h = _rms(x, g_ref[...]).astype(BF16)
    acc = x
    for c in range(D_FF // FF_CHUNK):
        cols = slice(c * FF_CHUNK, (c + 1) * FF_CHUNK)
        u = jnp.dot(h, wu_ref[:, cols], preferred_element_type=F32)
        a = jnp.square(jnp.maximum(u, 0.0)).astype(BF16)
        acc = acc + jnp.dot(a, wd_ref[cols, :], preferred_element_type=F32)
    if final_norm:
        acc = _rms(acc, gf_ref[...])
    o_ref[...] = acc


def _mlp(x, g, w_up, w_down, g_final, final_norm):
    m = x.shape[0]
    tm = min(ROW_TILE, m)
    return pl.pallas_call(
        functools.partial(_mlp_kernel, final_norm=final_norm),
        grid=(m // tm,),
        in_specs=[
            pl.BlockSpec((tm, D_MODEL), lambda i: (i, 0)),
            _const_spec((1, D_MODEL)),
            _const_spec((D_MODEL, D_FF)),
            _const_spec((D_FF, D_MODEL)),
            _const_spec((1, D_MODEL)),
        ],
        out_specs=pl.BlockSpec((tm, D_MODEL), lambda i: (i, 0)),
        out_shape=jax.ShapeDtypeStruct((m, D_MODEL), F32),
        compiler_params=_cparams("parallel"),
        name="mlp",
    )(x, g.reshape(1, D_MODEL), w_up, w_down, g_final.reshape(1, D_MODEL))


Q_COLS = N_GROUPS * ATTN_OUT
KV_COLS = 2 * N_GROUPS * ATTN_OUT


def _qkv_kernel(x_ref, g_ref, w_ref, q_ref, kvb_ref, kvf_ref):
    h = _rms(x_ref[...], g_ref[...]).astype(BF16)
    q = jnp.dot(h, w_ref[:, :Q_COLS], preferred_element_type=F32)
    q_ref[...] = (q * (HEAD_DIM ** -0.5)).astype(BF16)
    for g in range(N_GROUPS):
        cols = slice(g * 2 * ATTN_OUT, (g + 1) * 2 * ATTN_OUT)
        kv = jnp.dot(h, w_ref[:, Q_COLS + cols.start:Q_COLS + cols.stop], preferred_element_type=F32)
        kvf_ref[:, cols] = kv
        kvb_ref[:, cols] = kv.astype(BF16)


def _qkv(x, g, w):
    m = x.shape[0]
    tm = min(ROW_TILE, m)
    row = lambda i: (i, 0)
    return pl.pallas_call(
        _qkv_kernel,
        grid=(m // tm,),
        in_specs=[
            pl.BlockSpec((tm, D_MODEL), row),
            _const_spec((1, D_MODEL)),
            _const_spec((D_MODEL, Q_COLS + KV_COLS)),
        ],
        out_specs=[
            pl.BlockSpec((tm, Q_COLS), row),
            pl.BlockSpec((tm, KV_COLS), row),
            pl.BlockSpec((tm, KV_COLS), row),
        ],
        out_shape=[
            jax.ShapeDtypeStruct((m, Q_COLS), BF16),
            jax.ShapeDtypeStruct((m, KV_COLS), BF16),
            jax.ShapeDtypeStruct((m, KV_COLS), F32),
        ],
        compiler_params=_cparams("parallel"),
        name="qkv",
    )(x, g.reshape(1, D_MODEL), w)


def _attn_prompt_kernel(q_ref, kvp_ref, kvc_ref, bias_ref, o_ref, lse_ref):
    n = pl.program_id(2)
    q = q_ref[0]
    kv = jnp.concatenate([kvp_ref[0], kvc_ref[0]], axis=0)
    qi = lax.broadcasted_iota(jnp.int32, (BAND, 2 * BAND), 0)
    kj = lax.broadcasted_iota(jnp.int32, (BAND, 2 * BAND), 1)
    m = BAND + qi - kj
    first_key = jnp.where(n > 0, 0, BAND)
    valid = (m >= 0) & (m <= BAND) & (kj >= first_key)
    for h in range(HEADS):
        lo, hi = h * HEAD_DIM, (h + 1) * HEAD_DIM
        s = lax.dot_general(q[:, lo:hi], kv[:, lo:hi], (((1,), (1,)), ((), ())),
                            preferred_element_type=F32)
        s = jnp.where(valid, s + bias_ref[h], NEG_INF)
        mx = jnp.max(s, axis=-1, keepdims=True)
        e = jnp.exp(s - mx)
        den = jnp.sum(e, axis=-1, keepdims=True)
        o = jnp.dot(e.astype(BF16), kv[:, ATTN_OUT + lo:ATTN_OUT + hi], preferred_element_type=F32)
        o_ref[0, :, lo:hi] = o / den
        lse_ref[0, 0, :, h:h + 1] = mx + jnp.log(den)


def _attn_prompt(q, kvb, bias, g, dil, batch, seq):
    sub = seq // dil
    nb = sub // BAND
    qv = q.reshape(batch, sub, dil * Q_COLS)
    kvv = kvb.reshape(batch, sub, dil * KV_COLS)
    o, lse = pl.pallas_call(
        _attn_prompt_kernel,
        grid=(batch, dil, nb),
        in_specs=[
            pl.BlockSpec((1, BAND, ATTN_OUT), lambda b, r, n: (b, n, r * N_GROUPS + g)),
            pl.BlockSpec((1, BAND, 2 * ATTN_OUT), lambda b, r, n: (b, jnp.maximum(n - 1, 0), r * N_GROUPS + g)),
            pl.BlockSpec((1, BAND, 2 * ATTN_OUT), lambda b, r, n: (b, n, r * N_GROUPS + g)),
            _const_spec((HEADS, BAND, 2 * BAND)),
        ],
        out_specs=[
            pl.BlockSpec((1, BAND, ATTN_OUT), lambda b, r, n: (b, n, r)),
            pl.BlockSpec((1, 1, BAND, HEADS), lambda b, r, n: (b, r, n, 0)),
        ],
        out_shape=[
            jax.ShapeDtypeStruct((batch, sub, dil * ATTN_OUT), F32),
            jax.ShapeDtypeStruct((batch, dil, sub, HEADS), F32),
        ],
        compiler_params=_cparams("parallel", "parallel", "arbitrary"),
        name="attn_prompt_d%d" % dil,
    )(qv, kvv, kvv, bias)
    o = o.reshape(batch * seq, ATTN_OUT)
    lse = jnp.transpose(lse, (0, 2, 1, 3)).reshape(batch * seq, HEADS)
    return o, lse


def _prompt_bias(rel_bias_g, dil):
    m = BAND + np.arange(BAND)[:, None] - np.arange(2 * BAND)[None, :]
    bias = rel_bias_g.astype(F32)[_t5_bucket(np.clip(m, 0, BAND) * dil)]
    return jnp.transpose(bias, (2, 0, 1))


def _wo_merge_kernel(o0_ref, o1_ref, o2_ref, l0_ref, l1_ref, l2_ref, e_ref, w_ref, x_ref, out_ref):
    ls = (l0_ref[...], l1_ref[...], l2_ref[...])
    mx = jnp.maximum(jnp.maximum(ls[0], ls[1]), ls[2])
    es = [jnp.exp(l - mx) for l in ls]
    inv = 1.0 / (es[0] + es[1] + es[2])
    acc = None
    for e, o_ref in zip(es, (o0_ref, o1_ref, o2_ref)):
        wt = e * inv
        hi = wt.astype(BF16)
        lo = (wt - hi.astype(F32)).astype(BF16)
        wexp = (jnp.dot(hi, e_ref[...], preferred_element_type=F32)
                + jnp.dot(lo, e_ref[...], preferred_element_type=F32))
        term = wexp * o_ref[...]
        acc = term if acc is None else acc + term
    out_ref[...] = x_ref[...] + jnp.dot(acc.astype(BF16), w_ref[...], preferred_element_type=F32)


def _wo_merge(os, lses, w_o, x):
    m = x.shape[0]
    tm = min(ROW_TILE, m)
    row = lambda i: (i, 0)
    expand = jnp.asarray(np.repeat(np.eye(HEADS, dtype=np.float32), HEAD_DIM, axis=1), BF16)
    return pl.pallas_call(
        _wo_merge_kernel,
        grid=(m // tm,),
        in_specs=[pl.BlockSpec((tm, ATTN_OUT), row)] * 3 + [pl.BlockSpec((tm, HEADS), row)] * 3 + [
            _const_spec((HEADS, ATTN_OUT)),
            _const_spec((ATTN_OUT, D_MODEL)),
            pl.BlockSpec((tm, D_MODEL), row),
        ],
        out_specs=pl.BlockSpec((tm, D_MODEL), row),
        out_shape=jax.ShapeDtypeStruct((m, D_MODEL), F32),
        compiler_params=_cparams("parallel"),
        name="wo_merge",
    )(*os, *lses, expand, w_o, x)


def _wo_kernel(o_ref, w_ref, x_ref, out_ref):
    out_ref[...] = x_ref[...] + jnp.dot(o_ref[...].astype(BF16), w_ref[...], preferred_element_type=F32)


def _wo(o, w_o, x):
    m = x.shape[0]
    tm = min(ROW_TILE, m)
    row = lambda i: (i, 0)
    return pl.pallas_call(
        _wo_kernel,
        grid=(m // tm,),
        in_specs=[pl.BlockSpec((tm, ATTN_OUT), row), _const_spec((ATTN_OUT, D_MODEL)),
                  pl.BlockSpec((tm, D_MODEL), row)],
        out_specs=pl.BlockSpec((tm, D_MODEL), row),
        out_shape=jax.ShapeDtypeStruct((m, D_MODEL), F32),
        compiler_params=_cparams("parallel"),
        name="wo",
    )(o, w_o, x)


Q_PAD = 8
NEW_PAD = 16


def _sample_tables(rel_bias, dec_seq, buf_rows):
    tq = np.arange(Q_PAD)
    tables = []
    for g, (_, dil) in enumerate(ATTN_GROUPS):
        wb = buf_rows[g]
        for kpos, real in ((np.arange(wb), np.ones(wb, bool)),
                           (wb + np.arange(NEW_PAD), np.arange(NEW_PAD) < dec_seq)):
            delta = (wb + tq)[:, None] - kpos[None, :]
            ok = (delta >= 0) & (delta % dil == 0) & (delta // dil <= BAND)
            ok &= (tq < dec_seq)[:, None] & real[None, :]
            bias = rel_bias[:, g].astype(F32)[_t5_bucket(np.maximum(delta, 0))]
            tables.append(jnp.transpose(bias, (2, 0, 1)))
            tables.append(jnp.asarray(ok.astype(np.float32)))
    return tables


_NT = (((1,), (1,)), ((), ()))


def _attn_sample_kernel(q_ref, kvn_ref, c0_ref, c1_ref, c2_ref, *rest):
    tables, o_ref = rest[:-1], rest[-1]
    for h in range(HEADS):
        parts = []
        for g, c_ref in enumerate((c0_ref, c1_ref, c2_ref)):
            bias_ref, valid_ref, nbias_ref, nvalid_ref = tables[4 * g:4 * g + 4]
            q = q_ref[0, g, h]
            s = jnp.dot(q, c_ref[0, h].astype(BF16), preferred_element_type=F32)
            parts.append((jnp.where(valid_ref[...] > 0.0, s + bias_ref[h], NEG_INF), c_ref, None))
            s = lax.dot_general(q, kvn_ref[0, g, 0, h], _NT, preferred_element_type=F32)
            parts.append((jnp.where(nvalid_ref[...] > 0.0, s + nbias_ref[h], NEG_INF), None, g))
        mx = None
        for s, _, _ in parts:
            pm = jnp.max(s, axis=-1, keepdims=True)
            mx = pm if mx is None else jnp.maximum(mx, pm)
        den = jnp.zeros((Q_PAD, 1), F32)
        acc = jnp.zeros((Q_PAD, HEAD_DIM), F32)
        for s, c_ref, g in parts:
            e = jnp.exp(s - mx)
            den = den + jnp.sum(e, axis=-1, keepdims=True)
            e = e.astype(BF16)
            if c_ref is not None:
                acc = acc + lax.dot_general(e, c_ref[1, h].astype(BF16), _NT, preferred_element_type=F32)
            else:
                acc = acc + jnp.dot(e, kvn_ref[0, g, 1, h], preferred_element_type=F32)
        o_ref[0, h] = acc / den


def _attn_sample(q, kvb, caches, li, tables, n, dec_seq):
    q5 = q.reshape(n, dec_seq, N_GROUPS, HEADS, HEAD_DIM).transpose(0, 2, 3, 1, 4)
    q5 = jnp.pad(q5, ((0, 0), (0, 0), (0, 0), (0, Q_PAD - dec_seq), (0, 0)))
    kvn = kvb.reshape(n, dec_seq, N_GROUPS, 2, HEADS, HEAD_DIM).transpose(0, 2, 3, 4, 1, 5)
    kvn = jnp.pad(kvn, ((0, 0), (0, 0), (0, 0), (0, 0), (0, NEW_PAD - dec_seq), (0, 0)))
    cache_spec = lambda c: pl.BlockSpec((None, None) + c.shape[2:], lambda i: (li, i, 0, 0, 0, 0))
    o = pl.pallas_call(
        _attn_sample_kernel,
        grid=(n,),
        in_specs=[
            pl.BlockSpec((1, N_GROUPS, HEADS, Q_PAD, HEAD_DIM), lambda i: (i, 0, 0, 0, 0)),
            pl.BlockSpec((1, N_GROUPS, 2, HEADS, NEW_PAD, HEAD_DIM), lambda i: (i, 0, 0, 0, 0, 0)),
        ] + [cache_spec(c) for c in caches] + [_const_spec(t.shape) for t in tables],
        out_specs=pl.BlockSpec((1, HEADS, Q_PAD, HEAD_DIM), lambda i: (i, 0, 0, 0)),
        out_shape=jax.ShapeDtypeStruct((n, HEADS, Q_PAD, HEAD_DIM), F32),
        compiler_params=_cparams("parallel"),
        name="attn_sample",
    )(q5, kvn, *caches, *tables)
    return o[:, :, :dec_seq].transpose(0, 2, 1, 3).reshape(n * dec_seq, ATTN_OUT)


HALO = 16


def _pool_prompt_kernel(x_ref, g_ref, w_ref, sc_ref, o_ref, st_ref, hp_ref, *, tt):
    i = pl.program_id(1)

    @pl.when(i == 0)
    def _():
        hp_ref[0:HALO, :] = jnp.zeros((HALO, D_MODEL), F32)

    x = x_ref[0]
    h = _rms(x, g_ref[...])
    hp_ref[HALO:HALO + tt, :] = h
    pos1 = (i * tt + 1 + lax.broadcasted_iota(jnp.int32, (tt, 1), 0)).astype(F32)
    ys = []
    for g, w in enumerate(POOL_WINDOWS):
        cols = slice(g * POOL_CH, (g + 1) * POOL_CH)
        hg = h[:, cols]
        win = hg
        for j in range(1, w):
            win = win + hp_ref[HALO - j:HALO - j + tt, cols]
        p = win / jnp.minimum(float(w), pos1) - hg
        ys.append(jnp.dot(p.astype(BF16), w_ref[g], preferred_element_type=F32))
    o_ref[0] = x + jnp.concatenate(ys, axis=1) * sc_ref[...]
    tail = hp_ref[tt:tt + HALO, :]
    hp_ref[0:HALO, :] = tail

    @pl.when(i == pl.num_programs(1) - 1)
    def _():
        st_ref[0] = tail


def _pool_prompt(x, g, w_pool, scale, batch, seq):
    tt = min(ROW_TILE, seq)
    out, st = pl.pallas_call(
        functools.partial(_pool_prompt_kernel, tt=tt),
        grid=(batch, seq // tt),
        in_specs=[
            pl.BlockSpec((1, tt, D_MODEL), lambda b, i: (b, i, 0)),
            _const_spec((1, D_MODEL)),
            _const_spec(w_pool.shape),
            _const_spec((1, D_MODEL)),
        ],
        out_specs=[
            pl.BlockSpec((1, tt, D_MODEL), lambda b, i: (b, i, 0)),
            pl.BlockSpec((1, HALO, D_MODEL), lambda b, i: (b, 0, 0)),
        ],
        out_shape=[
            jax.ShapeDtypeStruct((batch, seq, D_MODEL), F32),
            jax.ShapeDtypeStruct((batch, HALO, D_MODEL), F32),
        ],
        scratch_shapes=[pltpu.VMEM((HALO + tt, D_MODEL), F32)],
        compiler_params=_cparams("parallel", "arbitrary"),
        name="pool_prompt",
    )(x.reshape(batch, seq, D_MODEL), g.reshape(1, D_MODEL), w_pool, scale.reshape(1, D_MODEL))
    return out.reshape(batch * seq, D_MODEL), st[:, HALO - POOL_STATE:, :]


def _pool_sample_kernel(x_ref, st_ref, g_ref, w_ref, sc_ref, o_ref, nst_ref, *, dec_seq, past_len):
    chunk = lambda ref, k: ref[:, k * D_MODEL:(k + 1) * D_MODEL]
    xs = [chunk(x_ref, t) for t in range(dec_seq)]
    hs = [_rms(x, g_ref[...]) for x in xs]
    rows = [st_ref[k] for k in range(POOL_STATE)] + hs
    for t in range(dec_seq):
        ys = []
        for g, w in enumerate(POOL_WINDOWS):
            cols = slice(g * POOL_CH, (g + 1) * POOL_CH)
            win = rows[POOL_STATE + t][:, cols]
            for j in range(1, w):
                win = win + rows[POOL_STATE + t - j][:, cols]
            p = win / float(min(w, past_len + t + 1)) - hs[t][:, cols]
            ys.append(jnp.dot(p.astype(BF16), w_ref[g], preferred_element_type=F32))
        o_ref[:, t * D_MODEL:(t + 1) * D_MODEL] = xs[t] + jnp.concatenate(ys, axis=1) * sc_ref[...]
    new_rows = rows[-POOL_STATE:]
    for k in range(POOL_STATE):
        nst_ref[k] = new_rows[k]


def _pool_sample(x, state_t, li, g, w_pool, scale, n, dec_seq, past_len):
    bn = min(32, n)
    out, nst = pl.pallas_call(
        functools.partial(_pool_sample_kernel, dec_seq=dec_seq, past_len=past_len),
        grid=(n // bn,),
        in_specs=[
            pl.BlockSpec((bn, dec_seq * D_MODEL), lambda i: (i, 0)),
            pl.BlockSpec((None, POOL_STATE, bn, D_MODEL), lambda i: (li, 0, i, 0)),
            _const_spec((1, D_MODEL)),
            _const_spec(w_pool.shape),
            _const_spec((1, D_MODEL)),
        ],
        out_specs=[
            pl.BlockSpec((bn, dec_seq * D_MODEL), lambda i: (i, 0)),
            pl.BlockSpec((POOL_STATE, bn, D_MODEL), lambda i: (0, i, 0)),
        ],
        out_shape=[
            jax.ShapeDtypeStruct((n, dec_seq * D_MODEL), F32),
            jax.ShapeDtypeStruct((POOL_STATE, n, D_MODEL), F32),
        ],
        compiler_params=_cparams("parallel"),
        name="pool_sample",
    )(x.reshape(n, dec_seq * D_MODEL), state_t, g.reshape(1, D_MODEL), w_pool, scale.reshape(1, D_MODEL))
    return out.reshape(n * dec_seq, D_MODEL), nst


def _permute_qkv_weight(w):
    w = w.reshape(D_MODEL, 3, N_GROUPS, ATTN_OUT)
    q = w[:, 0].reshape(D_MODEL, Q_COLS)
    kv = jnp.stack([w[:, 1], w[:, 2]], axis=2).reshape(D_MODEL, KV_COLS)
    return jnp.concatenate([q, kv], axis=1)


def kernel(x_prompt, x_sample, state_pool, cache_kv_w128, cache_kv_w512, cache_kv_w2048, rel_bias, norm_mix,
           norm_ffn, norm_final, w_pool, pool_scale, w_qkv, w_o, w_up, w_down):
    batch, seq, _ = x_prompt.shape
    n, dec_seq, _ = x_sample.shape
    depth = norm_mix.shape[0]
    caches = tuple(jnp.transpose(c, (0, 1, 3, 4, 5, 2)) for c in (cache_kv_w128, cache_kv_w512, cache_kv_w2048))
    state_t = jnp.transpose(state_pool, (0, 2, 1, 3))
    past_len = PAST_LEN

    xp = x_prompt.reshape(batch * seq, D_MODEL)
    xs = x_sample.reshape(n * dec_seq, D_MODEL)
    w_up_b = w_up.astype(BF16)
    w_down_b = w_down.astype(BF16)
    w_pool_b = w_pool.astype(BF16)
    w_o_b = w_o.astype(BF16)
    w_qkv_b = jax.vmap(_permute_qkv_weight)(w_qkv).astype(BF16)
    prompt_bias = [_prompt_bias(rel_bias[:, g], dil) for g, (_, dil) in enumerate(ATTN_GROUPS)]
    sample_tables = _sample_tables(rel_bias, dec_seq, tuple(c.shape[-1] for c in caches))

    pool_p, pool_s = [], []
    kv_p = [[] for _ in ATTN_GROUPS]
    kv_s = [[] for _ in ATTN_GROUPS]
    for i in range(depth):
        li = i // 2
        if i % 2 == 0:
            xp, sp = _pool_prompt(xp, norm_mix[i], w_pool_b[li], pool_scale[li], batch, seq)
            xs, ss = _pool_sample(xs, state_t, li, norm_mix[i], w_pool_b[li], pool_scale[li], n, dec_seq,
                                  past_len)
            pool_p.append(sp)
            pool_s.append(ss)
        else:
            qp, kvbp, kvfp = _qkv(xp, norm_mix[i], w_qkv_b[li])
            qs, kvbs, kvfs = _qkv(xs, norm_mix[i], w_qkv_b[li])
            os, lses = [], []
            for g, (_, dil) in enumerate(ATTN_GROUPS):
                o, lse = _attn_prompt(qp, kvbp, prompt_bias[g], g, dil, batch, seq)
                os.append(o)
                lses.append(lse)
            xp = _wo_merge(os, lses, w_o_b[li], xp)
            o_s = _attn_sample(qs, kvbs, caches, li, sample_tables, n, dec_seq)
            xs = _wo(o_s, w_o_b[li], xs)
            kvfp = kvfp.reshape(batch, seq, N_GROUPS, 2, HEADS, HEAD_DIM)
            kvfs = kvfs.reshape(n, dec_seq, N_GROUPS, 2, HEADS, HEAD_DIM)
            for g, (win, _) in enumerate(ATTN_GROUPS):
                kv_p[g].append(kvfp[:, seq - min(win, seq):, g])
                kv_s[g].append(kvfs[:, :, g])
        last = i == depth - 1
        xp = _mlp(xp, norm_ffn[i], w_up_b[i], w_down_b[i], norm_final, last)
        xs = _mlp(xs, norm_ffn[i], w_up_b[i], w_down_b[i], norm_final, last)
    return (xp.reshape(batch, seq, D_MODEL), xs.reshape(n, dec_seq, D_MODEL),
            jnp.stack(pool_p), jnp.transpose(jnp.stack(pool_s), (0, 2, 1, 3)),
            jnp.stack(kv_p[0]), jnp.stack(kv_s[0]),
            jnp.stack(kv_p[1]), jnp.stack(kv_s[1]),
            jnp.stack(kv_p[2]), jnp.stack(kv_s[2]))
```

```python
import functools

import numpy as np
import jax
import jax.numpy as jnp
from jax import lax
from jax.experimental import pallas as pl
from jax.experimental.pallas import tpu as pltpu

F32 = jnp.float32
BF16 = jnp.bfloat16

D_MODEL = 1024
D_FF = 4 * D_MODEL
POOL_WINDOWS = (2, 4, 8, 16)
POOL_CH = D_MODEL // len(POOL_WINDOWS)
POOL_STATE = max(POOL_WINDOWS) - 1
ATTN_GROUPS = ((128, 1), (512, 4), (2048, 16))
N_GROUPS = len(ATTN_GROUPS)
HEADS = 8
HEAD_DIM = 64
ATTN_OUT = HEADS * HEAD_DIM
BAND = 128
N_BUCKETS = 32
MAX_EXACT = N_BUCKETS // 2
REL_MAX_DIST = 2048
PAST_LEN = 2048
RMS_EPS = 1e-6
NEG_INF = -1e30

VMEM_LIMIT_BYTES = 56 * 1024 * 1024
ROW_TILE = 512
FF_CHUNK = 1024
_NT = (((1,), (1,)), ((), ()))
LANES = 128


def _cparams(*sem):
    return pltpu.CompilerParams(dimension_semantics=sem, vmem_limit_bytes=VMEM_LIMIT_BYTES)


def _rms(x, g):
    ms = jnp.mean(x * x, axis=-1, keepdims=True)
    return x * lax.rsqrt(ms + RMS_EPS) * g


def _t5_bucket(dist):
    n = np.maximum(np.asarray(dist), 0)
    large = MAX_EXACT + (np.log(np.maximum(n, 1) / MAX_EXACT) / np.log(REL_MAX_DIST / MAX_EXACT)
                         * (N_BUCKETS - MAX_EXACT)).astype(np.int64)
    large = np.minimum(large, N_BUCKETS - 1)
    return np.where(n < MAX_EXACT, n, large).astype(np.int32)


def _const_spec(shape):
    nd = len(shape)
    return pl.BlockSpec(shape, lambda *_: (0,) * nd, pipeline_mode=pl.Buffered(1))


def _mlp_kernel(x_ref, g_ref, wu_ref, wd_ref, gf_ref, o_ref, *, final_norm):
    x = x_ref[...]
    h = _rms(x, g_ref[...]).astype(BF16)
    acc = x
    for c in range(D_FF // FF_CHUNK):
        cols = slice(c * FF_CHUNK, (c + 1) * FF_CHUNK)
        u = jnp.dot(h, wu_ref[:, cols], preferred_element_type=F32)
        a = jnp.square(jnp.maximum(u, 0.0)).astype(BF16)
        acc = acc + jnp.dot(a, wd_ref[cols, :], preferred_element_type=F32)
    if final_norm:
        acc = _rms(acc, gf_ref[...])
    o_ref[...] = acc


def _mlp(x, g, w_up, w_down, g_final, final_norm):
    m = x.shape[0]
    tm = min(ROW_TILE, m)
    return pl.pallas_call(
        functools.partial(_mlp_kernel, final_norm=final_norm),
        grid=(m // tm,),
        in_specs=[
            pl.BlockSpec((tm, D_MODEL), lambda i: (i, 0)),
            _const_spec((1, D_MODEL)),
            _const_spec((D_MODEL, D_FF)),
            _const_spec((D_FF, D_MODEL)),
            _const_spec((1, D_MODEL)),
        ],
        out_specs=pl.BlockSpec((tm, D_MODEL), lambda i: (i, 0)),
        out_shape=jax.ShapeDtypeStruct((m, D_MODEL), F32),
        compiler_params=_cparams("parallel"),
        name="mlp",
    )(x, g.reshape(1, D_MODEL), w_up, w_down, g_final.reshape(1, D_MODEL))


Q_COLS = N_GROUPS * ATTN_OUT
KV_COLS = 2 * N_GROUPS * ATTN_OUT


def _qkv_kernel(x_ref, g_ref, w_ref, q_ref, kvb_ref, kvf_ref):
    h = _rms(x_ref[...], g_ref[...]).astype(BF16)
    q = jnp.dot(h, w_ref[:, :Q_COLS], preferred_element_type=F32)
    q_ref[...] = (q * (HEAD_DIM ** -0.5)).astype(BF16)
    for g in range(N_GROUPS):
        cols = slice(g * 2 * ATTN_OUT, (g + 1) * 2 * ATTN_OUT)
        kv = jnp.dot(h, w_ref[:, Q_COLS + cols.start:Q_COLS + cols.stop], preferred_element_type=F32)
        kvf_ref[:, cols] = kv
        kvb_ref[:, cols] = kv.astype(BF16)


def _qkv(x, g, w):
    m = x.shape[0]
    tm = min(ROW_TILE, m)
    row = lambda i: (i, 0)
    return pl.pallas_call(
        _qkv_kernel,
        grid=(m // tm,),
        in_specs=[
            pl.BlockSpec((tm, D_MODEL), row),
            _const_spec((1, D_MODEL)),
            _const_spec((D_MODEL, Q_COLS + KV_COLS)),
        ],
        out_specs=[
            pl.BlockSpec((tm, Q_COLS), row),
            pl.BlockSpec((tm, KV_COLS), row),
            pl.BlockSpec((tm, KV_COLS), row),
        ],
        out_shape=[
            jax.ShapeDtypeStruct((m, Q_COLS), BF16),
            jax.ShapeDtypeStruct((m, KV_COLS), BF16),
            jax.ShapeDtypeStruct((m, KV_COLS), F32),
        ],
        compiler_params=_cparams("parallel"),
        name="qkv",
    )(x, g.reshape(1, D_MODEL), w)


def _qkv_prompt_kernel(x_ref, g_ref, w_ref, wt_ref, *rest, tm, n_tiles, wins):
    q_refs, kv_refs, kvt_refs, scr_ref = rest[0:3], rest[3:6], rest[6:9], rest[9]
    i = pl.program_id(1)
    h = _rms(x_ref[0], g_ref[...]).astype(BF16)

    def write_split(ref, val, dil):
        if dil == 1:
            ref[0, 0] = val.astype(BF16)
            return
        chunks = val.shape[1] // LANES
        for c in range(chunks):
            scr_ref[c] = val[:, c * LANES:(c + 1) * LANES]
        for r in range(dil):
            rows = [scr_ref[c, pl.ds(r, tm // dil, stride=dil), :] for c in range(chunks)]
            ref[0, r] = jnp.concatenate(rows, axis=1).astype(BF16)

    for g, (_, dil) in enumerate(ATTN_GROUPS):
        q = jnp.dot(h, w_ref[:, g * ATTN_OUT:(g + 1) * ATTN_OUT], preferred_element_type=F32)
        write_split(q_refs[g], q * (HEAD_DIM ** -0.5), dil)
        c0 = Q_COLS + g * 2 * ATTN_OUT
        kv = jnp.dot(h, w_ref[:, c0:c0 + 2 * ATTN_OUT], preferred_element_type=F32)
        write_split(kv_refs[g], kv, dil)

    for g, win in enumerate(wins):
        rows = min(win, tm)
        first_tile = n_tiles - max(win // tm, 1)

        @pl.when(i >= first_tile)
        def _(g=g, rows=rows):
            wt = wt_ref[g * 2 * ATTN_OUT:(g + 1) * 2 * ATTN_OUT, :]
            kvt = lax.dot_general(wt, h[tm - rows:, :], _NT, preferred_element_type=F32)
            kvt_refs[g][0] = kvt.reshape(2, HEADS, HEAD_DIM, rows)


def _qkv_prompt(x, g, w, w_t, batch, seq):
    tm = min(ROW_TILE, seq)
    n_tiles = seq // tm
    wins = tuple(min(win, seq) for win, _ in ATTN_GROUPS)
    assert all(w_ % tm == 0 or tm % w_ == 0 for w_ in wins)
    split = lambda width: [
        (pl.BlockSpec((1, dil, tm // dil, width), lambda b, i: (b, 0, i, 0)),
         jax.ShapeDtypeStruct((batch, dil, seq // dil, width), BF16)) for _, dil in ATTN_GROUPS]
    kvt = [(pl.BlockSpec((1, 2, HEADS, HEAD_DIM, min(win, tm)),
                         lambda b, i, first=n_tiles - max(win // tm, 1): (b, 0, 0, 0, jnp.maximum(i - first, 0))),
            jax.ShapeDtypeStruct((batch, 2, HEADS, HEAD_DIM, win), F32)) for win in wins]
    outs = split(ATTN_OUT) + split(2 * ATTN_OUT) + kvt
    res = pl.pallas_call(
        functools.partial(_qkv_prompt_kernel, tm=tm, n_tiles=n_tiles, wins=wins),
        grid=(batch, n_tiles),
        in_specs=[
            pl.BlockSpec((1, tm, D_MODEL), lambda b, i: (b, i, 0)),
            _const_spec((1, D_MODEL)),
            _const_spec((D_MODEL, Q_COLS + KV_COLS)),
            _const_spec((KV_COLS, D_MODEL)),
        ],
        out_specs=[o[0] for o in outs],
        out_shape=[o[1] for o in outs],
        scratch_shapes=[pltpu.VMEM((2 * ATTN_OUT // LANES, tm, LANES), F32)],
        compiler_params=_cparams("parallel", "arbitrary"),
        name="qkv_prompt",
    )(x.reshape(batch, seq, D_MODEL), g.reshape(1, D_MODEL), w, w_t)
    return res[0:3], res[3:6], res[6:9]


def _attn_prompt_kernel(q_ref, kvp_ref, kvc_ref, bias_ref, o_ref, lse_ref):
    n = pl.program_id(2)
    q = q_ref[0, 0]
    kv = jnp.concatenate([kvp_ref[0, 0], kvc_ref[0, 0]], axis=0)
    qi = lax.broadcasted_iota(jnp.int32, (BAND, 2 * BAND), 0)
    kj = lax.broadcasted_iota(jnp.int32, (BAND, 2 * BAND), 1)
    m = BAND + qi - kj
    first_key = jnp.where(n > 0, 0, BAND)
    valid = (m >= 0) & (m <= BAND) & (kj >= first_key)
    lse_ref[0, 0] = jnp.zeros((BAND, LANES), F32)
    for h in range(HEADS):
        lo, hi = h * HEAD_DIM, (h + 1) * HEAD_DIM
        s = lax.dot_general(q[:, lo:hi], kv[:, lo:hi], (((1,), (1,)), ((), ())),
                            preferred_element_type=F32)
        s = jnp.where(valid, s + bias_ref[h], NEG_INF)
        mx = jnp.max(s, axis=-1, keepdims=True)
        e = jnp.exp(s - mx)
        den = jnp.sum(e, axis=-1, keepdims=True)
        o = jnp.dot(e.astype(BF16), kv[:, ATTN_OUT + lo:ATTN_OUT + hi], preferred_element_type=F32)
        o_ref[0, 0, :, lo:hi] = o / den
        lse_ref[0, 0, :, h:h + 1] = mx + jnp.log(den)


def _attn_prompt(q, kv, bias):
    batch, dil, sub, _ = q.shape
    cur = lambda b, r, n: (b, r, n, 0)
    return pl.pallas_call(
        _attn_prompt_kernel,
        grid=(batch, dil, sub // BAND),
        in_specs=[
            pl.BlockSpec((1, 1, BAND, ATTN_OUT), cur),
            pl.BlockSpec((1, 1, BAND, 2 * ATTN_OUT), lambda b, r, n: (b, r, jnp.maximum(n - 1, 0), 0)),
            pl.BlockSpec((1, 1, BAND, 2 * ATTN_OUT), cur),
            _const_spec((HEADS, BAND, 2 * BAND)),
        ],
        out_specs=[
            pl.BlockSpec((1, 1, BAND, ATTN_OUT), cur),
            pl.BlockSpec((1, 1, BAND, LANES), cur),
        ],
        out_shape=[
            jax.ShapeDtypeStruct((batch, dil, sub, ATTN_OUT), F32),
            jax.ShapeDtypeStruct((batch, dil, sub, LANES), F32),
        ],
        compiler_params=_cparams("parallel", "parallel", "arbitrary"),
        name="attn_prompt_d%d" % dil,
    )(q, kv, kv, bias)


def _prompt_bias(rel_bias_g, dil):
    m = BAND + np.arange(BAND)[:, None] - np.arange(2 * BAND)[None, :]
    bias = rel_bias_g.astype(F32)[_t5_bucket(np.clip(m, 0, BAND) * dil)]
    return jnp.transpose(bias, (2, 0, 1))


def _wo_merge_kernel(o0_ref, o1_ref, o2_ref, l0_ref, l1_ref, l2_ref, e_ref, w_ref, x_ref, out_ref,
                     oscr_ref, lscr_ref, *, tm):
    def token_order(ref, scr_ref, dil):
        if dil == 1:
            return ref[0, 0]
        chunks = ref.shape[-1] // LANES
        for r in range(dil):
            for c in range(chunks):
                scr_ref[c, pl.ds(r, tm // dil, stride=dil), :] = ref[0, r, :, c * LANES:(c + 1) * LANES]
        return jnp.concatenate([scr_ref[c] for c in range(chunks)], axis=1)

    dils = [dil for _, dil in ATTN_GROUPS]
    ls = [token_order(l_ref, lscr_ref.at[g:g + 1], dils[g])[:, :HEADS]
          for g, l_ref in enumerate((l0_ref, l1_ref, l2_ref))]
    mx = jnp.maximum(jnp.maximum(ls[0], ls[1]), ls[2])
    es = [jnp.exp(l - mx) for l in ls]
    inv = 1.0 / (es[0] + es[1] + es[2])
    acc = None
    for g, o_ref in enumerate((o0_ref, o1_ref, o2_ref)):
        wt = es[g] * inv
        hi = wt.astype(BF16)
        lo = (wt - hi.astype(F32)).astype(BF16)
        wexp = (jnp.dot(hi, e_ref[...], preferred_element_type=F32)
                + jnp.dot(lo, e_ref[...], preferred_element_type=F32))
        term = wexp * token_order(o_ref, oscr_ref, dils[g])
        acc = term if acc is None else acc + term
    out_ref[0] = x_ref[0] + jnp.dot(acc.astype(BF16), w_ref[...], preferred_element_type=F32)


def _wo_merge(os, lses, w_o, x, batch, seq):
    tm = min(ROW_TILE, seq)
    expand = jnp.asarray(np.repeat(np.eye(HEADS, dtype=np.float32), HEAD_DIM, axis=1), BF16)
    split = lambda width: [pl.BlockSpec((1, dil, tm // dil, width), lambda b, i: (b, 0, i, 0))
                           for _, dil in ATTN_GROUPS]
    row = pl.BlockSpec((1, tm, D_MODEL), lambda b, i: (b, i, 0))
    out = pl.pallas_call(
        functools.partial(_wo_merge_kernel, tm=tm),
        grid=(batch, seq // tm),
        in_specs=split(ATTN_OUT) + split(LANES) + [
            _const_spec((HEADS, ATTN_OUT)),
            _const_spec((ATTN_OUT, D_MODEL)),
            row,
        ],
        out_specs=row,
        out_shape=jax.ShapeDtypeStruct((batch, seq, D_MODEL), F32),
        scratch_shapes=[pltpu.VMEM((ATTN_OUT // LANES, tm, LANES), F32), pltpu.VMEM((N_GROUPS, tm, LANES), F32)],
        compiler_params=_cparams("parallel", "parallel"),
        name="wo_merge",
    )(*os, *lses, expand, w_o, x.reshape(batch, seq, D_MODEL))
    return out.reshape(batch * seq, D_MODEL)


def _wo_kernel(o_ref, w_ref, x_ref, out_ref):
    out_ref[...] = x_ref[...] + jnp.dot(o_ref[...].astype(BF16), w_ref[...], preferred_element_type=F32)


def _wo(o, w_o, x):
    m = x.shape[0]
    tm = min(ROW_TILE, m)
    row = lambda i: (i, 0)
    return pl.pallas_call(
        _wo_kernel,
        grid=(m // tm,),
        in_specs=[pl.BlockSpec((tm, ATTN_OUT), row), _const_spec((ATTN_OUT, D_MODEL)),
                  pl.BlockSpec((tm, D_MODEL), row)],
        out_specs=pl.BlockSpec((tm, D_MODEL), row),
        out_shape=jax.ShapeDtypeStruct((m, D_MODEL), F32),
        compiler_params=_cparams("parallel"),
        name="wo",
    )(o, w_o, x)


Q_PAD = 8
NEW_PAD = 16


def _sample_tables(rel_bias, dec_seq, buf_rows):
    tq = np.arange(Q_PAD)
    tables = []
    for g, (_, dil) in enumerate(ATTN_GROUPS):
        wb = buf_rows[g]
        for kpos, real in ((np.arange(wb), np.ones(wb, bool)),
                           (wb + np.arange(NEW_PAD), np.arange(NEW_PAD) < dec_seq)):
            delta = (wb + tq)[:, None] - kpos[None, :]
            ok = (delta >= 0) & (delta % dil == 0) & (delta // dil <= BAND)
            ok &= (tq < dec_seq)[:, None] & real[None, :]
            bias = rel_bias[:, g].astype(F32)[_t5_bucket(np.maximum(delta, 0))]
            tables.append(jnp.transpose(bias, (2, 0, 1)))
            tables.append(jnp.asarray(ok.astype(np.float32)))
    return tables


def _attn_sample_kernel(q_ref, kvn_ref, c0_ref, c1_ref, c2_ref, *rest):
    tables, o_ref = rest[:-1], rest[-1]
    for h in range(HEADS):
        parts = []
        for g, c_ref in enumerate((c0_ref, c1_ref, c2_ref)):
            bias_ref, valid_ref, nbias_ref, nvalid_ref = tables[4 * g:4 * g + 4]
            q = q_ref[0, g, h]
            s = jnp.dot(q, c_ref[0, h].astype(BF16), preferred_element_type=F32)
            parts.append((jnp.where(valid_ref[...] > 0.0, s + bias_ref[h], NEG_INF), c_ref, None))
            s = lax.dot_general(q, kvn_ref[0, g, 0, h], _NT, preferred_element_type=F32)
            parts.append((jnp.where(nvalid_ref[...] > 0.0, s + nbias_ref[h], NEG_INF), None, g))
        mx = None
        for s, _, _ in parts:
            pm = jnp.max(s, axis=-1, keepdims=True)
            mx = pm if mx is None else jnp.maximum(mx, pm)
        den = jnp.zeros((Q_PAD, 1), F32)
        acc = jnp.zeros((Q_PAD, HEAD_DIM), F32)
        for s, c_ref, g in parts:
            e = jnp.exp(s - mx)
            den = den + jnp.sum(e, axis=-1, keepdims=True)
            e = e.astype(BF16)
            if c_ref is not None:
                acc = acc + lax.dot_general(e, c_ref[1, h].astype(BF16), _NT, preferred_element_type=F32)
            else:
                acc = acc + jnp.dot(e, kvn_ref[0, g, 1, h], preferred_element_type=F32)
        o_ref[0, h] = acc / den


def _attn_sample(q, kvb, caches, li, tables, n, dec_seq):
    q5 = q.reshape(n, dec_seq, N_GROUPS, HEADS, HEAD_DIM).transpose(0, 2, 3, 1, 4)
    q5 = jnp.pad(q5, ((0, 0), (0, 0), (0, 0), (0, Q_PAD - dec_seq), (0, 0)))
    kvn = kvb.reshape(n, dec_seq, N_GROUPS, 2, HEADS, HEAD_DIM).transpose(0, 2, 3, 4, 1, 5)
    kvn = jnp.pad(kvn, ((0, 0), (0, 0), (0, 0), (0, 0), (0, NEW_PAD - dec_seq), (0, 0)))
    cache_spec = lambda c: pl.BlockSpec((None, None) + c.shape[2:], lambda i: (li, i, 0, 0, 0, 0))
    o = pl.pallas_call(
        _attn_sample_kernel,
        grid=(n,),
        in_specs=[
            pl.BlockSpec((1, N_GROUPS, HEADS, Q_PAD, HEAD_DIM), lambda i: (i, 0, 0, 0, 0)),
            pl.BlockSpec((1, N_GROUPS, 2, HEADS, NEW_PAD, HEAD_DIM), lambda i: (i, 0, 0, 0, 0, 0)),
        ] + [cache_spec(c) for c in caches] + [_const_spec(t.shape) for t in tables],
        out_specs=pl.BlockSpec((1, HEADS, Q_PAD, HEAD_DIM), lambda i: (i, 0, 0, 0)),
        out_shape=jax.ShapeDtypeStruct((n, HEADS, Q_PAD, HEAD_DIM), F32),
        compiler_params=_cparams("parallel"),
        name="attn_sample",
    )(q5, kvn, *caches, *tables)
    return o[:, :, :dec_seq].transpose(0, 2, 1, 3).reshape(n * dec_seq, ATTN_OUT)


HALO = 16


def _pool_prompt_kernel(x_ref, g_ref, w_ref, sc_ref, o_ref, st_ref, hp_ref, *, tt):
    i = pl.program_id(1)

    @pl.when(i == 0)
    def _():
        hp_ref[0:HALO, :] = jnp.zeros((HALO, D_MODEL), F32)

    x = x_ref[0]
    h = _rms(x, g_ref[...])
    hp_ref[HALO:HALO + tt, :] = h
    pos1 = (i * tt + 1 + lax.broadcasted_iota(jnp.int32, (tt, 1), 0)).astype(F32)
    ys = []
    for g, w in enumerate(POOL_WINDOWS):
        cols = slice(g * POOL_CH, (g + 1) * POOL_CH)
        hg = h[:, cols]
        win = hg
        for j in range(1, w):
            win = win + hp_ref[HALO - j:HALO - j + tt, cols]
        p = win / jnp.minimum(float(w), pos1) - hg
        ys.append(jnp.dot(p.astype(BF16), w_ref[g], preferred_element_type=F32))
    o_ref[0] = x + jnp.concatenate(ys, axis=1) * sc_ref[...]
    tail = hp_ref[tt:tt + HALO, :]
    hp_ref[0:HALO, :] = tail

    @pl.when(i == pl.num_programs(1) - 1)
    def _():
        st_ref[0] = tail


def _pool_prompt(x, g, w_pool, scale, batch, seq):
    tt = min(ROW_TILE, seq)
    out, st = pl.pallas_call(
        functools.partial(_pool_prompt_kernel, tt=tt),
        grid=(batch, seq // tt),
        in_specs=[
            pl.BlockSpec((1, tt, D_MODEL), lambda b, i: (b, i, 0)),
            _const_spec((1, D_MODEL)),
            _const_spec(w_pool.shape),
            _const_spec((1, D_MODEL)),
        ],
        out_specs=[
            pl.BlockSpec((1, tt, D_MODEL), lambda b, i: (b, i, 0)),
            pl.BlockSpec((1, HALO, D_MODEL), lambda b, i: (b, 0, 0)),
        ],
        out_shape=[
            jax.ShapeDtypeStruct((batch, seq, D_MODEL), F32),
            jax.ShapeDtypeStruct((batch, HALO, D_MODEL), F32),
        ],
        scratch_shapes=[pltpu.VMEM((HALO + tt, D_MODEL), F32)],
        compiler_params=_cparams("parallel", "arbitrary"),
        name="pool_prompt",
    )(x.reshape(batch, seq, D_MODEL), g.reshape(1, D_MODEL), w_pool, scale.reshape(1, D_MODEL))
    return out.reshape(batch * seq, D_MODEL), st[:, HALO - POOL_STATE:, :]


def _pool_sample_kernel(x_ref, st_ref, g_ref, w_ref, sc_ref, o_ref, nst_ref, *, dec_seq, past_len):
    chunk = lambda ref, k: ref[:, k * D_MODEL:(k + 1) * D_MODEL]
    xs = [chunk(x_ref, t) for t in range(dec_seq)]
    hs = [_rms(x, g_ref[...]) for x in xs]
    rows = [st_ref[k] for k in range(POOL_STATE)] + hs
    for t in range(dec_seq):
        ys = []
        for g, w in enumerate(POOL_WINDOWS):
            cols = slice(g * POOL_CH, (g + 1) * POOL_CH)
            win = rows[POOL_STATE + t][:, cols]
            for j in range(1, w):
                win = win + rows[POOL_STATE + t - j][:, cols]
            p = win / float(min(w, past_len + t + 1)) - hs[t][:, cols]
            ys.append(jnp.dot(p.astype(BF16), w_ref[g], preferred_element_type=F32))
        o_ref[:, t * D_MODEL:(t + 1) * D_MODEL] = xs[t] + jnp.concatenate(ys, axis=1) * sc_ref[...]
    new_rows = rows[-POOL_STATE:]
    for k in range(POOL_STATE):
        nst_ref[k] = new_rows[k]


def _pool_sample(x, state_t, li, g, w_pool, scale, n, dec_seq, past_len):
    bn = min(32, n)
    out, nst = pl.pallas_call(
        functools.partial(_pool_sample_kernel, dec_seq=dec_seq, past_len=past_len),
        grid=(n // bn,),
        in_specs=[
            pl.BlockSpec((bn, dec_seq * D_MODEL), lambda i: (i, 0)),
            pl.BlockSpec((None, POOL_STATE, bn, D_MODEL), lambda i: (li, 0, i, 0)),
            _const_spec((1, D_MODEL)),
            _const_spec(w_pool.shape),
            _const_spec((1, D_MODEL)),
        ],
        out_specs=[
            pl.BlockSpec((bn, dec_seq * D_MODEL), lambda i: (i, 0)),
            pl.BlockSpec((POOL_STATE, bn, D_MODEL), lambda i: (0, i, 0)),
        ],
        out_shape=[
            jax.ShapeDtypeStruct((n, dec_seq * D_MODEL), F32),
            jax.ShapeDtypeStruct((POOL_STATE, n, D_MODEL), F32),
        ],
        compiler_params=_cparams("parallel"),
        name="pool_sample",
    )(x.reshape(n, dec_seq * D_MODEL), state_t, g.reshape(1, D_MODEL), w_pool, scale.reshape(1, D_MODEL))
    return out.reshape(n * dec_seq, D_MODEL), nst


def _permute_qkv_weight(w):
    w = w.reshape(D_MODEL, 3, N_GROUPS, ATTN_OUT)
    q = w[:, 0].reshape(D_MODEL, Q_COLS)
    kv = jnp.stack([w[:, 1], w[:, 2]], axis=2).reshape(D_MODEL, KV_COLS)
    return jnp.concatenate([q, kv], axis=1)


def kernel(x_prompt, x_sample, state_pool, cache_kv_w128, cache_kv_w512, cache_kv_w2048, rel_bias, norm_mix,
           norm_ffn, norm_final, w_pool, pool_scale, w_qkv, w_o, w_up, w_down):
    batch, seq, _ = x_prompt.shape
    n, dec_seq, _ = x_sample.shape
    depth = norm_mix.shape[0]
    caches = tuple(jnp.transpose(c, (0, 1, 3, 4, 5, 2)) for c in (cache_kv_w128, cache_kv_w512, cache_kv_w2048))
    state_t = jnp.transpose(state_pool, (0, 2, 1, 3))
    past_len = PAST_LEN

    xp = x_prompt.reshape(batch * seq, D_MODEL)
    xs = x_sample.reshape(n * dec_seq, D_MODEL)
    w_up_b = w_up.astype(BF16)
    w_down_b = w_down.astype(BF16)
    w_pool_b = w_pool.astype(BF16)
    w_o_b = w_o.astype(BF16)
    w_qkv_b = jax.vmap(_permute_qkv_weight)(w_qkv).astype(BF16)
    w_kv_t = jnp.transpose(w_qkv_b[:, :, Q_COLS:], (0, 2, 1))
    prompt_bias = [_prompt_bias(rel_bias[:, g], dil) for g, (_, dil) in enumerate(ATTN_GROUPS)]
    sample_tables = _sample_tables(rel_bias, dec_seq, tuple(c.shape[-1] for c in caches))

    kv_rows_major = lambda per_layer: jnp.transpose(jnp.stack(per_layer), (0, 1, 5, 2, 3, 4))

    pool_p, pool_s = [], []
    kv_p = [[] for _ in ATTN_GROUPS]
    kv_s = [[] for _ in ATTN_GROUPS]
    for i in range(depth):
        li = i // 2
        if i % 2 == 0:
            xp, sp = _pool_prompt(xp, norm_mix[i], w_pool_b[li], pool_scale[li], batch, seq)
            xs, ss = _pool_sample(xs, state_t, li, norm_mix[i], w_pool_b[li], pool_scale[li], n, dec_seq,
                                  past_len)
            pool_p.append(sp)
            pool_s.append(ss)
        else:
            qps, kvps, kvts = _qkv_prompt(xp, norm_mix[i], w_qkv_b[li], w_kv_t[li], batch, seq)
            qs, kvbs, kvfs = _qkv(xs, norm_mix[i], w_qkv_b[li])
            os, lses = zip(*[_attn_prompt(qps[g], kvps[g], prompt_bias[g]) for g in range(N_GROUPS)])
            xp = _wo_merge(os, lses, w_o_b[li], xp, batch, seq)
            o_s = _attn_sample(qs, kvbs, caches, li, sample_tables, n, dec_seq)
            xs = _wo(o_s, w_o_b[li], xs)
            kvfs = kvfs.reshape(n, dec_seq, N_GROUPS, 2, HEADS, HEAD_DIM)
            for g in range(N_GROUPS):
                kv_p[g].append(kvts[g])
                kv_s[g].append(kvfs[:, :, g])
        last = i == depth - 1
        xp = _mlp(xp, norm_ffn[i], w_up_b[i], w_down_b[i], norm_final, last)
        xs = _mlp(xs, norm_ffn[i], w_up_b[i], w_down_b[i], norm_final, last)
    return (xp.reshape(batch, seq, D_MODEL), xs.reshape(n, dec_seq, D_MODEL),
            jnp.stack(pool_p), jnp.transpose(jnp.stack(pool_s), (0, 2, 1, 3)),
            kv_rows_major(kv_p[0]), jnp.stack(kv_s[0]),
            kv_rows_major(kv_p[1]), jnp.stack(kv_s[1]),
            kv_rows_major(kv_p[2]), jnp.stack(kv_s[2]))
```

```python
import functools

import numpy as np
import jax
import jax.numpy as jnp
from jax import lax
from jax.experimental import pallas as pl
from jax.experimental.pallas import tpu as pltpu

F32 = jnp.float32
BF16 = jnp.bfloat16

D_MODEL = 1024
D_FF = 4 * D_MODEL
POOL_WINDOWS = (2, 4, 8, 16)
POOL_CH = D_MODEL // len(POOL_WINDOWS)
POOL_STATE = max(POOL_WINDOWS) - 1
ATTN_GROUPS = ((128, 1), (512, 4), (2048, 16))
N_GROUPS = len(ATTN_GROUPS)
HEADS = 8
HEAD_DIM = 64
ATTN_OUT = HEADS * HEAD_DIM
BAND = 128
N_BUCKETS = 32
MAX_EXACT = N_BUCKETS // 2
REL_MAX_DIST = 2048
PAST_LEN = 2048
RMS_EPS = 1e-6
NEG_INF = -1e30

VMEM_LIMIT_BYTES = 56 * 1024 * 1024
ROW_TILE = 512
FF_CHUNK = 1024
_NT = (((1,), (1,)), ((), ()))
LANES = 128


def _cparams(*sem):
    return pltpu.CompilerParams(dimension_semantics=sem, vmem_limit_bytes=VMEM_LIMIT_BYTES)


def _rms(x, g):
    ms = jnp.mean(x * x, axis=-1, keepdims=True)
    return x * lax.rsqrt(ms + RMS_EPS) * g


def _t5_bucket(dist):
    n = np.maximum(np.asarray(dist), 0)
    large = MAX_EXACT + (np.log(np.maximum(n, 1) / MAX_EXACT) / np.log(REL_MAX_DIST / MAX_EXACT)
                         * (N_BUCKETS - MAX_EXACT)).astype(np.int64)
    large = np.minimum(large, N_BUCKETS - 1)
    return np.where(n < MAX_EXACT, n, large).astype(np.int32)


def _const_spec(shape):
    nd = len(shape)
    return pl.BlockSpec(shape, lambda *_: (0,) * nd, pipeline_mode=pl.Buffered(1))


def _layer_spec(shape, layer):
    nd = len(shape)
    return pl.BlockSpec((None,) + tuple(shape), lambda *_: (layer,) + (0,) * nd, pipeline_mode=pl.Buffered(1))


def _mlp_kernel(x_ref, g_ref, wu_ref, wd_ref, gf_ref, o_ref, *, final_norm):
    x = x_ref[...]
    h = _rms(x, g_ref[...]).astype(BF16)
    acc = x
    for c in range(D_FF // FF_CHUNK):
        cols = slice(c * FF_CHUNK, (c + 1) * FF_CHUNK)
        u = jnp.dot(h, wu_ref[:, cols], preferred_element_type=F32)
        a = jnp.square(jnp.maximum(u, 0.0)).astype(BF16)
        acc = acc + jnp.dot(a, wd_ref[cols, :], preferred_element_type=F32)
    if final_norm:
        acc = _rms(acc, gf_ref[...])
    o_ref[...] = acc


def _mlp(x, g, w_up, w_down, layer, g_final, final_norm):
    m = x.shape[0]
    tm = min(ROW_TILE, m)
    return pl.pallas_call(
        functools.partial(_mlp_kernel, final_norm=final_norm),
        grid=(m // tm,),
        in_specs=[
            pl.BlockSpec((tm, D_MODEL), lambda i: (i, 0)),
            _const_spec((1, D_MODEL)),
            _layer_spec((D_MODEL, D_FF), layer),
            _layer_spec((D_FF, D_MODEL), layer),
            _const_spec((1, D_MODEL)),
        ],
        out_specs=pl.BlockSpec((tm, D_MODEL), lambda i: (i, 0)),
        out_shape=jax.ShapeDtypeStruct((m, D_MODEL), F32),
        compiler_params=_cparams("parallel"),
        name="mlp",
    )(x, g.reshape(1, D_MODEL), w_up, w_down, g_final.reshape(1, D_MODEL))


Q_COLS = N_GROUPS * ATTN_OUT
KV_COLS = 2 * N_GROUPS * ATTN_OUT


def _qkv_kernel(x_ref, g_ref, w_ref, q_ref, kvb_ref, kvf_ref):
    h = _rms(x_ref[...], g_ref[...]).astype(BF16)
    q = jnp.dot(h, w_ref[:, :Q_COLS], preferred_element_type=F32)
    q_ref[...] = (q * (HEAD_DIM ** -0.5)).astype(BF16)
    for g in range(N_GROUPS):
        cols = slice(g * 2 * ATTN_OUT, (g + 1) * 2 * ATTN_OUT)
        kv = jnp.dot(h, w_ref[:, Q_COLS + cols.start:Q_COLS + cols.stop], preferred_element_type=F32)
        kvf_ref[:, cols] = kv
        kvb_ref[:, cols] = kv.astype(BF16)


def _qkv(x, g, w):
    m = x.shape[0]
    tm = min(ROW_TILE, m)
    row = lambda i: (i, 0)
    return pl.pallas_call(
        _qkv_kernel,
        grid=(m // tm,),
        in_specs=[
            pl.BlockSpec((tm, D_MODEL), row),
            _const_spec((1, D_MODEL)),
            _const_spec((D_MODEL, Q_COLS + KV_COLS)),
        ],
        out_specs=[
            pl.BlockSpec((tm, Q_COLS), row),
            pl.BlockSpec((tm, KV_COLS), row),
            pl.BlockSpec((tm, KV_COLS), row),
        ],
        out_shape=[
            jax.ShapeDtypeStruct((m, Q_COLS), BF16),
            jax.ShapeDtypeStruct((m, KV_COLS), BF16),
            jax.ShapeDtypeStruct((m, KV_COLS), F32),
        ],
        compiler_params=_cparams("parallel"),
        name="qkv",
    )(x, g.reshape(1, D_MODEL), w)


def _qkv_prompt_kernel(x_ref, g_ref, w_ref, wt_ref, *rest, tm, n_tiles, wins):
    q_refs, kv_refs, kvt_refs, scr_ref = rest[0:3], rest[3:6], rest[6:9], rest[9]
    i = pl.program_id(1)
    h = _rms(x_ref[0], g_ref[...]).astype(BF16)

    def write_split(ref, val, dil):
        if dil == 1:
            ref[0, 0] = val.astype(BF16)
            return
        chunks = val.shape[1] // LANES
        for c in range(chunks):
            scr_ref[c] = val[:, c * LANES:(c + 1) * LANES]
        for r in range(dil):
            rows = [scr_ref[c, pl.ds(r, tm // dil, stride=dil), :] for c in range(chunks)]
            ref[0, r] = jnp.concatenate(rows, axis=1).astype(BF16)

    for g, (_, dil) in enumerate(ATTN_GROUPS):
        q = jnp.dot(h, w_ref[:, g * ATTN_OUT:(g + 1) * ATTN_OUT], preferred_element_type=F32)
        write_split(q_refs[g], q * (HEAD_DIM ** -0.5), dil)
        c0 = Q_COLS + g * 2 * ATTN_OUT
        kv = jnp.dot(h, w_ref[:, c0:c0 + 2 * ATTN_OUT], preferred_element_type=F32)
        write_split(kv_refs[g], kv, dil)

    for g, win in enumerate(wins):
        rows = min(win, tm)
        first_tile = n_tiles - max(win // tm, 1)

        @pl.when(i >= first_tile)
        def _(g=g, rows=rows):
            wt = wt_ref[g * 2 * ATTN_OUT:(g + 1) * 2 * ATTN_OUT, :]
            kvt = lax.dot_general(wt, h[tm - rows:, :], _NT, preferred_element_type=F32)
            kvt_refs[g][0] = kvt.reshape(2, HEADS, HEAD_DIM, rows)


def _qkv_prompt(x, g, w, w_t, batch, seq):
    tm = min(ROW_TILE, seq)
    n_tiles = seq // tm
    wins = tuple(min(win, seq) for win, _ in ATTN_GROUPS)
    assert all(w_ % tm == 0 or tm % w_ == 0 for w_ in wins)
    split = lambda width: [
        (pl.BlockSpec((1, dil, tm // dil, width), lambda b, i: (b, 0, i, 0)),
         jax.ShapeDtypeStruct((batch, dil, seq // dil, width), BF16)) for _, dil in ATTN_GROUPS]
    kvt = [(pl.BlockSpec((1, 2, HEADS, HEAD_DIM, min(win, tm)),
                         lambda b, i, first=n_tiles - max(win // tm, 1): (b, 0, 0, 0, jnp.maximum(i - first, 0))),
            jax.ShapeDtypeStruct((batch, 2, HEADS, HEAD_DIM, win), F32)) for win in wins]
    outs = split(ATTN_OUT) + split(2 * ATTN_OUT) + kvt
    res = pl.pallas_call(
        functools.partial(_qkv_prompt_kernel, tm=tm, n_tiles=n_tiles, wins=wins),
        grid=(batch, n_tiles),
        in_specs=[
            pl.BlockSpec((1, tm, D_MODEL), lambda b, i: (b, i, 0)),
            _const_spec((1, D_MODEL)),
            _const_spec((D_MODEL, Q_COLS + KV_COLS)),
            _const_spec((KV_COLS, D_MODEL)),
        ],
        out_specs=[o[0] for o in outs],
        out_shape=[o[1] for o in outs],
        scratch_shapes=[pltpu.VMEM((2 * ATTN_OUT // LANES, tm, LANES), F32)],
        compiler_params=_cparams("parallel", "arbitrary"),
        name="qkv_prompt",
    )(x.reshape(batch, seq, D_MODEL), g.reshape(1, D_MODEL), w, w_t)
    return res[0:3], res[3:6], res[6:9]


def _attn_prompt_kernel(q_ref, kvp_ref, kvc_ref, bias_ref, cap_ref, o_ref, st_ref, s_scr, p_scr):
    n = pl.program_id(2)
    first = jnp.where(n > 0, 0, 1)
    lane = lax.broadcasted_iota(jnp.int32, (BAND, LANES), 1)
    low_half = lane < HEAD_DIM
    half_sel = [low_half.astype(F32).astype(BF16), (~low_half).astype(F32).astype(BF16)]

    for pair in range(HEADS // 2):
        cols = slice(pair * LANES, (pair + 1) * LANES)
        q2 = q_ref[0, 0, :, cols]
        k2 = jnp.concatenate([kvp_ref[0, 0, :, cols], kvc_ref[0, 0, :, cols]], axis=0)
        for half in range(2):
            s_scr[2 * pair + half] = lax.dot_general(q2 * half_sel[half], k2, _NT, preferred_element_type=F32)

    stats = jnp.zeros((BAND, LANES), F32)
    for h in range(HEADS):
        t = jnp.minimum(s_scr[h] + bias_ref[h], cap_ref[first])
        mx = jnp.max(t, axis=-1, keepdims=True)
        e = jnp.exp(t - mx)
        den = jnp.sum(e, axis=-1, keepdims=True)
        p_scr[h] = e.astype(BF16)
        stats = jnp.where(lane == h, mx, jnp.where(lane == HEADS + h, den, stats))
    st_ref[0, 0] = stats

    for pair in range(HEADS // 2):
        cols = slice(pair * LANES, (pair + 1) * LANES)
        v2 = jnp.concatenate([kvp_ref[0, 0, :, ATTN_OUT + cols.start:ATTN_OUT + cols.stop],
                              kvc_ref[0, 0, :, ATTN_OUT + cols.start:ATTN_OUT + cols.stop]], axis=0)
        o_lo = jnp.dot(p_scr[2 * pair], v2, preferred_element_type=F32)
        o_hi = jnp.dot(p_scr[2 * pair + 1], v2, preferred_element_type=F32)
        o_ref[0, 0, :, cols] = jnp.where(low_half, o_lo, o_hi)


def _attn_prompt(q, kv, bias, cap):
    batch, dil, sub, _ = q.shape
    cur = lambda b, r, n: (b, r, n, 0)
    return pl.pallas_call(
        _attn_prompt_kernel,
        grid=(batch, dil, sub // BAND),
        in_specs=[
            pl.BlockSpec((1, 1, BAND, ATTN_OUT), cur),
            pl.BlockSpec((1, 1, BAND, 2 * ATTN_OUT), lambda b, r, n: (b, r, jnp.maximum(n - 1, 0), 0)),
            pl.BlockSpec((1, 1, BAND, 2 * ATTN_OUT), cur),
            _const_spec((HEADS, BAND, 2 * BAND)),
            _const_spec((2, BAND, 2 * BAND)),
        ],
        out_specs=[
            pl.BlockSpec((1, 1, BAND, ATTN_OUT), cur),
            pl.BlockSpec((1, 1, BAND, LANES), cur),
        ],
        out_shape=[
            jax.ShapeDtypeStruct((batch, dil, sub, ATTN_OUT), F32),
            jax.ShapeDtypeStruct((batch, dil, sub, LANES), F32),
        ],
        scratch_shapes=[pltpu.VMEM((HEADS, BAND, 2 * BAND), F32), pltpu.VMEM((HEADS, BAND, 2 * BAND), BF16)],
        compiler_params=_cparams("parallel", "parallel", "arbitrary"),
        name="attn_prompt_d%d" % dil,
    )(q, kv, kv, bias, cap)


def _prompt_bias(rel_bias_g, dil):
    tab = rel_bias_g.astype(F32)[_t5_bucket(np.arange(BAND, -1, -1) * dil)]
    ext = jnp.pad(tab.T, ((0, 0), (BAND, BAND - 1)))
    return jnp.stack([ext[:, BAND - i:3 * BAND - i] for i in range(BAND)], axis=1)


def _prompt_cap():
    m = BAND + np.arange(BAND)[:, None] - np.arange(2 * BAND)[None, :]
    band = (m >= 0) & (m <= BAND)
    no_prev = band & (np.arange(2 * BAND) >= BAND)[None, :]
    big = np.finfo(np.float32).max
    return jnp.asarray(np.where(np.stack([band, no_prev]), big, NEG_INF).astype(np.float32))


def _wo_merge_kernel(o0_ref, o1_ref, o2_ref, l0_ref, l1_ref, l2_ref, e_ref, w_ref, x_ref, out_ref,
                     oscr_ref, lscr_ref, *, tm):
    def token_order(ref, scr_ref, dil):
        if dil == 1:
            return ref[0, 0]
        chunks = ref.shape[-1] // LANES
        for r in range(dil):
            for c in range(chunks):
                scr_ref[c, pl.ds(r, tm // dil, stride=dil), :] = ref[0, r, :, c * LANES:(c + 1) * LANES]
        return jnp.concatenate([scr_ref[c] for c in range(chunks)], axis=1)

    dils = [dil for _, dil in ATTN_GROUPS]
    sts = [token_order(l_ref, lscr_ref.at[g:g + 1], dils[g]) for g, l_ref in enumerate((l0_ref, l1_ref, l2_ref))]
    mxs = [st[:, :HEADS] for st in sts]
    dens = [st[:, HEADS:2 * HEADS] for st in sts]
    mx = jnp.maximum(jnp.maximum(mxs[0], mxs[1]), mxs[2])
    es = [jnp.exp(m - mx) for m in mxs]
    inv = 1.0 / (es[0] * dens[0] + es[1] * dens[1] + es[2] * dens[2])
    acc = None
    for g, o_ref in enumerate((o0_ref, o1_ref, o2_ref)):
        wt = es[g] * inv
        hi = wt.astype(BF16)
        lo = (wt - hi.astype(F32)).astype(BF16)
        wexp = (jnp.dot(hi, e_ref[...], preferred_element_type=F32)
                + jnp.dot(lo, e_ref[...], preferred_element_type=F32))
        term = wexp * token_order(o_ref, oscr_ref, dils[g])
        acc = term if acc is None else acc + term
    out_ref[0] = x_ref[0] + jnp.dot(acc.astype(BF16), w_ref[...], preferred_element_type=F32)


def _wo_merge(os, lses, w_o, x, batch, seq):
    tm = min(ROW_TILE, seq)
    expand = jnp.asarray(np.repeat(np.eye(HEADS, dtype=np.float32), HEAD_DIM, axis=1), BF16)
    split = lambda width: [pl.BlockSpec((1, dil, tm // dil, width), lambda b, i: (b, 0, i, 0))
                           for _, dil in ATTN_GROUPS]
    row = pl.BlockSpec((1, tm, D_MODEL), lambda b, i: (b, i, 0))
    out = pl.pallas_call(
        functools.partial(_wo_merge_kernel, tm=tm),
        grid=(batch, seq // tm),
        in_specs=split(ATTN_OUT) + split(LANES) + [
            _const_spec((HEADS, ATTN_OUT)),
            _const_spec((ATTN_OUT, D_MODEL)),
            row,
        ],
        out_specs=row,
        out_shape=jax.ShapeDtypeStruct((batch, seq, D_MODEL), F32),
        scratch_shapes=[pltpu.VMEM((ATTN_OUT // LANES, tm, LANES), F32), pltpu.VMEM((N_GROUPS, tm, LANES), F32)],
        compiler_params=_cparams("parallel", "parallel"),
        name="wo_merge",
    )(*os, *lses, expand, w_o, x.reshape(batch, seq, D_MODEL))
    return out.reshape(batch * seq, D_MODEL)


def _wo_kernel(o_ref, w_ref, x_ref, out_ref):
    out_ref[...] = x_ref[...] + jnp.dot(o_ref[...].astype(BF16), w_ref[...], preferred_element_type=F32)


def _wo(o, w_o, x):
    m = x.shape[0]
    tm = min(ROW_TILE, m)
    row = lambda i: (i, 0)
    return pl.pallas_call(
        _wo_kernel,
        grid=(m // tm,),
        in_specs=[pl.BlockSpec((tm, ATTN_OUT), row), _const_spec((ATTN_OUT, D_MODEL)),
                  pl.BlockSpec((tm, D_MODEL), row)],
        out_specs=pl.BlockSpec((tm, D_MODEL), row),
        out_shape=jax.ShapeDtypeStruct((m, D_MODEL), F32),
        compiler_params=_cparams("parallel"),
        name="wo",
    )(o, w_o, x)


Q_PAD = 8
NEW_PAD = 16


def _sample_tables(rel_bias, dec_seq, buf_rows):
    tq = np.arange(Q_PAD)
    tables = []
    for g, (_, dil) in enumerate(ATTN_GROUPS):
        wb = buf_rows[g]
        by_dist = rel_bias[:, g].astype(F32)[_t5_bucket(np.arange(wb + Q_PAD))]
        rev = by_dist[::-1].T
        for kpos, real in ((np.arange(wb), np.ones(wb, bool)),
                           (wb + np.arange(NEW_PAD), np.arange(NEW_PAD) < dec_seq)):
            delta = (wb + tq)[:, None] - kpos[None, :]
            ok = (delta >= 0) & (delta % dil == 0) & (delta // dil <= BAND)
            ok &= (tq < dec_seq)[:, None] & real[None, :]
            if kpos.shape[0] == wb:
                bias = jnp.stack([rev[:, Q_PAD - 1 - t:Q_PAD - 1 - t + wb] for t in range(Q_PAD)], axis=1)
            else:
                bias = jnp.transpose(by_dist[np.maximum(delta, 0)], (2, 0, 1))
            tables.append(bias)
            tables.append(jnp.asarray(ok.astype(np.float32)))
    return tables


def _attn_sample_kernel(q_ref, kvn_ref, c0_ref, c1_ref, c2_ref, *rest):
    tables, o_ref = rest[:-1], rest[-1]
    for h in range(HEADS):
        parts = []
        for g, c_ref in enumerate((c0_ref, c1_ref, c2_ref)):
            bias_ref, valid_ref, nbias_ref, nvalid_ref = tables[4 * g:4 * g + 4]
            q = q_ref[0, g, h]
            s = jnp.dot(q, c_ref[0, h].astype(BF16), preferred_element_type=F32)
            parts.append((jnp.where(valid_ref[...] > 0.0, s + bias_ref[h], NEG_INF), c_ref, None))
            s = lax.dot_general(q, kvn_ref[0, g, 0, h], _NT, preferred_element_type=F32)
            parts.append((jnp.where(nvalid_ref[...] > 0.0, s + nbias_ref[h], NEG_INF), None, g))
        mx = None
        for s, _, _ in parts:
            pm = jnp.max(s, axis=-1, keepdims=True)
            mx = pm if mx is None else jnp.maximum(mx, pm)
        den = jnp.zeros((Q_PAD, 1), F32)
        acc = jnp.zeros((Q_PAD, HEAD_DIM), F32)
        for s, c_ref, g in parts:
            e = jnp.exp(s - mx)
            den = den + jnp.sum(e, axis=-1, keepdims=True)
            e = e.astype(BF16)
            if c_ref is not None:
                acc = acc + lax.dot_general(e, c_ref[1, h].astype(BF16), _NT, preferred_element_type=F32)
            else:
                acc = acc + jnp.dot(e, kvn_ref[0, g, 1, h], preferred_element_type=F32)
        o_ref[0, h] = acc / den


def _attn_sample(q, kvb, caches, li, tables, n, dec_seq):
    q5 = q.reshape(n, dec_seq, N_GROUPS, HEADS, HEAD_DIM).transpose(0, 2, 3, 1, 4)
    q5 = jnp.pad(q5, ((0, 0), (0, 0), (0, 0), (0, Q_PAD - dec_seq), (0, 0)))
    kvn = kvb.reshape(n, dec_seq, N_GROUPS, 2, HEADS, HEAD_DIM).transpose(0, 2, 3, 4, 1, 5)
    kvn = jnp.pad(kvn, ((0, 0), (0, 0), (0, 0), (0, 0), (0, NEW_PAD - dec_seq), (0, 0)))
    cache_spec = lambda c: pl.BlockSpec((None, None) + c.shape[2:], lambda i: (li, i, 0, 0, 0, 0))
    o = pl.pallas_call(
        _attn_sample_kernel,
        grid=(n,),
        in_specs=[
            pl.BlockSpec((1, N_GROUPS, HEADS, Q_PAD, HEAD_DIM), lambda i: (i, 0, 0, 0, 0)),
            pl.BlockSpec((1, N_GROUPS, 2, HEADS, NEW_PAD, HEAD_DIM), lambda i: (i, 0, 0, 0, 0, 0)),
        ] + [cache_spec(c) for c in caches] + [_const_spec(t.shape) for t in tables],
        out_specs=pl.BlockSpec((1, HEADS, Q_PAD, HEAD_DIM), lambda i: (i, 0, 0, 0)),
        out_shape=jax.ShapeDtypeStruct((n, HEADS, Q_PAD, HEAD_DIM), F32),
        compiler_params=_cparams("parallel"),
        name="attn_sample",
    )(q5, kvn, *caches, *tables)
    return o[:, :, :dec_seq].transpose(0, 2, 1, 3).reshape(n * dec_seq, ATTN_OUT)


HALO = 16


def _pool_prompt_kernel(x_ref, g_ref, w_ref, sc_ref, o_ref, st_ref, hp_ref, *, tt):
    i = pl.program_id(1)

    @pl.when(i == 0)
    def _():
        hp_ref[0:HALO, :] = jnp.zeros((HALO, D_MODEL), F32)

    x = x_ref[0]
    h = _rms(x, g_ref[...])
    hp_ref[HALO:HALO + tt, :] = h
    pos1 = (i * tt + 1 + lax.broadcasted_iota(jnp.int32, (tt, 1), 0)).astype(F32)
    ys = []
    for g, w in enumerate(POOL_WINDOWS):
        cols = slice(g * POOL_CH, (g + 1) * POOL_CH)
        hg = h[:, cols]
        win = hg
        for j in range(1, w):
            win = win + hp_ref[HALO - j:HALO - j + tt, cols]
        p = win / jnp.minimum(float(w), pos1) - hg
        ys.append(jnp.dot(p.astype(BF16), w_ref[g], preferred_element_type=F32))
    o_ref[0] = x + jnp.concatenate(ys, axis=1) * sc_ref[...]
    tail = hp_ref[tt:tt + HALO, :]
    hp_ref[0:HALO, :] = tail

    @pl.when(i == pl.num_programs(1) - 1)
    def _():
        st_ref[0] = tail


def _pool_prompt(x, g, w_pool, scale, batch, seq):
    tt = min(ROW_TILE, seq)
    out, st = pl.pallas_call(
        functools.partial(_pool_prompt_kernel, tt=tt),
        grid=(batch, seq // tt),
        in_specs=[
            pl.BlockSpec((1, tt, D_MODEL), lambda b, i: (b, i, 0)),
            _const_spec((1, D_MODEL)),
            _const_spec(w_pool.shape),
            _const_spec((1, D_MODEL)),
        ],
        out_specs=[
            pl.BlockSpec((1, tt, D_MODEL), lambda b, i: (b, i, 0)),
            pl.BlockSpec((1, HALO, D_MODEL), lambda b, i: (b, 0, 0)),
        ],
        out_shape=[
            jax.ShapeDtypeStruct((batch, seq, D_MODEL), F32),
            jax.ShapeDtypeStruct((batch, HALO, D_MODEL), F32),
        ],
        scratch_shapes=[pltpu.VMEM((HALO + tt, D_MODEL), F32)],
        compiler_params=_cparams("parallel", "arbitrary"),
        name="pool_prompt",
    )(x.reshape(batch, seq, D_MODEL), g.reshape(1, D_MODEL), w_pool, scale.reshape(1, D_MODEL))
    return out.reshape(batch * seq, D_MODEL), st[:, HALO - POOL_STATE:, :]


def _pool_sample_kernel(x_ref, st_ref, g_ref, w_ref, sc_ref, o_ref, nst_ref, *, dec_seq, past_len):
    chunk = lambda ref, k: ref[:, k * D_MODEL:(k + 1) * D_MODEL]
    xs = [chunk(x_ref, t) for t in range(dec_seq)]
    hs = [_rms(x, g_ref[...]) for x in xs]
    rows = [st_ref[k] for k in range(POOL_STATE)] + hs
    for t in range(dec_seq):
        ys = []
        for g, w in enumerate(POOL_WINDOWS):
            cols = slice(g * POOL_CH, (g + 1) * POOL_CH)
            win = rows[POOL_STATE + t][:, cols]
            for j in range(1, w):
                win = win + rows[POOL_STATE + t - j][:, cols]
            p = win / float(min(w, past_len + t + 1)) - hs[t][:, cols]
            ys.append(jnp.dot(p.astype(BF16), w_ref[g], preferred_element_type=F32))
        o_ref[:, t * D_MODEL:(t + 1) * D_MODEL] = xs[t] + jnp.concatenate(ys, axis=1) * sc_ref[...]
    new_rows = rows[-POOL_STATE:]
    for k in range(POOL_STATE):
        nst_ref[k] = new_rows[k]


def _pool_sample(x, state_t, li, g, w_pool, scale, n, dec_seq, past_len):
    bn = min(32, n)
    out, nst = pl.pallas_call(
        functools.partial(_pool_sample_kernel, dec_seq=dec_seq, past_len=past_len),
        grid=(n // bn,),
        in_specs=[
            pl.BlockSpec((bn, dec_seq * D_MODEL), lambda i: (i, 0)),
            pl.BlockSpec((None, POOL_STATE, bn, D_MODEL), lambda i: (li, 0, i, 0)),
            _const_spec((1, D_MODEL)),
            _const_spec(w_pool.shape),
            _const_spec((1, D_MODEL)),
        ],
        out_specs=[
            pl.BlockSpec((bn, dec_seq * D_MODEL), lambda i: (i, 0)),
            pl.BlockSpec((POOL_STATE, bn, D_MODEL), lambda i: (0, i, 0)),
        ],
        out_shape=[
            jax.ShapeDtypeStruct((n, dec_seq * D_MODEL), F32),
            jax.ShapeDtypeStruct((POOL_STATE, n, D_MODEL), F32),
        ],
        compiler_params=_cparams("parallel"),
        name="pool_sample",
    )(x.reshape(n, dec_seq * D_MODEL), state_t, g.reshape(1, D_MODEL), w_pool, scale.reshape(1, D_MODEL))
    return out.reshape(n * dec_seq, D_MODEL), nst


def _permute_qkv_weight(w):
    w = w.reshape(D_MODEL, 3, N_GROUPS, ATTN_OUT)
    q = w[:, 0].reshape(D_MODEL, Q_COLS)
    kv = jnp.stack([w[:, 1], w[:, 2]], axis=2).reshape(D_MODEL, KV_COLS)
    return jnp.concatenate([q, kv], axis=1)


def kernel(x_prompt, x_sample, state_pool, cache_kv_w128, cache_kv_w512, cache_kv_w2048, rel_bias, norm_mix,
           norm_ffn, norm_final, w_pool, pool_scale, w_qkv, w_o, w_up, w_down):
    batch, seq, _ = x_prompt.shape
    n, dec_seq, _ = x_sample.shape
    depth = norm_mix.shape[0]
    caches = tuple(jnp.transpose(c, (0, 1, 3, 4, 5, 2)) for c in (cache_kv_w128, cache_kv_w512, cache_kv_w2048))
    state_t = jnp.transpose(state_pool, (0, 2, 1, 3))
    past_len = PAST_LEN

    xp = x_prompt.reshape(batch * seq, D_MODEL)
    xs = x_sample.reshape(n * dec_seq, D_MODEL)
    w_up_b = w_up.astype(BF16)
    w_down_b = w_down.astype(BF16)
    w_pool_b = w_pool.astype(BF16)
    w_o_b = w_o.astype(BF16)
    w_qkv_b = jax.vmap(_permute_qkv_weight)(w_qkv).astype(BF16)
    w_kv_t = jnp.transpose(w_qkv_b[:, :, Q_COLS:], (0, 2, 1))
    prompt_bias = [_prompt_bias(rel_bias[:, g], dil) for g, (_, dil) in enumerate(ATTN_GROUPS)]
    prompt_cap = _prompt_cap()
    sample_tables = _sample_tables(rel_bias, dec_seq, tuple(c.shape[-1] for c in caches))

    kv_rows_major = lambda per_layer: jnp.transpose(jnp.stack(per_layer), (0, 1, 5, 2, 3, 4))

    pool_p, pool_s = [], []
    kv_p = [[] for _ in ATTN_GROUPS]
    kv_s = [[] for _ in ATTN_GROUPS]
    for i in range(depth):
        li = i // 2
        if i % 2 == 0:
            xp, sp = _pool_prompt(xp, norm_mix[i], w_pool_b[li], pool_scale[li], batch, seq)
            xs, ss = _pool_sample(xs, state_t, li, norm_mix[i], w_pool_b[li], pool_scale[li], n, dec_seq,
                                  past_len)
            pool_p.append(sp)
            pool_s.append(ss)
        else:
            qps, kvps, kvts = _qkv_prompt(xp, norm_mix[i], w_qkv_b[li], w_kv_t[li], batch, seq)
            qs, kvbs, kvfs = _qkv(xs, norm_mix[i], w_qkv_b[li])
            os, lses = zip(*[_attn_prompt(qps[g], kvps[g], prompt_bias[g], prompt_cap) for g in range(N_GROUPS)])
            xp = _wo_merge(os, lses, w_o_b[li], xp, batch, seq)
            o_s = _attn_sample(qs, kvbs, caches, li, sample_tables, n, dec_seq)
            xs = _wo(o_s, w_o_b[li], xs)
            kvfs = kvfs.reshape(n, dec_seq, N_GROUPS, 2, HEADS, HEAD_DIM)
            for g in range(N_GROUPS):
                kv_p[g].append(kvts[g])
                kv_s[g].append(kvfs[:, :, g])
        last = i == depth - 1
        xp = _mlp(xp, norm_ffn[i], w_up_b, w_down_b, i, norm_final, last)
        xs = _mlp(xs, norm_ffn[i], w_up_b, w_down_b, i, norm_final, last)
    return (xp.reshape(batch, seq, D_MODEL), xs.reshape(n, dec_seq, D_MODEL),
            jnp.stack(pool_p), jnp.transpose(jnp.stack(pool_s), (0, 2, 1, 3)),
            kv_rows_major(kv_p[0]), jnp.stack(kv_s[0]),
            kv_rows_major(kv_p[1]), jnp.stack(kv_s[1]),
            kv_rows_major(kv_p[2]), jnp.stack(kv_s[2]))
```

```python
import functools

import numpy as np
import jax
import jax.numpy as jnp
from jax import lax
from jax.experimental import pallas as pl
from jax.experimental.pallas import tpu as pltpu

F32 = jnp.float32
BF16 = jnp.bfloat16

D_MODEL = 1024
D_FF = 4 * D_MODEL
POOL_WINDOWS = (2, 4, 8, 16)
POOL_CH = D_MODEL // len(POOL_WINDOWS)
POOL_STATE = max(POOL_WINDOWS) - 1
ATTN_GROUPS = ((128, 1), (512, 4), (2048, 16))
N_GROUPS = len(ATTN_GROUPS)
HEADS = 8
HEAD_DIM = 64
ATTN_OUT = HEADS * HEAD_DIM
BAND = 128
N_BUCKETS = 32
MAX_EXACT = N_BUCKETS // 2
REL_MAX_DIST = 2048
PAST_LEN = 2048
RMS_EPS = 1e-6
NEG_INF = -1e30

VMEM_LIMIT_BYTES = 56 * 1024 * 1024
ROW_TILE = 512
FF_CHUNK = 1024
_NT = (((1,), (1,)), ((), ()))
LANES = 128


def _cparams(*sem):
    return pltpu.CompilerParams(dimension_semantics=sem, vmem_limit_bytes=VMEM_LIMIT_BYTES)


def _rms(x, g):
    ms = jnp.mean(x * x, axis=-1, keepdims=True)
    return x * lax.rsqrt(ms + RMS_EPS) * g


def _t5_bucket(dist):
    n = np.maximum(np.asarray(dist), 0)
    large = MAX_EXACT + (np.log(np.maximum(n, 1) / MAX_EXACT) / np.log(REL_MAX_DIST / MAX_EXACT)
                         * (N_BUCKETS - MAX_EXACT)).astype(np.int64)
    large = np.minimum(large, N_BUCKETS - 1)
    return np.where(n < MAX_EXACT, n, large).astype(np.int32)


def _const_spec(shape):
    nd = len(shape)
    return pl.BlockSpec(shape, lambda *_: (0,) * nd, pipeline_mode=pl.Buffered(1))


def _layer_spec(shape, layer):
    nd = len(shape)
    return pl.BlockSpec((None,) + tuple(shape), lambda *_: (layer,) + (0,) * nd, pipeline_mode=pl.Buffered(1))


def _mlp_kernel(x_ref, g_ref, wu_ref, wd_ref, gf_ref, o_ref, *, final_norm):
    x = x_ref[...]
    h = _rms(x, g_ref[...]).astype(BF16)
    acc = x
    for c in range(D_FF // FF_CHUNK):
        cols = slice(c * FF_CHUNK, (c + 1) * FF_CHUNK)
        u = jnp.dot(h, wu_ref[:, cols], preferred_element_type=F32)
        a = jnp.square(jnp.maximum(u, 0.0)).astype(BF16)
        acc = acc + jnp.dot(a, wd_ref[cols, :], preferred_element_type=F32)
    if final_norm:
        acc = _rms(acc, gf_ref[...])
    o_ref[...] = acc


def _mlp(x, g, w_up, w_down, layer, g_final, final_norm):
    m = x.shape[0]
    tm = min(ROW_TILE, m)
    return pl.pallas_call(
        functools.partial(_mlp_kernel, final_norm=final_norm),
        grid=(m // tm,),
        in_specs=[
            pl.BlockSpec((tm, D_MODEL), lambda i: (i, 0)),
            _const_spec((1, D_MODEL)),
            _layer_spec((D_MODEL, D_FF), layer),
            _layer_spec((D_FF, D_MODEL), layer),
            _const_spec((1, D_MODEL)),
        ],
        out_specs=pl.BlockSpec((tm, D_MODEL), lambda i: (i, 0)),
        out_shape=jax.ShapeDtypeStruct((m, D_MODEL), F32),
        compiler_params=_cparams("parallel"),
        name="mlp",
    )(x, g.reshape(1, D_MODEL), w_up, w_down, g_final.reshape(1, D_MODEL))


Q_COLS = N_GROUPS * ATTN_OUT
KV_COLS = 2 * N_GROUPS * ATTN_OUT


def _w_cols(which, g):
    start = (which * N_GROUPS + g) * ATTN_OUT
    return slice(start, start + ATTN_OUT)


def _qkv_kernel(x_ref, g_ref, w_ref, q_ref, kvb_ref, kvf_ref):
    h = _rms(x_ref[...], g_ref[...]).astype(BF16)
    q = jnp.dot(h, w_ref[:, :Q_COLS], preferred_element_type=F32)
    q_ref[...] = (q * (HEAD_DIM ** -0.5)).astype(BF16)
    for g in range(N_GROUPS):
        for which in (1, 2):
            out = slice((2 * g + which - 1) * ATTN_OUT, (2 * g + which) * ATTN_OUT)
            kv = jnp.dot(h, w_ref[:, _w_cols(which, g)], preferred_element_type=F32)
            kvf_ref[:, out] = kv
            kvb_ref[:, out] = kv.astype(BF16)


def _qkv(x, g, w):
    m = x.shape[0]
    tm = min(ROW_TILE, m)
    row = lambda i: (i, 0)
    return pl.pallas_call(
        _qkv_kernel,
        grid=(m // tm,),
        in_specs=[
            pl.BlockSpec((tm, D_MODEL), row),
            _const_spec((1, D_MODEL)),
            _const_spec((D_MODEL, Q_COLS + KV_COLS)),
        ],
        out_specs=[
            pl.BlockSpec((tm, Q_COLS), row),
            pl.BlockSpec((tm, KV_COLS), row),
            pl.BlockSpec((tm, KV_COLS), row),
        ],
        out_shape=[
            jax.ShapeDtypeStruct((m, Q_COLS), BF16),
            jax.ShapeDtypeStruct((m, KV_COLS), BF16),
            jax.ShapeDtypeStruct((m, KV_COLS), F32),
        ],
        compiler_params=_cparams("parallel"),
        name="qkv",
    )(x, g.reshape(1, D_MODEL), w)


def _qkv_prompt_kernel(x_ref, g_ref, w_ref, wt_ref, *rest, tm, n_tiles, wins):
    q_refs, kv_refs, kvt_refs, scr_ref = rest[0:3], rest[3:6], rest[6:9], rest[9]
    i = pl.program_id(1)
    h = _rms(x_ref[0], g_ref[...]).astype(BF16)
    chunks = ATTN_OUT // LANES
    slot = [0]

    def write_split(ref, off, val, dil):
        if dil == 1:
            ref[0, 0, :, off:off + ATTN_OUT] = val.astype(BF16)
            return
        base = slot[0]
        slot[0] += chunks
        for c in range(chunks):
            scr_ref[base + c] = val[:, c * LANES:(c + 1) * LANES]
        for r in range(dil):
            rows = [scr_ref[base + c, pl.ds(r, tm // dil, stride=dil), :] for c in range(chunks)]
            ref[0, r, :, off:off + ATTN_OUT] = jnp.concatenate(rows, axis=1).astype(BF16)

    for g, (_, dil) in enumerate(ATTN_GROUPS):
        q = jnp.dot(h, w_ref[:, _w_cols(0, g)], preferred_element_type=F32)
        write_split(q_refs[g], 0, q * (HEAD_DIM ** -0.5), dil)
        for which in (1, 2):
            kv = jnp.dot(h, w_ref[:, _w_cols(which, g)], preferred_element_type=F32)
            write_split(kv_refs[g], (which - 1) * ATTN_OUT, kv, dil)

    for g, win in enumerate(wins):
        rows = min(win, tm)
        first_tile = n_tiles - max(win // tm, 1)

        @pl.when(i >= first_tile)
        def _(g=g, rows=rows):
            for which in (1, 2):
                start = ((which - 1) * N_GROUPS + g) * ATTN_OUT
                kvt = lax.dot_general(wt_ref[start:start + ATTN_OUT, :], h[tm - rows:, :], _NT,
                                      preferred_element_type=F32)
                kvt_refs[g][0, which - 1] = kvt.reshape(HEADS, HEAD_DIM, rows)


def _qkv_prompt(x, g, w, w_t, batch, seq):
    tm = min(ROW_TILE, seq)
    n_tiles = seq // tm
    wins = tuple(min(win, seq) for win, _ in ATTN_GROUPS)
    assert all(w_ % tm == 0 or tm % w_ == 0 for w_ in wins)
    split = lambda width: [
        (pl.BlockSpec((1, dil, tm // dil, width), lambda b, i: (b, 0, i, 0)),
         jax.ShapeDtypeStruct((batch, dil, seq // dil, width), BF16)) for _, dil in ATTN_GROUPS]
    kvt = [(pl.BlockSpec((1, 2, HEADS, HEAD_DIM, min(win, tm)),
                         lambda b, i, first=n_tiles - max(win // tm, 1): (b, 0, 0, 0, jnp.maximum(i - first, 0))),
            jax.ShapeDtypeStruct((batch, 2, HEADS, HEAD_DIM, win), F32)) for win in wins]
    outs = split(ATTN_OUT) + split(2 * ATTN_OUT) + kvt
    n_split = sum(dil > 1 for _, dil in ATTN_GROUPS)
    res = pl.pallas_call(
        functools.partial(_qkv_prompt_kernel, tm=tm, n_tiles=n_tiles, wins=wins),
        grid=(batch, n_tiles),
        in_specs=[
            pl.BlockSpec((1, tm, D_MODEL), lambda b, i: (b, i, 0)),
            _const_spec((1, D_MODEL)),
            _const_spec((D_MODEL, Q_COLS + KV_COLS)),
            _const_spec((KV_COLS, D_MODEL)),
        ],
        out_specs=[o[0] for o in outs],
        out_shape=[o[1] for o in outs],
        scratch_shapes=[pltpu.VMEM((3 * n_split * ATTN_OUT // LANES, tm, LANES), F32)],
        compiler_params=_cparams("parallel", "arbitrary"),
        name="qkv_prompt",
    )(x.reshape(batch, seq, D_MODEL), g.reshape(1, D_MODEL), w, w_t)
    return res[0:3], res[3:6], res[6:9]


def _attn_prompt_kernel(q_ref, kvp_ref, kvc_ref, bias_ref, cap_ref, o_ref, st_ref, s_scr, p_scr):
    n = pl.program_id(2)
    lane = lax.broadcasted_iota(jnp.int32, (BAND, LANES), 1)
    low_half = lane < HEAD_DIM
    half_sel = [low_half.astype(F32).astype(BF16), (~low_half).astype(F32).astype(BF16)]

    for blk in range(BLOCKS_PER_STEP):
        rows = slice(blk * BAND, (blk + 1) * BAND)
        if blk == 0:
            first = jnp.where(n > 0, 0, 1)
            prev = lambda cols: kvp_ref[0, 0, :, cols]
        else:
            first = 0
            prev = lambda cols, blk=blk: kvc_ref[0, 0, (blk - 1) * BAND:blk * BAND, cols]
        keys = lambda cols, prev=prev, rows=rows: jnp.concatenate([prev(cols), kvc_ref[0, 0, rows, cols]], axis=0)

        for pair in range(HEADS // 2):
            cols = slice(pair * LANES, (pair + 1) * LANES)
            q2 = q_ref[0, 0, rows, cols]
            k2 = keys(cols)
            for half in range(2):
                s_scr[2 * pair + half] = lax.dot_general(q2 * half_sel[half], k2, _NT,
                                                         preferred_element_type=F32)

        stats = jnp.zeros((BAND, LANES), F32)
        for h in range(HEADS):
            t = jnp.minimum(s_scr[h] + bias_ref[h], cap_ref[first])
            mx = jnp.max(t, axis=-1, keepdims=True)
            e = jnp.exp(t - mx)
            den = jnp.sum(e, axis=-1, keepdims=True)
            p_scr[h] = e.astype(BF16)
            stats = jnp.where(lane == h, mx, jnp.where(lane == HEADS + h, den, stats))
        st_ref[0, 0, rows, :] = stats

        for pair in range(HEADS // 2):
            cols = slice(pair * LANES, (pair + 1) * LANES)
            v2 = keys(slice(ATTN_OUT + cols.start, ATTN_OUT + cols.stop))
            o_lo = jnp.dot(p_scr[2 * pair], v2, preferred_element_type=F32)
            o_hi = jnp.dot(p_scr[2 * pair + 1], v2, preferred_element_type=F32)
            o_ref[0, 0, rows, cols] = jnp.where(low_half, o_lo, o_hi)


BLOCKS_PER_STEP = 2


def _attn_prompt(q, kv, bias, cap):
    batch, dil, sub, _ = q.shape
    step = BLOCKS_PER_STEP * BAND
    assert sub % step == 0
    cur = lambda b, r, n: (b, r, n, 0)
    return pl.pallas_call(
        _attn_prompt_kernel,
        grid=(batch, dil, sub // step),
        in_specs=[
            pl.BlockSpec((1, 1, step, ATTN_OUT), cur),
            pl.BlockSpec((1, 1, BAND, 2 * ATTN_OUT),
                         lambda b, r, n: (b, r, jnp.maximum(n * BLOCKS_PER_STEP - 1, 0), 0)),
            pl.BlockSpec((1, 1, step, 2 * ATTN_OUT), cur),
            _const_spec((HEADS, BAND, 2 * BAND)),
            _const_spec((2, BAND, 2 * BAND)),
        ],
        out_specs=[
            pl.BlockSpec((1, 1, step, ATTN_OUT), cur),
            pl.BlockSpec((1, 1, step, LANES), cur),
        ],
        out_shape=[
            jax.ShapeDtypeStruct((batch, dil, sub, ATTN_OUT), F32),
            jax.ShapeDtypeStruct((batch, dil, sub, LANES), F32),
        ],
        scratch_shapes=[pltpu.VMEM((HEADS, BAND, 2 * BAND), F32), pltpu.VMEM((HEADS, BAND, 2 * BAND), BF16)],
        compiler_params=_cparams("parallel", "parallel", "arbitrary"),
        name="attn_prompt_d%d" % dil,
    )(q, kv, kv, bias, cap)


def _prompt_bias(rel_bias_g, dil):
    tab = rel_bias_g.astype(F32)[_t5_bucket(np.arange(BAND, -1, -1) * dil)]
    ext = jnp.pad(tab.T, ((0, 0), (BAND, BAND)))
    tiled = jnp.tile(ext, (1, BAND))[:, :BAND * 3 * BAND].reshape(HEADS, BAND, 3 * BAND)
    return tiled[:, :, BAND:]


def _prompt_cap():
    m = BAND + np.arange(BAND)[:, None] - np.arange(2 * BAND)[None, :]
    band = (m >= 0) & (m <= BAND)
    no_prev = band & (np.arange(2 * BAND) >= BAND)[None, :]
    big = np.finfo(np.float32).max
    return jnp.asarray(np.where(np.stack([band, no_prev]), big, NEG_INF).astype(np.float32))


def _wo_merge_kernel(o0_ref, o1_ref, o2_ref, l0_ref, l1_ref, l2_ref, e_ref, w_ref, x_ref, out_ref,
                     oscr_ref, lscr_ref, *, tm):
    def token_order(ref, scr_ref, dil):
        if dil == 1:
            return ref[0, 0]
        chunks = ref.shape[-1] // LANES
        for r in range(dil):
            for c in range(chunks):
                scr_ref[c, pl.ds(r, tm // dil, stride=dil), :] = ref[0, r, :, c * LANES:(c + 1) * LANES]
        return jnp.concatenate([scr_ref[c] for c in range(chunks)], axis=1)

    dils = [dil for _, dil in ATTN_GROUPS]
    sts = [token_order(l_ref, lscr_ref.at[g:g + 1], dils[g]) for g, l_ref in enumerate((l0_ref, l1_ref, l2_ref))]
    mxs = [st[:, :HEADS] for st in sts]
    dens = [st[:, HEADS:2 * HEADS] for st in sts]
    mx = jnp.maximum(jnp.maximum(mxs[0], mxs[1]), mxs[2])
    es = [jnp.exp(m - mx) for m in mxs]
    inv = 1.0 / (es[0] * dens[0] + es[1] * dens[1] + es[2] * dens[2])
    acc = None
    for g, o_ref in enumerate((o0_ref, o1_ref, o2_ref)):
        wt = es[g] * inv
        hi = wt.astype(BF16)
        lo = (wt - hi.astype(F32)).astype(BF16)
        wexp = (jnp.dot(hi, e_ref[...], preferred_element_type=F32)
                + jnp.dot(lo, e_ref[...], preferred_element_type=F32))
        term = wexp * token_order(o_ref, oscr_ref, dils[g])
        acc = term if acc is None else acc + term
    out_ref[0] = x_ref[0] + jnp.dot(acc.astype(BF16), w_ref[...], preferred_element_type=F32)


def _wo_merge(os, lses, w_o, x, batch, seq):
    tm = min(ROW_TILE, seq)
    expand = jnp.asarray(np.repeat(np.eye(HEADS, dtype=np.float32), HEAD_DIM, axis=1), BF16)
    split = lambda width: [pl.BlockSpec((1, dil, tm // dil, width), lambda b, i: (b, 0, i, 0))
                           for _, dil in ATTN_GROUPS]
    row = pl.BlockSpec((1, tm, D_MODEL), lambda b, i: (b, i, 0))
    out = pl.pallas_call(
        functools.partial(_wo_merge_kernel, tm=tm),
        grid=(batch, seq // tm),
        in_specs=split(ATTN_OUT) + split(LANES) + [
            _const_spec((HEADS, ATTN_OUT)),
            _const_spec((ATTN_OUT, D_MODEL)),
            row,
        ],
        out_specs=row,
        out_shape=jax.ShapeDtypeStruct((batch, seq, D_MODEL), F32),
        scratch_shapes=[pltpu.VMEM((ATTN_OUT // LANES, tm, LANES), F32), pltpu.VMEM((N_GROUPS, tm, LANES), F32)],
        compiler_params=_cparams("parallel", "parallel"),
        name="wo_merge",
    )(*os, *lses, expand, w_o, x.reshape(batch, seq, D_MODEL))
    return out.reshape(batch * seq, D_MODEL)


def _wo_kernel(o_ref, w_ref, x_ref, out_ref):
    out_ref[...] = x_ref[...] + jnp.dot(o_ref[...].astype(BF16), w_ref[...], preferred_element_type=F32)


def _wo(o, w_o, x):
    m = x.shape[0]
    tm = min(ROW_TILE, m)
    row = lambda i: (i, 0)
    return pl.pallas_call(
        _wo_kernel,
        grid=(m // tm,),
        in_specs=[pl.BlockSpec((tm, ATTN_OUT), row), _const_spec((ATTN_OUT, D_MODEL)),
                  pl.BlockSpec((tm, D_MODEL), row)],
        out_specs=pl.BlockSpec((tm, D_MODEL), row),
        out_shape=jax.ShapeDtypeStruct((m, D_MODEL), F32),
        compiler_params=_cparams("parallel"),
        name="wo",
    )(o, w_o, x)


Q_PAD = 8
NEW_PAD = 16


def _sample_tables(rel_bias, dec_seq, buf_rows):
    tq = np.arange(Q_PAD)
    tables = []
    for g, (_, dil) in enumerate(ATTN_GROUPS):
        wb = buf_rows[g]
        by_dist = rel_bias[:, g].astype(F32)[_t5_bucket(np.arange(wb + Q_PAD))]
        rev = by_dist[::-1].T
        for kpos, real in ((np.arange(wb), np.ones(wb, bool)),
                           (wb + np.arange(NEW_PAD), np.arange(NEW_PAD) < dec_seq)):
            delta = (wb + tq)[:, None] - kpos[None, :]
            ok = (delta >= 0) & (delta % dil == 0) & (delta // dil <= BAND)
            ok &= (tq < dec_seq)[:, None] & real[None, :]
            if kpos.shape[0] == wb:
                bias = jnp.stack([rev[:, Q_PAD - 1 - t:Q_PAD - 1 - t + wb] for t in range(Q_PAD)], axis=1)
            else:
                bias = jnp.transpose(by_dist[np.maximum(delta, 0)], (2, 0, 1))
            tables.append(bias)
            tables.append(jnp.asarray(ok.astype(np.float32)))
    return tables


def _attn_sample_kernel(q_ref, kvn_ref, c0_ref, c1_ref, c2_ref, *rest):
    tables, o_ref = rest[:-1], rest[-1]
    for h in range(HEADS):
        parts = []
        for g, c_ref in enumerate((c0_ref, c1_ref, c2_ref)):
            bias_ref, valid_ref, nbias_ref, nvalid_ref = tables[4 * g:4 * g + 4]
            q = q_ref[0, g, h]
            s = jnp.dot(q, c_ref[0, h].astype(BF16), preferred_element_type=F32)
            parts.append((jnp.where(valid_ref[...] > 0.0, s + bias_ref[h], NEG_INF), c_ref, None))
            s = lax.dot_general(q, kvn_ref[0, g, 0, h], _NT, preferred_element_type=F32)
            parts.append((jnp.where(nvalid_ref[...] > 0.0, s + nbias_ref[h], NEG_INF), None, g))
        mx = None
        for s, _, _ in parts:
            pm = jnp.max(s, axis=-1, keepdims=True)
            mx = pm if mx is None else jnp.maximum(mx, pm)
        den = jnp.zeros((Q_PAD, 1), F32)
        acc = jnp.zeros((Q_PAD, HEAD_DIM), F32)
        for s, c_ref, g in parts:
            e = jnp.exp(s - mx)
            den = den + jnp.sum(e, axis=-1, keepdims=True)
            e = e.astype(BF16)
            if c_ref is not None:
                acc = acc + lax.dot_general(e, c_ref[1, h].astype(BF16), _NT, preferred_element_type=F32)
            else:
                acc = acc + jnp.dot(e, kvn_ref[0, g, 1, h], preferred_element_type=F32)
        o_ref[0, h] = acc / den


def _attn_sample(q, kvb, caches, li, tables, n, dec_seq):
    q5 = q.reshape(n, dec_seq, N_GROUPS, HEADS, HEAD_DIM).transpose(0, 2, 3, 1, 4)
    q5 = jnp.pad(q5, ((0, 0), (0, 0), (0, 0), (0, Q_PAD - dec_seq), (0, 0)))
    kvn = kvb.reshape(n, dec_seq, N_GROUPS, 2, HEADS, HEAD_DIM).transpose(0, 2, 3, 4, 1, 5)
    kvn = jnp.pad(kvn, ((0, 0), (0, 0), (0, 0), (0, 0), (0, NEW_PAD - dec_seq), (0, 0)))
    cache_spec = lambda c: pl.BlockSpec((None, None) + c.shape[2:], lambda i: (li, i, 0, 0, 0, 0))
    o = pl.pallas_call(
        _attn_sample_kernel,
        grid=(n,),
        in_specs=[
            pl.BlockSpec((1, N_GROUPS, HEADS, Q_PAD, HEAD_DIM), lambda i: (i, 0, 0, 0, 0)),
            pl.BlockSpec((1, N_GROUPS, 2, HEADS, NEW_PAD, HEAD_DIM), lambda i: (i, 0, 0, 0, 0, 0)),
        ] + [cache_spec(c) for c in caches] + [_const_spec(t.shape) for t in tables],
        out_specs=pl.BlockSpec((1, HEADS, Q_PAD, HEAD_DIM), lambda i: (i, 0, 0, 0)),
        out_shape=jax.ShapeDtypeStruct((n, HEADS, Q_PAD, HEAD_DIM), F32),
        compiler_params=_cparams("parallel"),
        name="attn_sample",
    )(q5, kvn, *caches, *tables)
    return o[:, :, :dec_seq].transpose(0, 2, 1, 3).reshape(n * dec_seq, ATTN_OUT)


HALO = 16


def _pool_prompt_kernel(x_ref, g_ref, w_ref, sc_ref, o_ref, st_ref, hp_ref, *, tt):
    i = pl.program_id(1)

    @pl.when(i == 0)
    def _():
        hp_ref[0:HALO, :] = jnp.zeros((HALO, D_MODEL), F32)

    x = x_ref[0]
    h = _rms(x, g_ref[...])
    hp_ref[HALO:HALO + tt, :] = h
    pos1 = (i * tt + 1 + lax.broadcasted_iota(jnp.int32, (tt, 1), 0)).astype(F32)
    ys = []
    for g, w in enumerate(POOL_WINDOWS):
        cols = slice(g * POOL_CH, (g + 1) * POOL_CH)
        hg = h[:, cols]
        win = hg
        for j in range(1, w):
            win = win + hp_ref[HALO - j:HALO - j + tt, cols]
        p = win / jnp.minimum(float(w), pos1) - hg
        ys.append(jnp.dot(p.astype(BF16), w_ref[g], preferred_element_type=F32))
    o_ref[0] = x + jnp.concatenate(ys, axis=1) * sc_ref[...]
    tail = hp_ref[tt:tt + HALO, :]
    hp_ref[0:HALO, :] = tail

    @pl.when(i == pl.num_programs(1) - 1)
    def _():
        st_ref[0] = tail


def _pool_prompt(x, g, w_pool, scale, batch, seq):
    tt = min(ROW_TILE, seq)
    out, st = pl.pallas_call(
        functools.partial(_pool_prompt_kernel, tt=tt),
        grid=(batch, seq // tt),
        in_specs=[
            pl.BlockSpec((1, tt, D_MODEL), lambda b, i: (b, i, 0)),
            _const_spec((1, D_MODEL)),
            _const_spec(w_pool.shape),
            _const_spec((1, D_MODEL)),
        ],
        out_specs=[
            pl.BlockSpec((1, tt, D_MODEL), lambda b, i: (b, i, 0)),
            pl.BlockSpec((1, HALO, D_MODEL), lambda b, i: (b, 0, 0)),
        ],
        out_shape=[
            jax.ShapeDtypeStruct((batch, seq, D_MODEL), F32),
            jax.ShapeDtypeStruct((batch, HALO, D_MODEL), F32),
        ],
        scratch_shapes=[pltpu.VMEM((HALO + tt, D_MODEL), F32)],
        compiler_params=_cparams("parallel", "arbitrary"),
        name="pool_prompt",
    )(x.reshape(batch, seq, D_MODEL), g.reshape(1, D_MODEL), w_pool, scale.reshape(1, D_MODEL))
    return out.reshape(batch * seq, D_MODEL), st[:, HALO - POOL_STATE:, :]


def _pool_sample_kernel(x_ref, st_ref, g_ref, w_ref, sc_ref, o_ref, nst_ref, *, dec_seq, past_len):
    chunk = lambda ref, k: ref[:, k * D_MODEL:(k + 1) * D_MODEL]
    xs = [chunk(x_ref, t) for t in range(dec_seq)]
    hs = [_rms(x, g_ref[...]) for x in xs]
    rows = [st_ref[k] for k in range(POOL_STATE)] + hs
    for t in range(dec_seq):
        ys = []
        for g, w in enumerate(POOL_WINDOWS):
            cols = slice(g * POOL_CH, (g + 1) * POOL_CH)
            win = rows[POOL_STATE + t][:, cols]
            for j in range(1, w):
                win = win + rows[POOL_STATE + t - j][:, cols]
            p = win / float(min(w, past_len + t + 1)) - hs[t][:, cols]
            ys.append(jnp.dot(p.astype(BF16), w_ref[g], preferred_element_type=F32))
        o_ref[:, t * D_MODEL:(t + 1) * D_MODEL] = xs[t] + jnp.concatenate(ys, axis=1) * sc_ref[...]
    new_rows = rows[-POOL_STATE:]
    for k in range(POOL_STATE):
        nst_ref[k] = new_rows[k]


def _pool_sample(x, state_t, li, g, w_pool, scale, n, dec_seq, past_len):
    bn = min(32, n)
    out, nst = pl.pallas_call(
        functools.partial(_pool_sample_kernel, dec_seq=dec_seq, past_len=past_len),
        grid=(n // bn,),
        in_specs=[
            pl.BlockSpec((bn, dec_seq * D_MODEL), lambda i: (i, 0)),
            pl.BlockSpec((None, POOL_STATE, bn, D_MODEL), lambda i: (li, 0, i, 0)),
            _const_spec((1, D_MODEL)),
            _const_spec(w_pool.shape),
            _const_spec((1, D_MODEL)),
        ],
        out_specs=[
            pl.BlockSpec((bn, dec_seq * D_MODEL), lambda i: (i, 0)),
            pl.BlockSpec((POOL_STATE, bn, D_MODEL), lambda i: (0, i, 0)),
        ],
        out_shape=[
            jax.ShapeDtypeStruct((n, dec_seq * D_MODEL), F32),
            jax.ShapeDtypeStruct((POOL_STATE, n, D_MODEL), F32),
        ],
        compiler_params=_cparams("parallel"),
        name="pool_sample",
    )(x.reshape(n, dec_seq * D_MODEL), state_t, g.reshape(1, D_MODEL), w_pool, scale.reshape(1, D_MODEL))
    return out.reshape(n * dec_seq, D_MODEL), nst


def kernel(x_prompt, x_sample, state_pool, cache_kv_w128, cache_kv_w512, cache_kv_w2048, rel_bias, norm_mix,
           norm_ffn, norm_final, w_pool, pool_scale, w_qkv, w_o, w_up, w_down):
    batch, seq, _ = x_prompt.shape
    n, dec_seq, _ = x_sample.shape
    depth = norm_mix.shape[0]
    caches = tuple(jnp.transpose(c, (0, 1, 3, 4, 5, 2)) for c in (cache_kv_w128, cache_kv_w512, cache_kv_w2048))
    state_t = jnp.transpose(state_pool, (0, 2, 1, 3))
    past_len = PAST_LEN

    xp = x_prompt.reshape(batch * seq, D_MODEL)
    xs = x_sample.reshape(n * dec_seq, D_MODEL)
    w_up_b = w_up.astype(BF16)
    w_down_b = w_down.astype(BF16)
    w_pool_b = w_pool.astype(BF16)
    w_o_b = w_o.astype(BF16)
    w_qkv_b = w_qkv.astype(BF16)
    w_kv_t = jnp.transpose(w_qkv_b[:, :, Q_COLS:], (0, 2, 1))
    prompt_bias = [_prompt_bias(rel_bias[:, g], dil) for g, (_, dil) in enumerate(ATTN_GROUPS)]
    prompt_cap = _prompt_cap()
    sample_tables = _sample_tables(rel_bias, dec_seq, tuple(c.shape[-1] for c in caches))

    kv_rows_major = lambda per_layer: jnp.transpose(jnp.stack(per_layer), (0, 1, 5, 2, 3, 4))

    pool_p, pool_s = [], []
    kv_p = [[] for _ in ATTN_GROUPS]
    kv_s = [[] for _ in ATTN_GROUPS]
    for i in range(depth):
        li = i // 2
        if i % 2 == 0:
            xp, sp = _pool_prompt(xp, norm_mix[i], w_pool_b[li], pool_scale[li], batch, seq)
            xs, ss = _pool_sample(xs, state_t, li, norm_mix[i], w_pool_b[li], pool_scale[li], n, dec_seq,
                                  past_len)
            pool_p.append(sp)
            pool_s.append(ss)
        else:
            qps, kvps, kvts = _qkv_prompt(xp, norm_mix[i], w_qkv_b[li], w_kv_t[li], batch, seq)
            qs, kvbs, kvfs = _qkv(xs, norm_mix[i], w_qkv_b[li])
            os, lses = zip(*[_attn_prompt(qps[g], kvps[g], prompt_bias[g], prompt_cap) for g in range(N_GROUPS)])
            xp = _wo_merge(os, lses, w_o_b[li], xp, batch, seq)
            o_s = _attn_sample(qs, kvbs, caches, li, sample_tables, n, dec_seq)
            xs = _wo(o_s, w_o_b[li], xs)
            kvfs = kvfs.reshape(n, dec_seq, N_GROUPS, 2, HEADS, HEAD_DIM)
            for g in range(N_GROUPS):
                kv_p[g].append(kvts[g])
                kv_s[g].append(kvfs[:, :, g])
        last = i == depth - 1
        xp = _mlp(xp, norm_ffn[i], w_up_b, w_down_b, i, norm_final, last)
        xs = _mlp(xs, norm_ffn[i], w_up_b, w_down_b, i, norm_final, last)
    return (xp.reshape(batch, seq, D_MODEL), xs.reshape(n, dec_seq, D_MODEL),
            jnp.stack(pool_p), jnp.transpose(jnp.stack(pool_s), (0, 2, 1, 3)),
            kv_rows_major(kv_p[0]), jnp.stack(kv_s[0]),
            kv_rows_major(kv_p[1]), jnp.stack(kv_s[1]),
            kv_rows_major(kv_p[2]), jnp.stack(kv_s[2]))
```

```python
import functools

import numpy as np
import jax
import jax.numpy as jnp
from jax import lax
from jax.experimental import pallas as pl
from jax.experimental.pallas import tpu as pltpu

F32 = jnp.float32
BF16 = jnp.bfloat16

D_MODEL = 1024
D_FF = 4 * D_MODEL
POOL_WINDOWS = (2, 4, 8, 16)
POOL_CH = D_MODEL // len(POOL_WINDOWS)
POOL_STATE = max(POOL_WINDOWS) - 1
ATTN_GROUPS = ((128, 1), (512, 4), (2048, 16))
N_GROUPS = len(ATTN_GROUPS)
HEADS = 8
HEAD_DIM = 64
ATTN_OUT = HEADS * HEAD_DIM
BAND = 128
N_BUCKETS = 32
MAX_EXACT = N_BUCKETS // 2
REL_MAX_DIST = 2048
PAST_LEN = 2048
RMS_EPS = 1e-6
NEG_INF = -1e30

VMEM_LIMIT_BYTES = 56 * 1024 * 1024
ROW_TILE = 512
FF_CHUNK = 1024
_NT = (((1,), (1,)), ((), ()))
LANES = 128


def _cparams(*sem):
    return pltpu.CompilerParams(dimension_semantics=sem, vmem_limit_bytes=VMEM_LIMIT_BYTES)


def _rms(x, g):
    ms = jnp.mean(x * x, axis=-1, keepdims=True)
    return x * lax.rsqrt(ms + RMS_EPS) * g


def _t5_bucket(dist):
    n = np.maximum(np.asarray(dist), 0)
    large = MAX_EXACT + (np.log(np.maximum(n, 1) / MAX_EXACT) / np.log(REL_MAX_DIST / MAX_EXACT)
                         * (N_BUCKETS - MAX_EXACT)).astype(np.int64)
    large = np.minimum(large, N_BUCKETS - 1)
    return np.where(n < MAX_EXACT, n, large).astype(np.int32)


def _const_spec(shape):
    nd = len(shape)
    return pl.BlockSpec(shape, lambda *_: (0,) * nd, pipeline_mode=pl.Buffered(1))


def _layer_spec(shape, layer):
    nd = len(shape)
    return pl.BlockSpec((None,) + tuple(shape), lambda *_: (layer,) + (0,) * nd, pipeline_mode=pl.Buffered(1))


def _mlp_kernel(x_ref, g_ref, wu_ref, wd_ref, gf_ref, o_ref, *, final_norm):
    x = x_ref[...]
    h = _rms(x, g_ref[...]).astype(BF16)
    acc = x
    for c in range(D_FF // FF_CHUNK):
        cols = slice(c * FF_CHUNK, (c + 1) * FF_CHUNK)
        u = jnp.dot(h, wu_ref[:, cols], preferred_element_type=F32)
        a = jnp.square(jnp.maximum(u, 0.0)).astype(BF16)
        acc = acc + jnp.dot(a, wd_ref[cols, :], preferred_element_type=F32)
    if final_norm:
        acc = _rms(acc, gf_ref[...])
    o_ref[...] = acc


def _mlp(x, g, w_up, w_down, layer, g_final, final_norm):
    m = x.shape[0]
    tm = min(ROW_TILE, m)
    return pl.pallas_call(
        functools.partial(_mlp_kernel, final_norm=final_norm),
        grid=(m // tm,),
        in_specs=[
            pl.BlockSpec((tm, D_MODEL), lambda i: (i, 0)),
            _const_spec((1, D_MODEL)),
            _layer_spec((D_MODEL, D_FF), layer),
            _layer_spec((D_FF, D_MODEL), layer),
            _const_spec((1, D_MODEL)),
        ],
        out_specs=pl.BlockSpec((tm, D_MODEL), lambda i: (i, 0)),
        out_shape=jax.ShapeDtypeStruct((m, D_MODEL), F32),
        compiler_params=_cparams("parallel"),
        name="mlp",
    )(x, g.reshape(1, D_MODEL), w_up, w_down, g_final.reshape(1, D_MODEL))


Q_COLS = N_GROUPS * ATTN_OUT
KV_COLS = 2 * N_GROUPS * ATTN_OUT


def _w_cols(which, g):
    start = (which * N_GROUPS + g) * ATTN_OUT
    return slice(start, start + ATTN_OUT)


def _qkv_kernel(x_ref, g_ref, w_ref, q_ref, kvb_ref, kvf_ref):
    h = _rms(x_ref[...], g_ref[...]).astype(BF16)
    q = jnp.dot(h, w_ref[:, :Q_COLS], preferred_element_type=F32)
    q_ref[...] = (q * (HEAD_DIM ** -0.5)).astype(BF16)
    for g in range(N_GROUPS):
        for which in (1, 2):
            out = slice((2 * g + which - 1) * ATTN_OUT, (2 * g + which) * ATTN_OUT)
            kv = jnp.dot(h, w_ref[:, _w_cols(which, g)], preferred_element_type=F32)
            kvf_ref[:, out] = kv
            kvb_ref[:, out] = kv.astype(BF16)


def _qkv(x, g, w):
    m = x.shape[0]
    tm = min(ROW_TILE, m)
    row = lambda i: (i, 0)
    return pl.pallas_call(
        _qkv_kernel,
        grid=(m // tm,),
        in_specs=[
            pl.BlockSpec((tm, D_MODEL), row),
            _const_spec((1, D_MODEL)),
            _const_spec((D_MODEL, Q_COLS + KV_COLS)),
        ],
        out_specs=[
            pl.BlockSpec((tm, Q_COLS), row),
            pl.BlockSpec((tm, KV_COLS), row),
            pl.BlockSpec((tm, KV_COLS), row),
        ],
        out_shape=[
            jax.ShapeDtypeStruct((m, Q_COLS), BF16),
            jax.ShapeDtypeStruct((m, KV_COLS), BF16),
            jax.ShapeDtypeStruct((m, KV_COLS), F32),
        ],
        compiler_params=_cparams("parallel"),
        name="qkv",
    )(x, g.reshape(1, D_MODEL), w)


def _qkv_prompt_kernel(x_ref, g_ref, w_ref, wt_ref, *rest, tm, n_tiles, wins):
    q_refs, kv_refs, kvt_refs, scr_ref = rest[0:3], rest[3:6], rest[6:9], rest[9]
    i = pl.program_id(1)
    hf = _rms(x_ref[0], g_ref[...])
    h = hf.astype(BF16)

    h_by_dil = {1: h}
    dils = sorted({dil for _, dil in ATTN_GROUPS if dil > 1})
    chunks = D_MODEL // LANES
    if dils:
        for c in range(chunks):
            scr_ref[c] = hf[:, c * LANES:(c + 1) * LANES]
    for dil in dils:
        classes = [jnp.concatenate([scr_ref[c, pl.ds(r, tm // dil, stride=dil), :] for c in range(chunks)], axis=1)
                   for r in range(dil)]
        h_by_dil[dil] = jnp.concatenate(classes, axis=0).astype(BF16)

    for g, (_, dil) in enumerate(ATTN_GROUPS):
        hg = h_by_dil[dil]
        q = jnp.dot(hg, w_ref[:, _w_cols(0, g)], preferred_element_type=F32) * (HEAD_DIM ** -0.5)
        q_refs[g][0] = q.reshape(dil, tm // dil, ATTN_OUT).astype(BF16)
        for which in (1, 2):
            kv = jnp.dot(hg, w_ref[:, _w_cols(which, g)], preferred_element_type=F32)
            kv_refs[g][0, :, :, (which - 1) * ATTN_OUT:which * ATTN_OUT] = (
                kv.reshape(dil, tm // dil, ATTN_OUT).astype(BF16))

    for g, win in enumerate(wins):
        rows = min(win, tm)
        first_tile = n_tiles - max(win // tm, 1)

        @pl.when(i >= first_tile)
        def _(g=g, rows=rows):
            for which in (1, 2):
                start = ((which - 1) * N_GROUPS + g) * ATTN_OUT
                kvt = lax.dot_general(wt_ref[start:start + ATTN_OUT, :], h[tm - rows:, :], _NT,
                                      preferred_element_type=F32)
                kvt_refs[g][0, which - 1] = kvt.reshape(HEADS, HEAD_DIM, rows)


def _qkv_prompt(x, g, w, w_t, batch, seq):
    tm = min(ROW_TILE, seq)
    n_tiles = seq // tm
    wins = tuple(min(win, seq) for win, _ in ATTN_GROUPS)
    assert all(w_ % tm == 0 or tm % w_ == 0 for w_ in wins)
    split = lambda width: [
        (pl.BlockSpec((1, dil, tm // dil, width), lambda b, i: (b, 0, i, 0)),
         jax.ShapeDtypeStruct((batch, dil, seq // dil, width), BF16)) for _, dil in ATTN_GROUPS]
    kvt = [(pl.BlockSpec((1, 2, HEADS, HEAD_DIM, min(win, tm)),
                         lambda b, i, first=n_tiles - max(win // tm, 1): (b, 0, 0, 0, jnp.maximum(i - first, 0))),
            jax.ShapeDtypeStruct((batch, 2, HEADS, HEAD_DIM, win), F32)) for win in wins]
    outs = split(ATTN_OUT) + split(2 * ATTN_OUT) + kvt
    res = pl.pallas_call(
        functools.partial(_qkv_prompt_kernel, tm=tm, n_tiles=n_tiles, wins=wins),
        grid=(batch, n_tiles),
        in_specs=[
            pl.BlockSpec((1, tm, D_MODEL), lambda b, i: (b, i, 0)),
            _const_spec((1, D_MODEL)),
            _const_spec((D_MODEL, Q_COLS + KV_COLS)),
            _const_spec((KV_COLS, D_MODEL)),
        ],
        out_specs=[o[0] for o in outs],
        out_shape=[o[1] for o in outs],
        scratch_shapes=[pltpu.VMEM((D_MODEL // LANES, tm, LANES), F32)],
        compiler_params=_cparams("parallel", "arbitrary"),
        name="qkv_prompt",
    )(x.reshape(batch, seq, D_MODEL), g.reshape(1, D_MODEL), w, w_t)
    return res[0:3], res[3:6], res[6:9]


def _attn_prompt_kernel(q_ref, kvp_ref, kvc_ref, bias_ref, cap_ref, o_ref, st_ref, s_scr, p_scr, *,
                        classes, blocks):
    n = pl.program_id(2)
    lane = lax.broadcasted_iota(jnp.int32, (BAND, LANES), 1)
    low_half = lane < HEAD_DIM
    half_sel = [low_half.astype(F32).astype(BF16), (~low_half).astype(F32).astype(BF16)]

    for cls, blk in [(c, b) for c in range(classes) for b in range(blocks)]:
        rows = slice(blk * BAND, (blk + 1) * BAND)
        if blk == 0:
            first = jnp.where(n > 0, 0, 1)
            prev = lambda cols, cls=cls: kvp_ref[0, cls, :, cols]
        else:
            first = 0
            prev = lambda cols, cls=cls, blk=blk: kvc_ref[0, cls, (blk - 1) * BAND:blk * BAND, cols]
        keys = lambda cols, prev=prev, cls=cls, rows=rows: jnp.concatenate(
            [prev(cols), kvc_ref[0, cls, rows, cols]], axis=0)

        for pair in range(HEADS // 2):
            cols = slice(pair * LANES, (pair + 1) * LANES)
            q2 = q_ref[0, cls, rows, cols]
            k2 = keys(cols)
            for half in range(2):
                s_scr[2 * pair + half] = lax.dot_general(q2 * half_sel[half], k2, _NT,
                                                         preferred_element_type=F32)

        stats = jnp.zeros((BAND, LANES), F32)
        for h in range(HEADS):
            t = jnp.minimum(s_scr[h] + bias_ref[h], cap_ref[first])
            mx = jnp.max(t, axis=-1, keepdims=True)
            e = jnp.exp(t - mx)
            den = jnp.sum(e, axis=-1, keepdims=True)
            p_scr[h] = e.astype(BF16)
            stats = jnp.where(lane == h, mx, jnp.where(lane == HEADS + h, den, stats))
        st_ref[0, cls, rows, :] = stats

        for pair in range(HEADS // 2):
            cols = slice(pair * LANES, (pair + 1) * LANES)
            v2 = keys(slice(ATTN_OUT + cols.start, ATTN_OUT + cols.stop))
            o_lo = jnp.dot(p_scr[2 * pair], v2, preferred_element_type=F32)
            o_hi = jnp.dot(p_scr[2 * pair + 1], v2, preferred_element_type=F32)
            o_ref[0, cls, rows, cols] = jnp.where(low_half, o_lo, o_hi)


BLOCKS_PER_STEP = 4


def _attn_prompt(q, kv, bias, cap):
    batch, dil, sub, _ = q.shape
    blocks = min(BLOCKS_PER_STEP, sub // BAND)
    classes = min(BLOCKS_PER_STEP // blocks, dil)
    step = blocks * BAND
    assert sub % step == 0 and dil % classes == 0
    cur = lambda b, r, n: (b, r, n, 0)
    return pl.pallas_call(
        functools.partial(_attn_prompt_kernel, classes=classes, blocks=blocks),
        grid=(batch, dil // classes, sub // step),
        in_specs=[
            pl.BlockSpec((1, classes, step, ATTN_OUT), cur),
            pl.BlockSpec((1, classes, BAND, 2 * ATTN_OUT),
                         lambda b, r, n: (b, r, jnp.maximum(n * blocks - 1, 0), 0)),
            pl.BlockSpec((1, classes, step, 2 * ATTN_OUT), cur),
            _const_spec((HEADS, BAND, 2 * BAND)),
            _const_spec((2, BAND, 2 * BAND)),
        ],
        out_specs=[
            pl.BlockSpec((1, classes, step, ATTN_OUT), cur),
            pl.BlockSpec((1, classes, step, LANES), cur),
        ],
        out_shape=[
            jax.ShapeDtypeStruct((batch, dil, sub, ATTN_OUT), F32),
            jax.ShapeDtypeStruct((batch, dil, sub, LANES), F32),
        ],
        scratch_shapes=[pltpu.VMEM((HEADS, BAND, 2 * BAND), F32), pltpu.VMEM((HEADS, BAND, 2 * BAND), BF16)],
        compiler_params=_cparams("parallel", "parallel", "arbitrary"),
        name="attn_prompt_d%d" % dil,
    )(q, kv, kv, bias, cap)


def _prompt_bias(rel_bias_g, dil):
    tab = rel_bias_g.astype(F32)[_t5_bucket(np.arange(BAND, -1, -1) * dil)]
    ext = jnp.pad(tab.T, ((0, 0), (BAND, BAND)))
    tiled = jnp.tile(ext, (1, BAND))[:, :BAND * 3 * BAND].reshape(HEADS, BAND, 3 * BAND)
    return tiled[:, :, BAND:]


def _prompt_cap():
    m = BAND + np.arange(BAND)[:, None] - np.arange(2 * BAND)[None, :]
    band = (m >= 0) & (m <= BAND)
    no_prev = band & (np.arange(2 * BAND) >= BAND)[None, :]
    big = np.finfo(np.float32).max
    return jnp.asarray(np.where(np.stack([band, no_prev]), big, NEG_INF).astype(np.float32))


def _wo_merge_kernel(o0_ref, o1_ref, o2_ref, l0_ref, l1_ref, l2_ref, e_ref, w_ref, x_ref, out_ref,
                     oscr_ref, lscr_ref, *, tm):
    def token_order(ref, scr_ref, dil):
        if dil == 1:
            return ref[0, 0]
        chunks = ref.shape[-1] // LANES
        for r in range(dil):
            for c in range(chunks):
                scr_ref[c, pl.ds(r, tm // dil, stride=dil), :] = ref[0, r, :, c * LANES:(c + 1) * LANES]
        return jnp.concatenate([scr_ref[c] for c in range(chunks)], axis=1)

    dils = [dil for _, dil in ATTN_GROUPS]
    sts = [token_order(l_ref, lscr_ref.at[g:g + 1], dils[g]) for g, l_ref in enumerate((l0_ref, l1_ref, l2_ref))]
    mxs = [st[:, :HEADS] for st in sts]
    dens = [st[:, HEADS:2 * HEADS] for st in sts]
    mx = jnp.maximum(jnp.maximum(mxs[0], mxs[1]), mxs[2])
    es = [jnp.exp(m - mx) for m in mxs]
    inv = 1.0 / (es[0] * dens[0] + es[1] * dens[1] + es[2] * dens[2])
    acc = None
    for g, o_ref in enumerate((o0_ref, o1_ref, o2_ref)):
        wt = es[g] * inv
        hi = wt.astype(BF16)
        lo = (wt - hi.astype(F32)).astype(BF16)
        wexp = (jnp.dot(hi, e_ref[...], preferred_element_type=F32)
                + jnp.dot(lo, e_ref[...], preferred_element_type=F32))
        term = wexp * token_order(o_ref, oscr_ref, dils[g])
        acc = term if acc is None else acc + term
    out_ref[0] = x_ref[0] + jnp.dot(acc.astype(BF16), w_ref[...], preferred_element_type=F32)


def _wo_merge(os, lses, w_o, x, batch, seq):
    tm = min(ROW_TILE, seq)
    expand = jnp.asarray(np.repeat(np.eye(HEADS, dtype=np.float32), HEAD_DIM, axis=1), BF16)
    split = lambda width: [pl.BlockSpec((1, dil, tm // dil, width), lambda b, i: (b, 0, i, 0))
                           for _, dil in ATTN_GROUPS]
    row = pl.BlockSpec((1, tm, D_MODEL), lambda b, i: (b, i, 0))
    out = pl.pallas_call(
        functools.partial(_wo_merge_kernel, tm=tm),
        grid=(batch, seq // tm),
        in_specs=split(ATTN_OUT) + split(LANES) + [
            _const_spec((HEADS, ATTN_OUT)),
            _const_spec((ATTN_OUT, D_MODEL)),
            row,
        ],
        out_specs=row,
        out_shape=jax.ShapeDtypeStruct((batch, seq, D_MODEL), F32),
        scratch_shapes=[pltpu.VMEM((ATTN_OUT // LANES, tm, LANES), F32), pltpu.VMEM((N_GROUPS, tm, LANES), F32)],
        compiler_params=_cparams("parallel", "parallel"),
        name="wo_merge",
    )(*os, *lses, expand, w_o, x.reshape(batch, seq, D_MODEL))
    return out.reshape(batch * seq, D_MODEL)


def _wo_kernel(o_ref, w_ref, x_ref, out_ref):
    out_ref[...] = x_ref[...] + jnp.dot(o_ref[...].astype(BF16), w_ref[...], preferred_element_type=F32)


def _wo(o, w_o, x):
    m = x.shape[0]
    tm = min(ROW_TILE, m)
    row = lambda i: (i, 0)
    return pl.pallas_call(
        _wo_kernel,
        grid=(m // tm,),
        in_specs=[pl.BlockSpec((tm, ATTN_OUT), row), _const_spec((ATTN_OUT, D_MODEL)),
                  pl.BlockSpec((tm, D_MODEL), row)],
        out_specs=pl.BlockSpec((tm, D_MODEL), row),
        out_shape=jax.ShapeDtypeStruct((m, D_MODEL), F32),
        compiler_params=_cparams("parallel"),
        name="wo",
    )(o, w_o, x)


Q_PAD = 8
NEW_PAD = 16


def _sample_tables(rel_bias, dec_seq, buf_rows):
    tq = np.arange(Q_PAD)
    tables = []
    for g, (_, dil) in enumerate(ATTN_GROUPS):
        wb = buf_rows[g]
        by_dist = rel_bias[:, g].astype(F32)[_t5_bucket(np.arange(wb + Q_PAD))]
        rev = by_dist[::-1].T
        for kpos, real in ((np.arange(wb), np.ones(wb, bool)),
                           (wb + np.arange(NEW_PAD), np.arange(NEW_PAD) < dec_seq)):
            delta = (wb + tq)[:, None] - kpos[None, :]
            ok = (delta >= 0) & (delta % dil == 0) & (delta // dil <= BAND)
            ok &= (tq < dec_seq)[:, None] & real[None, :]
            if kpos.shape[0] == wb:
                bias = jnp.stack([rev[:, Q_PAD - 1 - t:Q_PAD - 1 - t + wb] for t in range(Q_PAD)], axis=1)
            else:
                bias = jnp.transpose(by_dist[np.maximum(delta, 0)], (2, 0, 1))
            tables.append(bias)
            tables.append(jnp.asarray(ok.astype(np.float32)))
    return tables


def _attn_sample_kernel(q_ref, kvn_ref, c0_ref, c1_ref, c2_ref, *rest):
    tables, o_ref = rest[:-1], rest[-1]
    for h in range(HEADS):
        parts = []
        for g, c_ref in enumerate((c0_ref, c1_ref, c2_ref)):
            bias_ref, valid_ref, nbias_ref, nvalid_ref = tables[4 * g:4 * g + 4]
            q = q_ref[0, g, h]
            s = jnp.dot(q, c_ref[0, h].astype(BF16), preferred_element_type=F32)
            parts.append((jnp.where(valid_ref[...] > 0.0, s + bias_ref[h], NEG_INF), c_ref, None))
            s = lax.dot_general(q, kvn_ref[0, g, 0, h], _NT, preferred_element_type=F32)
            parts.append((jnp.where(nvalid_ref[...] > 0.0, s + nbias_ref[h], NEG_INF), None, g))
        mx = None
        for s, _, _ in parts:
            pm = jnp.max(s, axis=-1, keepdims=True)
            mx = pm if mx is None else jnp.maximum(mx, pm)
        den = jnp.zeros((Q_PAD, 1), F32)
        acc = jnp.zeros((Q_PAD, HEAD_DIM), F32)
        for s, c_ref, g in parts:
            e = jnp.exp(s - mx)
            den = den + jnp.sum(e, axis=-1, keepdims=True)
            e = e.astype(BF16)
            if c_ref is not None:
                acc = acc + lax.dot_general(e, c_ref[1, h].astype(BF16), _NT, preferred_element_type=F32)
            else:
                acc = acc + jnp.dot(e, kvn_ref[0, g, 1, h], preferred_element_type=F32)
        o_ref[0, h] = acc / den


def _attn_sample(q, kvb, caches, li, tables, n, dec_seq):
    q5 = q.reshape(n, dec_seq, N_GROUPS, HEADS, HEAD_DIM).transpose(0, 2, 3, 1, 4)
    q5 = jnp.pad(q5, ((0, 0), (0, 0), (0, 0), (0, Q_PAD - dec_seq), (0, 0)))
    kvn = kvb.reshape(n, dec_seq, N_GROUPS, 2, HEADS, HEAD_DIM).transpose(0, 2, 3, 4, 1, 5)
    kvn = jnp.pad(kvn, ((0, 0), (0, 0), (0, 0), (0, 0), (0, NEW_PAD - dec_seq), (0, 0)))
    cache_spec = lambda c: pl.BlockSpec((None, None) + c.shape[2:], lambda i: (li, i, 0, 0, 0, 0))
    o = pl.pallas_call(
        _attn_sample_kernel,
        grid=(n,),
        in_specs=[
            pl.BlockSpec((1, N_GROUPS, HEADS, Q_PAD, HEAD_DIM), lambda i: (i, 0, 0, 0, 0)),
            pl.BlockSpec((1, N_GROUPS, 2, HEADS, NEW_PAD, HEAD_DIM), lambda i: (i, 0, 0, 0, 0, 0)),
        ] + [cache_spec(c) for c in caches] + [_const_spec(t.shape) for t in tables],
        out_specs=pl.BlockSpec((1, HEADS, Q_PAD, HEAD_DIM), lambda i: (i, 0, 0, 0)),
        out_shape=jax.ShapeDtypeStruct((n, HEADS, Q_PAD, HEAD_DIM), F32),
        compiler_params=_cparams("parallel"),
        name="attn_sample",
    )(q5, kvn, *caches, *tables)
    return o[:, :, :dec_seq].transpose(0, 2, 1, 3).reshape(n * dec_seq, ATTN_OUT)


HALO = 16


def _pool_prompt_kernel(x_ref, g_ref, w_ref, sc_ref, o_ref, st_ref, hp_ref, *, tt):
    i = pl.program_id(1)

    @pl.when(i == 0)
    def _():
        hp_ref[0:HALO, :] = jnp.zeros((HALO, D_MODEL), F32)

    x = x_ref[0]
    h = _rms(x, g_ref[...])
    hp_ref[HALO:HALO + tt, :] = h
    pos1 = (i * tt + 1 + lax.broadcasted_iota(jnp.int32, (tt, 1), 0)).astype(F32)
    ys = []
    for g, w in enumerate(POOL_WINDOWS):
        cols = slice(g * POOL_CH, (g + 1) * POOL_CH)
        hg = h[:, cols]
        win = hg
        for j in range(1, w):
            win = win + hp_ref[HALO - j:HALO - j + tt, cols]
        p = win / jnp.minimum(float(w), pos1) - hg
        ys.append(jnp.dot(p.astype(BF16), w_ref[g], preferred_element_type=F32))
    o_ref[0] = x + jnp.concatenate(ys, axis=1) * sc_ref[...]
    tail = hp_ref[tt:tt + HALO, :]
    hp_ref[0:HALO, :] = tail

    @pl.when(i == pl.num_programs(1) - 1)
    def _():
        st_ref[0] = tail


def _pool_prompt(x, g, w_pool, scale, batch, seq):
    tt = min(ROW_TILE, seq)
    out, st = pl.pallas_call(
        functools.partial(_pool_prompt_kernel, tt=tt),
        grid=(batch, seq // tt),
        in_specs=[
            pl.BlockSpec((1, tt, D_MODEL), lambda b, i: (b, i, 0)),
            _const_spec((1, D_MODEL)),
            _const_spec(w_pool.shape),
            _const_spec((1, D_MODEL)),
        ],
        out_specs=[
            pl.BlockSpec((1, tt, D_MODEL), lambda b, i: (b, i, 0)),
            pl.BlockSpec((1, HALO, D_MODEL), lambda b, i: (b, 0, 0)),
        ],
        out_shape=[
            jax.ShapeDtypeStruct((batch, seq, D_MODEL), F32),
            jax.ShapeDtypeStruct((batch, HALO, D_MODEL), F32),
        ],
        scratch_shapes=[pltpu.VMEM((HALO + tt, D_MODEL), F32)],
        compiler_params=_cparams("parallel", "arbitrary"),
        name="pool_prompt",
    )(x.reshape(batch, seq, D_MODEL), g.reshape(1, D_MODEL), w_pool, scale.reshape(1, D_MODEL))
    return out.reshape(batch * seq, D_MODEL), st[:, HALO - POOL_STATE:, :]


def _pool_sample_kernel(x_ref, st_ref, g_ref, w_ref, sc_ref, o_ref, nst_ref, *, dec_seq, past_len):
    chunk = lambda ref, k: ref[:, k * D_MODEL:(k + 1) * D_MODEL]
    xs = [chunk(x_ref, t) for t in range(dec_seq)]
    hs = [_rms(x, g_ref[...]) for x in xs]
    rows = [st_ref[k] for k in range(POOL_STATE)] + hs
    for t in range(dec_seq):
        ys = []
        for g, w in enumerate(POOL_WINDOWS):
            cols = slice(g * POOL_CH, (g + 1) * POOL_CH)
            win = rows[POOL_STATE + t][:, cols]
            for j in range(1, w):
                win = win + rows[POOL_STATE + t - j][:, cols]
            p = win / float(min(w, past_len + t + 1)) - hs[t][:, cols]
            ys.append(jnp.dot(p.astype(BF16), w_ref[g], preferred_element_type=F32))
        o_ref[:, t * D_MODEL:(t + 1) * D_MODEL] = xs[t] + jnp.concatenate(ys, axis=1) * sc_ref[...]
    new_rows = rows[-POOL_STATE:]
    for k in range(POOL_STATE):
        nst_ref[k] = new_rows[k]


def _pool_sample(x, state_t, li, g, w_pool, scale, n, dec_seq, past_len):
    bn = min(32, n)
    out, nst = pl.pallas_call(
        functools.partial(_pool_sample_kernel, dec_seq=dec_seq, past_len=past_len),
        grid=(n // bn,),
        in_specs=[
            pl.BlockSpec((bn, dec_seq * D_MODEL), lambda i: (i, 0)),
            pl.BlockSpec((None, POOL_STATE, bn, D_MODEL), lambda i: (li, 0, i, 0)),
            _const_spec((1, D_MODEL)),
            _const_spec(w_pool.shape),
            _const_spec((1, D_MODEL)),
        ],
        out_specs=[
            pl.BlockSpec((bn, dec_seq * D_MODEL), lambda i: (i, 0)),
            pl.BlockSpec((POOL_STATE, bn, D_MODEL), lambda i: (0, i, 0)),
        ],
        out_shape=[
            jax.ShapeDtypeStruct((n, dec_seq * D_MODEL), F32),
            jax.ShapeDtypeStruct((POOL_STATE, n, D_MODEL), F32),
        ],
        compiler_params=_cparams("parallel"),
        name="pool_sample",
    )(x.reshape(n, dec_seq * D_MODEL), state_t, g.reshape(1, D_MODEL), w_pool, scale.reshape(1, D_MODEL))
    return out.reshape(n * dec_seq, D_MODEL), nst


def kernel(x_prompt, x_sample, state_pool, cache_kv_w128, cache_kv_w512, cache_kv_w2048, rel_bias, norm_mix,
           norm_ffn, norm_final, w_pool, pool_scale, w_qkv, w_o, w_up, w_down):
    batch, seq, _ = x_prompt.shape
    n, dec_seq, _ = x_sample.shape
    depth = norm_mix.shape[0]
    caches = tuple(jnp.transpose(c, (0, 1, 3, 4, 5, 2)) for c in (cache_kv_w128, cache_kv_w512, cache_kv_w2048))
    state_t = jnp.transpose(state_pool, (0, 2, 1, 3))
    past_len = PAST_LEN

    xp = x_prompt.reshape(batch * seq, D_MODEL)
    xs = x_sample.reshape(n * dec_seq, D_MODEL)
    w_up_b = w_up.astype(BF16)
    w_down_b = w_down.astype(BF16)
    w_pool_b = w_pool.astype(BF16)
    w_o_b = w_o.astype(BF16)
    w_qkv_b = w_qkv.astype(BF16)
    w_kv_t = jnp.transpose(w_qkv_b[:, :, Q_COLS:], (0, 2, 1))
    prompt_bias = [_prompt_bias(rel_bias[:, g], dil) for g, (_, dil) in enumerate(ATTN_GROUPS)]
    prompt_cap = _prompt_cap()
    sample_tables = _sample_tables(rel_bias, dec_seq, tuple(c.shape[-1] for c in caches))

    kv_rows_major = lambda per_layer: jnp.transpose(jnp.stack(per_layer), (0, 1, 5, 2, 3, 4))

    pool_p, pool_s = [], []
    kv_p = [[] for _ in ATTN_GROUPS]
    kv_s = [[] for _ in ATTN_GROUPS]
    for i in range(depth):
        li = i // 2
        if i % 2 == 0:
            xp, sp = _pool_prompt(xp, norm_mix[i], w_pool_b[li], pool_scale[li], batch, seq)
            xs, ss = _pool_sample(xs, state_t, li, norm_mix[i], w_pool_b[li], pool_scale[li], n, dec_seq,
                                  past_len)
            pool_p.append(sp)
            pool_s.append(ss)
        else:
            qps, kvps, kvts = _qkv_prompt(xp, norm_mix[i], w_qkv_b[li], w_kv_t[li], batch, seq)
            qs, kvbs, kvfs = _qkv(xs, norm_mix[i], w_qkv_b[li])
            os, lses = zip(*[_attn_prompt(qps[g], kvps[g], prompt_bias[g], prompt_cap) for g in range(N_GROUPS)])
            xp = _wo_merge(os, lses, w_o_b[li], xp, batch, seq)
            o_s = _attn_sample(qs, kvbs, caches, li, sample_tables, n, dec_seq)
            xs = _wo(o_s, w_o_b[li], xs)
            kvfs = kvfs.reshape(n, dec_seq, N_GROUPS, 2, HEADS, HEAD_DIM)
            for g in range(N_GROUPS):
                kv_p[g].append(kvts[g])
                kv_s[g].append(kvfs[:, :, g])
        last = i == depth - 1
        xp = _mlp(xp, norm_ffn[i], w_up_b, w_down_b, i, norm_final, last)
        xs = _mlp(xs, norm_ffn[i], w_up_b, w_down_b, i, norm_final, last)
    return (xp.reshape(batch, seq, D_MODEL), xs.reshape(n, dec_seq, D_MODEL),
            jnp.stack(pool_p), jnp.transpose(jnp.stack(pool_s), (0, 2, 1, 3)),
            kv_rows_major(kv_p[0]), jnp.stack(kv_s[0]),
            kv_rows_major(kv_p[1]), jnp.stack(kv_s[1]),
            kv_rows_major(kv_p[2]), jnp.stack(kv_s[2]))
```

```python
import functools

import numpy as np
import jax
import jax.numpy as jnp
from jax import lax
from jax.experimental import pallas as pl
from jax.experimental.pallas import tpu as pltpu

F32 = jnp.float32
BF16 = jnp.bfloat16

D_MODEL = 1024
D_FF = 4 * D_MODEL
POOL_WINDOWS = (2, 4, 8, 16)
POOL_CH = D_MODEL // len(POOL_WINDOWS)
POOL_STATE = max(POOL_WINDOWS) - 1
ATTN_GROUPS = ((128, 1), (512, 4), (2048, 16))
N_GROUPS = len(ATTN_GROUPS)
HEADS = 8
HEAD_DIM = 64
ATTN_OUT = HEADS * HEAD_DIM
BAND = 128
N_BUCKETS = 32
MAX_EXACT = N_BUCKETS // 2
REL_MAX_DIST = 2048
PAST_LEN = 2048
RMS_EPS = 1e-6
NEG_INF = -1e30

VMEM_LIMIT_BYTES = 56 * 1024 * 1024
FUSED_VMEM_LIMIT_BYTES = 62 * 1024 * 1024
ROW_TILE = 512
FF_CHUNK = 1024
FF_CHUNKS = D_FF // FF_CHUNK
_NT = (((1,), (1,)), ((), ()))
LANES = 128


def _cparams(*sem):
    return pltpu.CompilerParams(dimension_semantics=sem, vmem_limit_bytes=VMEM_LIMIT_BYTES)


def _rms(x, g):
    ms = jnp.mean(x * x, axis=-1, keepdims=True)
    return x * lax.rsqrt(ms + RMS_EPS) * g


def _t5_bucket(dist):
    n = np.maximum(np.asarray(dist), 0)
    large = MAX_EXACT + (np.log(np.maximum(n, 1) / MAX_EXACT) / np.log(REL_MAX_DIST / MAX_EXACT)
                         * (N_BUCKETS - MAX_EXACT)).astype(np.int64)
    large = np.minimum(large, N_BUCKETS - 1)
    return np.where(n < MAX_EXACT, n, large).astype(np.int32)


def _const_spec(shape):
    nd = len(shape)
    return pl.BlockSpec(shape, lambda *_: (0,) * nd, pipeline_mode=pl.Buffered(1))


def _layer_spec(shape, layer):
    nd = len(shape)
    return pl.BlockSpec((None,) + tuple(shape), lambda *_: (layer,) + (0,) * nd, pipeline_mode=pl.Buffered(1))


def _mlp_kernel(x_ref, g_ref, wu_ref, wd_ref, gf_ref, o_ref, *, final_norm):
    x = x_ref[...]
    h = _rms(x, g_ref[...]).astype(BF16)
    acc = x
    for c in range(FF_CHUNKS):
        u = jnp.dot(h, wu_ref[c], preferred_element_type=F32)
        a = jnp.square(jnp.maximum(u, 0.0)).astype(BF16)
        acc = acc + jnp.dot(a, wd_ref[c], preferred_element_type=F32)
    if final_norm:
        acc = _rms(acc, gf_ref[...])
    o_ref[...] = acc


def _mlp(x, g, w_up, w_down, layer, g_final, final_norm):
    m = x.shape[0]
    tm = min(ROW_TILE, m)
    return pl.pallas_call(
        functools.partial(_mlp_kernel, final_norm=final_norm),
        grid=(m // tm,),
        in_specs=[
            pl.BlockSpec((tm, D_MODEL), lambda i: (i, 0)),
            _const_spec((1, D_MODEL)),
            _layer_spec((FF_CHUNKS, D_MODEL, FF_CHUNK), layer),
            _layer_spec((FF_CHUNKS, FF_CHUNK, D_MODEL), layer),
            _const_spec((1, D_MODEL)),
        ],
        out_specs=pl.BlockSpec((tm, D_MODEL), lambda i: (i, 0)),
        out_shape=jax.ShapeDtypeStruct((m, D_MODEL), F32),
        compiler_params=_cparams("parallel"),
        name="mlp",
    )(x, g.reshape(1, D_MODEL), w_up, w_down, g_final.reshape(1, D_MODEL))


Q_COLS = N_GROUPS * ATTN_OUT
KV_COLS = 2 * N_GROUPS * ATTN_OUT


def _w_cols(which, g):
    start = (which * N_GROUPS + g) * ATTN_OUT
    return slice(start, start + ATTN_OUT)


def _qkv_kernel(x_ref, g_ref, w_ref, q_ref, kvb_ref, kvf_ref):
    h = _rms(x_ref[...], g_ref[...]).astype(BF16)
    q = jnp.dot(h, w_ref[:, :Q_COLS], preferred_element_type=F32)
    q_ref[...] = (q * (HEAD_DIM ** -0.5)).astype(BF16)
    for g in range(N_GROUPS):
        for which in (1, 2):
            out = slice((2 * g + which - 1) * ATTN_OUT, (2 * g + which) * ATTN_OUT)
            kv = jnp.dot(h, w_ref[:, _w_cols(which, g)], preferred_element_type=F32)
            kvf_ref[:, out] = kv
            kvb_ref[:, out] = kv.astype(BF16)


def _qkv(x, g, w):
    m = x.shape[0]
    tm = min(ROW_TILE, m)
    row = lambda i: (i, 0)
    return pl.pallas_call(
        _qkv_kernel,
        grid=(m // tm,),
        in_specs=[
            pl.BlockSpec((tm, D_MODEL), row),
            _const_spec((1, D_MODEL)),
            _const_spec((D_MODEL, Q_COLS + KV_COLS)),
        ],
        out_specs=[
            pl.BlockSpec((tm, Q_COLS), row),
            pl.BlockSpec((tm, KV_COLS), row),
            pl.BlockSpec((tm, KV_COLS), row),
        ],
        out_shape=[
            jax.ShapeDtypeStruct((m, Q_COLS), BF16),
            jax.ShapeDtypeStruct((m, KV_COLS), BF16),
            jax.ShapeDtypeStruct((m, KV_COLS), F32),
        ],
        compiler_params=_cparams("parallel"),
        name="qkv",
    )(x, g.reshape(1, D_MODEL), w)


def _qkv_prompt_kernel(x_ref, g_ref, w_ref, wt_ref, *rest, tm, n_tiles, wins):
    q_refs, kv_refs, kvt_refs, scr_ref = rest[0:3], rest[3:6], rest[6:9], rest[9]
    i = pl.program_id(1)
    hf = _rms(x_ref[0], g_ref[...])
    h = hf.astype(BF16)

    h_by_dil = {1: h}
    dils = sorted({dil for _, dil in ATTN_GROUPS if dil > 1})
    chunks = D_MODEL // LANES
    if dils:
        for c in range(chunks):
            scr_ref[c] = hf[:, c * LANES:(c + 1) * LANES]
    for dil in dils:
        classes = [jnp.concatenate([scr_ref[c, pl.ds(r, tm // dil, stride=dil), :] for c in range(chunks)], axis=1)
                   for r in range(dil)]
        h_by_dil[dil] = jnp.concatenate(classes, axis=0).astype(BF16)

    for g, (_, dil) in enumerate(ATTN_GROUPS):
        hg = h_by_dil[dil]
        q = jnp.dot(hg, w_ref[:, _w_cols(0, g)], preferred_element_type=F32) * (HEAD_DIM ** -0.5)
        q_refs[g][0] = q.reshape(dil, tm // dil, ATTN_OUT).astype(BF16)
        for which in (1, 2):
            kv = jnp.dot(hg, w_ref[:, _w_cols(which, g)], preferred_element_type=F32)
            kv_refs[g][0, :, :, (which - 1) * ATTN_OUT:which * ATTN_OUT] = (
                kv.reshape(dil, tm // dil, ATTN_OUT).astype(BF16))

    for g, win in enumerate(wins):
        rows = min(win, tm)
        first_tile = n_tiles - max(win // tm, 1)

        @pl.when(i >= first_tile)
        def _(g=g, rows=rows):
            for which in (1, 2):
                start = ((which - 1) * N_GROUPS + g) * ATTN_OUT
                kvt = lax.dot_general(wt_ref[start:start + ATTN_OUT, :], h[tm - rows:, :], _NT,
                                      preferred_element_type=F32)
                kvt_refs[g][0, which - 1] = kvt.reshape(HEADS, HEAD_DIM, rows)


def _qkv_prompt(x, g, w, w_t, batch, seq):
    tm = min(ROW_TILE, seq)
    n_tiles = seq // tm
    wins = tuple(min(win, seq) for win, _ in ATTN_GROUPS)
    assert all(w_ % tm == 0 or tm % w_ == 0 for w_ in wins)
    split = lambda width: [
        (pl.BlockSpec((1, dil, tm // dil, width), lambda b, i: (b, 0, i, 0)),
         jax.ShapeDtypeStruct((batch, dil, seq // dil, width), BF16)) for _, dil in ATTN_GROUPS]
    kvt = [(pl.BlockSpec((1, 2, HEADS, HEAD_DIM, min(win, tm)),
                         lambda b, i, first=n_tiles - max(win // tm, 1): (b, 0, 0, 0, jnp.maximum(i - first, 0))),
            jax.ShapeDtypeStruct((batch, 2, HEADS, HEAD_DIM, win), F32)) for win in wins]
    outs = split(ATTN_OUT) + split(2 * ATTN_OUT) + kvt
    res = pl.pallas_call(
        functools.partial(_qkv_prompt_kernel, tm=tm, n_tiles=n_tiles, wins=wins),
        grid=(batch, n_tiles),
        in_specs=[
            pl.BlockSpec((1, tm, D_MODEL), lambda b, i: (b, i, 0)),
            _const_spec((1, D_MODEL)),
            _const_spec((D_MODEL, Q_COLS + KV_COLS)),
            _const_spec((KV_COLS, D_MODEL)),
        ],
        out_specs=[o[0] for o in outs],
        out_shape=[o[1] for o in outs],
        scratch_shapes=[pltpu.VMEM((D_MODEL // LANES, tm, LANES), F32)],
        compiler_params=_cparams("parallel", "arbitrary"),
        name="qkv_prompt",
    )(x.reshape(batch, seq, D_MODEL), g.reshape(1, D_MODEL), w, w_t)
    return res[0:3], res[3:6], res[6:9]


def _attn_prompt_kernel(q_ref, kvp_ref, kvc_ref, bias_ref, cap_ref, o_ref, st_ref, s_scr, p_scr, *,
                        classes, blocks):
    n = pl.program_id(2)
    lane = lax.broadcasted_iota(jnp.int32, (BAND, LANES), 1)
    low_half = lane < HEAD_DIM
    half_sel = [low_half.astype(F32).astype(BF16), (~low_half).astype(F32).astype(BF16)]

    for cls, blk in [(c, b) for c in range(classes) for b in range(blocks)]:
        rows = slice(blk * BAND, (blk + 1) * BAND)
        if blk == 0:
            first = jnp.where(n > 0, 0, 1)
            prev = lambda cols, cls=cls: kvp_ref[0, cls, :, cols]
        else:
            first = 0
            prev = lambda cols, cls=cls, blk=blk: kvc_ref[0, cls, (blk - 1) * BAND:blk * BAND, cols]
        keys = lambda cols, prev=prev, cls=cls, rows=rows: jnp.concatenate(
            [prev(cols), kvc_ref[0, cls, rows, cols]], axis=0)

        for pair in range(HEADS // 2):
            cols = slice(pair * LANES, (pair + 1) * LANES)
            q2 = q_ref[0, cls, rows, cols]
            k2 = keys(cols)
            for half in range(2):
                s_scr[2 * pair + half] = lax.dot_general(q2 * half_sel[half], k2, _NT,
                                                         preferred_element_type=F32)

        stats = jnp.zeros((BAND, LANES), F32)
        for h in range(HEADS):
            t = jnp.minimum(s_scr[h] + bias_ref[h], cap_ref[first])
            mx = jnp.max(t, axis=-1, keepdims=True)
            e = jnp.exp(t - mx)
            den = jnp.sum(e, axis=-1, keepdims=True)
            p_scr[h] = e.astype(BF16)
            stats = jnp.where(lane == h, mx, jnp.where(lane == HEADS + h, den, stats))
        st_ref[0, cls, rows, :] = stats

        for pair in range(HEADS // 2):
            cols = slice(pair * LANES, (pair + 1) * LANES)
            v2 = keys(slice(ATTN_OUT + cols.start, ATTN_OUT + cols.stop))
            o_lo = jnp.dot(p_scr[2 * pair], v2, preferred_element_type=F32)
            o_hi = jnp.dot(p_scr[2 * pair + 1], v2, preferred_element_type=F32)
            o_ref[0, cls, rows, cols] = jnp.where(low_half, o_lo, o_hi)


BLOCKS_PER_STEP = 4


def _attn_prompt(q, kv, bias, cap):
    batch, dil, sub, _ = q.shape
    blocks = min(BLOCKS_PER_STEP, sub // BAND)
    classes = min(BLOCKS_PER_STEP // blocks, dil)
    step = blocks * BAND
    assert sub % step == 0 and dil % classes == 0
    cur = lambda b, r, n: (b, r, n, 0)
    return pl.pallas_call(
        functools.partial(_attn_prompt_kernel, classes=classes, blocks=blocks),
        grid=(batch, dil // classes, sub // step),
        in_specs=[
            pl.BlockSpec((1, classes, step, ATTN_OUT), cur),
            pl.BlockSpec((1, classes, BAND, 2 * ATTN_OUT),
                         lambda b, r, n: (b, r, jnp.maximum(n * blocks - 1, 0), 0)),
            pl.BlockSpec((1, classes, step, 2 * ATTN_OUT), cur),
            _const_spec((HEADS, BAND, 2 * BAND)),
            _const_spec((2, BAND, 2 * BAND)),
        ],
        out_specs=[
            pl.BlockSpec((1, classes, step, ATTN_OUT), cur),
            pl.BlockSpec((1, classes, step, LANES), cur),
        ],
        out_shape=[
            jax.ShapeDtypeStruct((batch, dil, sub, ATTN_OUT), F32),
            jax.ShapeDtypeStruct((batch, dil, sub, LANES), F32),
        ],
        scratch_shapes=[pltpu.VMEM((HEADS, BAND, 2 * BAND), F32), pltpu.VMEM((HEADS, BAND, 2 * BAND), BF16)],
        compiler_params=_cparams("parallel", "parallel", "arbitrary"),
        name="attn_prompt_d%d" % dil,
    )(q, kv, kv, bias, cap)


def _prompt_bias(rel_bias_g, dil):
    tab = rel_bias_g.astype(F32)[_t5_bucket(np.arange(BAND, -1, -1) * dil)]
    ext = jnp.pad(tab.T, ((0, 0), (BAND, BAND)))
    tiled = jnp.tile(ext, (1, BAND))[:, :BAND * 3 * BAND].reshape(HEADS, BAND, 3 * BAND)
    return tiled[:, :, BAND:]


def _prompt_cap():
    m = BAND + np.arange(BAND)[:, None] - np.arange(2 * BAND)[None, :]
    band = (m >= 0) & (m <= BAND)
    no_prev = band & (np.arange(2 * BAND) >= BAND)[None, :]
    big = np.finfo(np.float32).max
    return jnp.asarray(np.where(np.stack([band, no_prev]), big, NEG_INF).astype(np.float32))


def _wo_merge_kernel(o0_ref, o1_ref, o2_ref, l0_ref, l1_ref, l2_ref, e_ref, w_ref, x_ref, out_ref,
                     oscr_ref, lscr_ref, *, tm):
    def token_order(ref, scr_ref, dil):
        if dil == 1:
            return ref[0, 0]
        chunks = ref.shape[-1] // LANES
        for r in range(dil):
            for c in range(chunks):
                scr_ref[c, pl.ds(r, tm // dil, stride=dil), :] = ref[0, r, :, c * LANES:(c + 1) * LANES]
        return jnp.concatenate([scr_ref[c] for c in range(chunks)], axis=1)

    dils = [dil for _, dil in ATTN_GROUPS]
    sts = [token_order(l_ref, lscr_ref.at[g:g + 1], dils[g]) for g, l_ref in enumerate((l0_ref, l1_ref, l2_ref))]
    mxs = [st[:, :HEADS] for st in sts]
    dens = [st[:, HEADS:2 * HEADS] for st in sts]
    mx = jnp.maximum(jnp.maximum(mxs[0], mxs[1]), mxs[2])
    es = [jnp.exp(m - mx) for m in mxs]
    inv = 1.0 / (es[0] * dens[0] + es[1] * dens[1] + es[2] * dens[2])
    acc = None
    for g, o_ref in enumerate((o0_ref, o1_ref, o2_ref)):
        wt = es[g] * inv
        hi = wt.astype(BF16)
        lo = (wt - hi.astype(F32)).astype(BF16)
        wexp = (jnp.dot(hi, e_ref[...], preferred_element_type=F32)
                + jnp.dot(lo, e_ref[...], preferred_element_type=F32))
        term = wexp * token_order(o_ref, oscr_ref, dils[g])
        acc = term if acc is None else acc + term
    out_ref[0] = x_ref[0] + jnp.dot(acc.astype(BF16), w_ref[...], preferred_element_type=F32)


def _wo_merge(os, lses, w_o, x, batch, seq):
    tm = min(ROW_TILE, seq)
    expand = jnp.asarray(np.repeat(np.eye(HEADS, dtype=np.float32), HEAD_DIM, axis=1), BF16)
    split = lambda width: [pl.BlockSpec((1, dil, tm // dil, width), lambda b, i: (b, 0, i, 0))
                           for _, dil in ATTN_GROUPS]
    row = pl.BlockSpec((1, tm, D_MODEL), lambda b, i: (b, i, 0))
    out = pl.pallas_call(
        functools.partial(_wo_merge_kernel, tm=tm),
        grid=(batch, seq // tm),
        in_specs=split(ATTN_OUT) + split(LANES) + [
            _const_spec((HEADS, ATTN_OUT)),
            _const_spec((ATTN_OUT, D_MODEL)),
            row,
        ],
        out_specs=row,
        out_shape=jax.ShapeDtypeStruct((batch, seq, D_MODEL), F32),
        scratch_shapes=[pltpu.VMEM((ATTN_OUT // LANES, tm, LANES), F32), pltpu.VMEM((N_GROUPS, tm, LANES), F32)],
        compiler_params=_cparams("parallel", "parallel"),
        name="wo_merge",
    )(*os, *lses, expand, w_o, x.reshape(batch, seq, D_MODEL))
    return out.reshape(batch * seq, D_MODEL)


def _wo_kernel(o_ref, w_ref, x_ref, out_ref):
    out_ref[...] = x_ref[...] + jnp.dot(o_ref[...].astype(BF16), w_ref[...], preferred_element_type=F32)


def _wo(o, w_o, x):
    m = x.shape[0]
    tm = min(ROW_TILE, m)
    row = lambda i: (i, 0)
    return pl.pallas_call(
        _wo_kernel,
        grid=(m // tm,),
        in_specs=[pl.BlockSpec((tm, ATTN_OUT), row), _const_spec((ATTN_OUT, D_MODEL)),
                  pl.BlockSpec((tm, D_MODEL), row)],
        out_specs=pl.BlockSpec((tm, D_MODEL), row),
        out_shape=jax.ShapeDtypeStruct((m, D_MODEL), F32),
        compiler_params=_cparams("parallel"),
        name="wo",
    )(o, w_o, x)


Q_PAD = 8
NEW_PAD = 16


def _sample_tables(rel_bias, dec_seq, buf_rows):
    tq = np.arange(Q_PAD)
    tables = []
    for g, (_, dil) in enumerate(ATTN_GROUPS):
        wb = buf_rows[g]
        by_dist = rel_bias[:, g].astype(F32)[_t5_bucket(np.arange(wb + Q_PAD))]
        rev = by_dist[::-1].T
        for kpos, real in ((np.arange(wb), np.ones(wb, bool)),
                           (wb + np.arange(NEW_PAD), np.arange(NEW_PAD) < dec_seq)):
            delta = (wb + tq)[:, None] - kpos[None, :]
            ok = (delta >= 0) & (delta % dil == 0) & (delta // dil <= BAND)
            ok &= (tq < dec_seq)[:, None] & real[None, :]
            if kpos.shape[0] == wb:
                bias = jnp.stack([rev[:, Q_PAD - 1 - t:Q_PAD - 1 - t + wb] for t in range(Q_PAD)], axis=1)
            else:
                bias = jnp.transpose(by_dist[np.maximum(delta, 0)], (2, 0, 1))
            tables.append(bias)
            tables.append(jnp.asarray(ok.astype(np.float32)))
    return tables


def _attn_sample_kernel(q_ref, kvn_ref, c0_ref, c1_ref, c2_ref, *rest):
    _attn_sample_body(q_ref, kvn_ref, (c0_ref, c1_ref, c2_ref), rest[:-1], rest[-1])


def _attn_sample_body(q_ref, kvn_ref, c_refs, tables, o_ref, heads=range(HEADS)):
    c0_ref, c1_ref, c2_ref = c_refs
    for h in heads:
        parts = []
        for g, c_ref in enumerate((c0_ref, c1_ref, c2_ref)):
            bias_ref, valid_ref, nbias_ref, nvalid_ref = tables[4 * g:4 * g + 4]
            q = q_ref[0, g, h]
            s = jnp.dot(q, c_ref[0, h].astype(BF16), preferred_element_type=F32)
            parts.append((jnp.where(valid_ref[...] > 0.0, s + bias_ref[h], NEG_INF), c_ref, None))
            s = lax.dot_general(q, kvn_ref[0, g, 0, h], _NT, preferred_element_type=F32)
            parts.append((jnp.where(nvalid_ref[...] > 0.0, s + nbias_ref[h], NEG_INF), None, g))
        mx = None
        for s, _, _ in parts:
            pm = jnp.max(s, axis=-1, keepdims=True)
            mx = pm if mx is None else jnp.maximum(mx, pm)
        den = jnp.zeros((Q_PAD, 1), F32)
        acc = jnp.zeros((Q_PAD, HEAD_DIM), F32)
        for s, c_ref, g in parts:
            e = jnp.exp(s - mx)
            den = den + jnp.sum(e, axis=-1, keepdims=True)
            e = e.astype(BF16)
            if c_ref is not None:
                acc = acc + lax.dot_general(e, c_ref[1, h].astype(BF16), _NT, preferred_element_type=F32)
            else:
                acc = acc + jnp.dot(e, kvn_ref[0, g, 1, h], preferred_element_type=F32)
        o_ref[0, h] = acc / den


def _mlp_attn_kernel(x_ref, g_ref, wu_ref, wd_ref, q_ref, kvn_ref, c0_ref, c1_ref, c2_ref, *rest):
    tables, out_ref, o_ref, acc_ref = rest[:-3], rest[-3], rest[-2], rest[-1]
    j = pl.program_id(1)

    @pl.when(j == 0)
    def _():
        acc_ref[...] = x_ref[...]

    h = _rms(x_ref[...], g_ref[...]).astype(BF16)
    u = jnp.dot(h, wu_ref[j], preferred_element_type=F32)
    a = jnp.square(jnp.maximum(u, 0.0)).astype(BF16)
    acc = acc_ref[...] + jnp.dot(a, wd_ref[j], preferred_element_type=F32)
    acc_ref[...] = acc
    out_ref[...] = acc
    _attn_sample_body(q_ref, kvn_ref, (c0_ref, c1_ref, c2_ref), tables, o_ref)


def _sample_qkv_by_head(q, kvb, n, dec_seq):
    q5 = q.reshape(n, dec_seq, N_GROUPS, HEADS, HEAD_DIM).transpose(0, 2, 3, 1, 4)
    q5 = jnp.pad(q5, ((0, 0), (0, 0), (0, 0), (0, Q_PAD - dec_seq), (0, 0)))
    kvn = kvb.reshape(n, dec_seq, N_GROUPS, 2, HEADS, HEAD_DIM).transpose(0, 2, 3, 4, 1, 5)
    kvn = jnp.pad(kvn, ((0, 0), (0, 0), (0, 0), (0, 0), (0, NEW_PAD - dec_seq), (0, 0)))
    return q5, kvn


def _attn_sample(q, kvb, caches, li, tables, n, dec_seq):
    q5, kvn = _sample_qkv_by_head(q, kvb, n, dec_seq)
    cache_spec = lambda c: pl.BlockSpec((None, None) + c.shape[2:], lambda i: (li, i, 0, 0, 0, 0))
    o = pl.pallas_call(
        _attn_sample_kernel,
        grid=(n,),
        in_specs=[
            pl.BlockSpec((1, N_GROUPS, HEADS, Q_PAD, HEAD_DIM), lambda i: (i, 0, 0, 0, 0)),
            pl.BlockSpec((1, N_GROUPS, 2, HEADS, NEW_PAD, HEAD_DIM), lambda i: (i, 0, 0, 0, 0, 0)),
        ] + [cache_spec(c) for c in caches] + [_const_spec(t.shape) for t in tables],
        out_specs=pl.BlockSpec((1, HEADS, Q_PAD, HEAD_DIM), lambda i: (i, 0, 0, 0)),
        out_shape=jax.ShapeDtypeStruct((n, HEADS, Q_PAD, HEAD_DIM), F32),
        compiler_params=_cparams("parallel"),
        name="attn_sample",
    )(q5, kvn, *caches, *tables)
    return o[:, :, :dec_seq].transpose(0, 2, 1, 3).reshape(n * dec_seq, ATTN_OUT)


def _mlp_with_sample_attn(x, g, w_up, w_down, layer, q, kvb, caches, li, tables, n, dec_seq):
    m = x.shape[0]
    tm = min(ROW_TILE, m)
    assert n % (m // tm) == 0 and n // (m // tm) == FF_CHUNKS
    q5, kvn = _sample_qkv_by_head(q, kvb, n, dec_seq)
    seq = lambda i, j: i * FF_CHUNKS + j
    cache_spec = lambda c: pl.BlockSpec((None, None) + c.shape[2:], lambda i, j: (li, seq(i, j), 0, 0, 0, 0))
    row = pl.BlockSpec((tm, D_MODEL), lambda i, j: (i, 0))
    out, o = pl.pallas_call(
        _mlp_attn_kernel,
        grid=(m // tm, FF_CHUNKS),
        in_specs=[
            row,
            _const_spec((1, D_MODEL)),
            _layer_spec((FF_CHUNKS, D_MODEL, FF_CHUNK), layer),
            _layer_spec((FF_CHUNKS, FF_CHUNK, D_MODEL), layer),
            pl.BlockSpec((1, N_GROUPS, HEADS, Q_PAD, HEAD_DIM), lambda i, j: (seq(i, j), 0, 0, 0, 0)),
            pl.BlockSpec((1, N_GROUPS, 2, HEADS, NEW_PAD, HEAD_DIM), lambda i, j: (seq(i, j), 0, 0, 0, 0, 0)),
        ] + [cache_spec(c) for c in caches] + [_const_spec(t.shape) for t in tables],
        out_specs=[row, pl.BlockSpec((1, HEADS, Q_PAD, HEAD_DIM), lambda i, j: (seq(i, j), 0, 0, 0))],
        out_shape=[jax.ShapeDtypeStruct((m, D_MODEL), F32),
                   jax.ShapeDtypeStruct((n, HEADS, Q_PAD, HEAD_DIM), F32)],
        scratch_shapes=[pltpu.VMEM((tm, D_MODEL), F32)],
        compiler_params=pltpu.CompilerParams(dimension_semantics=("parallel", "arbitrary"),
                                             vmem_limit_bytes=FUSED_VMEM_LIMIT_BYTES),
        name="mlp_attn",
    )(x, g.reshape(1, D_MODEL), w_up, w_down, q5, kvn, *caches, *tables)
    return out, o[:, :, :dec_seq].transpose(0, 2, 1, 3).reshape(n * dec_seq, ATTN_OUT)


HALO = 16


def _pool_prompt_kernel(x_ref, g_ref, w_ref, sc_ref, o_ref, st_ref, hp_ref, *, tt):
    i = pl.program_id(1)

    @pl.when(i == 0)
    def _():
        hp_ref[0:HALO, :] = jnp.zeros((HALO, D_MODEL), F32)

    x = x_ref[0]
    h = _rms(x, g_ref[...])
    hp_ref[HALO:HALO + tt, :] = h
    pos1 = (i * tt + 1 + lax.broadcasted_iota(jnp.int32, (tt, 1), 0)).astype(F32)
    ys = []
    for g, w in enumerate(POOL_WINDOWS):
        cols = slice(g * POOL_CH, (g + 1) * POOL_CH)
        hg = h[:, cols]
        win = hg
        for j in range(1, w):
            win = win + hp_ref[HALO - j:HALO - j + tt, cols]
        p = win / jnp.minimum(float(w), pos1) - hg
        ys.append(jnp.dot(p.astype(BF16), w_ref[g], preferred_element_type=F32))
    o_ref[0] = x + jnp.concatenate(ys, axis=1) * sc_ref[...]
    tail = hp_ref[tt:tt + HALO, :]
    hp_ref[0:HALO, :] = tail

    @pl.when(i == pl.num_programs(1) - 1)
    def _():
        st_ref[0] = tail


def _pool_prompt(x, g, w_pool, scale, batch, seq):
    tt = min(ROW_TILE, seq)
    out, st = pl.pallas_call(
        functools.partial(_pool_prompt_kernel, tt=tt),
        grid=(batch, seq // tt),
        in_specs=[
            pl.BlockSpec((1, tt, D_MODEL), lambda b, i: (b, i, 0)),
            _const_spec((1, D_MODEL)),
            _const_spec(w_pool.shape),
            _const_spec((1, D_MODEL)),
        ],
        out_specs=[
            pl.BlockSpec((1, tt, D_MODEL), lambda b, i: (b, i, 0)),
            pl.BlockSpec((1, HALO, D_MODEL), lambda b, i: (b, 0, 0)),
        ],
        out_shape=[
            jax.ShapeDtypeStruct((batch, seq, D_MODEL), F32),
            jax.ShapeDtypeStruct((batch, HALO, D_MODEL), F32),
        ],
        scratch_shapes=[pltpu.VMEM((HALO + tt, D_MODEL), F32)],
        compiler_params=_cparams("parallel", "arbitrary"),
        name="pool_prompt",
    )(x.reshape(batch, seq, D_MODEL), g.reshape(1, D_MODEL), w_pool, scale.reshape(1, D_MODEL))
    return out.reshape(batch * seq, D_MODEL), st[:, HALO - POOL_STATE:, :]


def _pool_sample_kernel(x_ref, st_ref, g_ref, w_ref, sc_ref, o_ref, nst_ref, *, dec_seq, past_len):
    chunk = lambda ref, k: ref[:, k * D_MODEL:(k + 1) * D_MODEL]
    xs = [chunk(x_ref, t) for t in range(dec_seq)]
    hs = [_rms(x, g_ref[...]) for x in xs]
    rows = [st_ref[k] for k in range(POOL_STATE)] + hs
    for t in range(dec_seq):
        ys = []
        for g, w in enumerate(POOL_WINDOWS):
            cols = slice(g * POOL_CH, (g + 1) * POOL_CH)
            win = rows[POOL_STATE + t][:, cols]
            for j in range(1, w):
                win = win + rows[POOL_STATE + t - j][:, cols]
            p = win / float(min(w, past_len + t + 1)) - hs[t][:, cols]
            ys.append(jnp.dot(p.astype(BF16), w_ref[g], preferred_element_type=F32))
        o_ref[:, t * D_MODEL:(t + 1) * D_MODEL] = xs[t] + jnp.concatenate(ys, axis=1) * sc_ref[...]
    new_rows = rows[-POOL_STATE:]
    for k in range(POOL_STATE):
        nst_ref[k] = new_rows[k]


def _pool_sample(x, state_t, li, g, w_pool, scale, n, dec_seq, past_len):
    bn = min(32, n)
    out, nst = pl.pallas_call(
        functools.partial(_pool_sample_kernel, dec_seq=dec_seq, past_len=past_len),
        grid=(n // bn,),
        in_specs=[
            pl.BlockSpec((bn, dec_seq * D_MODEL), lambda i: (i, 0)),
            pl.BlockSpec((None, POOL_STATE, bn, D_MODEL), lambda i: (li, 0, i, 0)),
            _const_spec((1, D_MODEL)),
            _const_spec(w_pool.shape),
            _const_spec((1, D_MODEL)),
        ],
        out_specs=[
            pl.BlockSpec((bn, dec_seq * D_MODEL), lambda i: (i, 0)),
            pl.BlockSpec((POOL_STATE, bn, D_MODEL), lambda i: (0, i, 0)),
        ],
        out_shape=[
            jax.ShapeDtypeStruct((n, dec_seq * D_MODEL), F32),
            jax.ShapeDtypeStruct((POOL_STATE, n, D_MODEL), F32),
        ],
        compiler_params=_cparams("parallel"),
        name="pool_sample",
    )(x.reshape(n, dec_seq * D_MODEL), state_t, g.reshape(1, D_MODEL), w_pool, scale.reshape(1, D_MODEL))
    return out.reshape(n * dec_seq, D_MODEL), nst


def kernel(x_prompt, x_sample, state_pool, cache_kv_w128, cache_kv_w512, cache_kv_w2048, rel_bias, norm_mix,
           norm_ffn, norm_final, w_pool, pool_scale, w_qkv, w_o, w_up, w_down):
    batch, seq, _ = x_prompt.shape
    n, dec_seq, _ = x_sample.shape
    depth = norm_mix.shape[0]
    caches = tuple(jnp.transpose(c, (0, 1, 3, 4, 5, 2)) for c in (cache_kv_w128, cache_kv_w512, cache_kv_w2048))
    state_t = jnp.transpose(state_pool, (0, 2, 1, 3))
    past_len = PAST_LEN

    xp = x_prompt.reshape(batch * seq, D_MODEL)
    xs = x_sample.reshape(n * dec_seq, D_MODEL)
    w_up_b = w_up.astype(BF16).reshape(depth, D_MODEL, FF_CHUNKS, FF_CHUNK).transpose(0, 2, 1, 3)
    w_down_b = w_down.astype(BF16).reshape(depth, FF_CHUNKS, FF_CHUNK, D_MODEL)
    w_pool_b = w_pool.astype(BF16)
    w_o_b = w_o.astype(BF16)
    w_qkv_b = w_qkv.astype(BF16)
    w_kv_t = jnp.transpose(w_qkv_b[:, :, Q_COLS:], (0, 2, 1))
    prompt_bias = [_prompt_bias(rel_bias[:, g], dil) for g, (_, dil) in enumerate(ATTN_GROUPS)]
    prompt_cap = _prompt_cap()
    sample_tables = _sample_tables(rel_bias, dec_seq, tuple(c.shape[-1] for c in caches))

    kv_rows_major = lambda per_layer: jnp.transpose(jnp.stack(per_layer), (0, 1, 5, 2, 3, 4))

    pool_p, pool_s = [], []
    kv_p = [[] for _ in ATTN_GROUPS]
    kv_s = [[] for _ in ATTN_GROUPS]
    ahead = None
    for i in range(depth):
        li = i // 2
        last = i == depth - 1
        if i % 2 == 0:
            xp, sp = _pool_prompt(xp, norm_mix[i], w_pool_b[li], pool_scale[li], batch, seq)
            xs, ss = _pool_sample(xs, state_t, li, norm_mix[i], w_pool_b[li], pool_scale[li], n, dec_seq,
                                  past_len)
            pool_p.append(sp)
            pool_s.append(ss)
        else:
            qps, kvps, kvts = _qkv_prompt(xp, norm_mix[i], w_qkv_b[li], w_kv_t[li], batch, seq)
            os, lses = zip(*[_attn_prompt(qps[g], kvps[g], prompt_bias[g], prompt_cap) for g in range(N_GROUPS)])
            xp = _wo_merge(os, lses, w_o_b[li], xp, batch, seq)
            if ahead is None:
                qs, kvbs, kvfs = _qkv(xs, norm_mix[i], w_qkv_b[li])
                o_s = _attn_sample(qs, kvbs, caches, li, sample_tables, n, dec_seq)
            else:
                o_s, kvfs = ahead
                ahead = None
            xs = _wo(o_s, w_o_b[li], xs)
            kvfs = kvfs.reshape(n, dec_seq, N_GROUPS, 2, HEADS, HEAD_DIM)
            for g in range(N_GROUPS):
                kv_p[g].append(kvts[g])
                kv_s[g].append(kvfs[:, :, g])
        xs = _mlp(xs, norm_ffn[i], w_up_b, w_down_b, i, norm_final, last)
        if i % 2 == 0 and i + 1 < depth:
            nli = (i + 1) // 2
            qs, kvbs, kvfs = _qkv(xs, norm_mix[i + 1], w_qkv_b[nli])
            xp, o_s = _mlp_with_sample_attn(xp, norm_ffn[i], w_up_b, w_down_b, i, qs, kvbs, caches, nli,
                                            sample_tables, n, dec_seq)
            ahead = (o_s, kvfs)
        else:
            xp = _mlp(xp, norm_ffn[i], w_up_b, w_down_b, i, norm_final, last)
    return (xp.reshape(batch, seq, D_MODEL), xs.reshape(n, dec_seq, D_MODEL),
            jnp.stack(pool_p), jnp.transpose(jnp.stack(pool_s), (0, 2, 1, 3)),
            kv_rows_major(kv_p[0]), jnp.stack(kv_s[0]),
            kv_rows_major(kv_p[1]), jnp.stack(kv_s[1]),
            kv_rows_major(kv_p[2]), jnp.stack(kv_s[2]))
```

```python
import functools

import numpy as np
import jax
import jax.numpy as jnp
from jax import lax
from jax.experimental import pallas as pl
from jax.experimental.pallas import tpu as pltpu

F32 = jnp.float32
BF16 = jnp.bfloat16

D_MODEL = 1024
D_FF = 4 * D_MODEL
POOL_WINDOWS = (2, 4, 8, 16)
POOL_CH = D_MODEL // len(POOL_WINDOWS)
POOL_STATE = max(POOL_WINDOWS) - 1
ATTN_GROUPS = ((128, 1), (512, 4), (2048, 16))
N_GROUPS = len(ATTN_GROUPS)
HEADS = 8
HEAD_DIM = 64
ATTN_OUT = HEADS * HEAD_DIM
BAND = 128
N_BUCKETS = 32
MAX_EXACT = N_BUCKETS // 2
REL_MAX_DIST = 2048
PAST_LEN = 2048
RMS_EPS = 1e-6
NEG_INF = -1e30

VMEM_LIMIT_BYTES = 56 * 1024 * 1024
FUSED_VMEM_LIMIT_BYTES = 62 * 1024 * 1024
ROW_TILE = 512
FF_CHUNK = 1024
FF_CHUNKS = D_FF // FF_CHUNK
_NT = (((1,), (1,)), ((), ()))
LANES = 128


def _cparams(*sem):
    return pltpu.CompilerParams(dimension_semantics=sem, vmem_limit_bytes=VMEM_LIMIT_BYTES)


def _rms(x, g):
    ms = jnp.mean(x * x, axis=-1, keepdims=True)
    return x * lax.rsqrt(ms + RMS_EPS) * g


def _t5_bucket(dist):
    n = np.maximum(np.asarray(dist), 0)
    large = MAX_EXACT + (np.log(np.maximum(n, 1) / MAX_EXACT) / np.log(REL_MAX_DIST / MAX_EXACT)
                         * (N_BUCKETS - MAX_EXACT)).astype(np.int64)
    large = np.minimum(large, N_BUCKETS - 1)
    return np.where(n < MAX_EXACT, n, large).astype(np.int32)


def _const_spec(shape):
    nd = len(shape)
    return pl.BlockSpec(shape, lambda *_: (0,) * nd, pipeline_mode=pl.Buffered(1))


def _layer_spec(shape, layer):
    nd = len(shape)
    return pl.BlockSpec((None,) + tuple(shape), lambda *_: (layer,) + (0,) * nd, pipeline_mode=pl.Buffered(1))


def _mlp_kernel(x_ref, g_ref, wu_ref, wd_ref, gf_ref, o_ref, *, final_norm):
    x = x_ref[...]
    h = _rms(x, g_ref[...]).astype(BF16)
    acc = x
    for c in range(FF_CHUNKS):
        u = jnp.dot(h, wu_ref[:, c * FF_CHUNK:(c + 1) * FF_CHUNK], preferred_element_type=F32)
        a = jnp.square(jnp.maximum(u, 0.0)).astype(BF16)
        acc = acc + jnp.dot(a, wd_ref[c], preferred_element_type=F32)
    if final_norm:
        acc = _rms(acc, gf_ref[...])
    o_ref[...] = acc


def _mlp(x, g, w_up, w_down, layer, g_final, final_norm):
    m = x.shape[0]
    tm = min(ROW_TILE, m)
    return pl.pallas_call(
        functools.partial(_mlp_kernel, final_norm=final_norm),
        grid=(m // tm,),
        in_specs=[
            pl.BlockSpec((tm, D_MODEL), lambda i: (i, 0)),
            _const_spec((1, D_MODEL)),
            _layer_spec((D_MODEL, D_FF), layer),
            _layer_spec((FF_CHUNKS, FF_CHUNK, D_MODEL), layer),
            _const_spec((1, D_MODEL)),
        ],
        out_specs=pl.BlockSpec((tm, D_MODEL), lambda i: (i, 0)),
        out_shape=jax.ShapeDtypeStruct((m, D_MODEL), F32),
        compiler_params=_cparams("parallel"),
        name="mlp",
    )(x, g.reshape(1, D_MODEL), w_up, w_down, g_final.reshape(1, D_MODEL))


Q_COLS = N_GROUPS * ATTN_OUT
KV_COLS = 2 * N_GROUPS * ATTN_OUT


def _w_cols(which, g):
    start = (which * N_GROUPS + g) * ATTN_OUT
    return slice(start, start + ATTN_OUT)


def _qkv_kernel(x_ref, g_ref, w_ref, q_ref, kvb_ref, kvf_ref):
    h = _rms(x_ref[...], g_ref[...]).astype(BF16)
    q = jnp.dot(h, w_ref[:, :Q_COLS], preferred_element_type=F32)
    q_ref[...] = (q * (HEAD_DIM ** -0.5)).astype(BF16)
    for g in range(N_GROUPS):
        for which in (1, 2):
            out = slice((2 * g + which - 1) * ATTN_OUT, (2 * g + which) * ATTN_OUT)
            kv = jnp.dot(h, w_ref[:, _w_cols(which, g)], preferred_element_type=F32)
            kvf_ref[:, out] = kv
            kvb_ref[:, out] = kv.astype(BF16)


def _qkv(x, g, w):
    m = x.shape[0]
    tm = min(ROW_TILE, m)
    row = lambda i: (i, 0)
    return pl.pallas_call(
        _qkv_kernel,
        grid=(m // tm,),
        in_specs=[
            pl.BlockSpec((tm, D_MODEL), row),
            _const_spec((1, D_MODEL)),
            _const_spec((D_MODEL, Q_COLS + KV_COLS)),
        ],
        out_specs=[
            pl.BlockSpec((tm, Q_COLS), row),
            pl.BlockSpec((tm, KV_COLS), row),
            pl.BlockSpec((tm, KV_COLS), row),
        ],
        out_shape=[
            jax.ShapeDtypeStruct((m, Q_COLS), BF16),
            jax.ShapeDtypeStruct((m, KV_COLS), BF16),
            jax.ShapeDtypeStruct((m, KV_COLS), F32),
        ],
        compiler_params=_cparams("parallel"),
        name="qkv",
    )(x, g.reshape(1, D_MODEL), w)


def _qkv_prompt_kernel(x_ref, g_ref, w_ref, wt_ref, *rest, tm, n_tiles, wins):
    q_refs, kv_refs, kvt_refs, scr_ref = rest[0:3], rest[3:6], rest[6:9], rest[9]
    i = pl.program_id(1)
    hf = _rms(x_ref[0], g_ref[...])
    h = hf.astype(BF16)

    h_by_dil = {1: h}
    dils = sorted({dil for _, dil in ATTN_GROUPS if dil > 1})
    chunks = D_MODEL // LANES
    if dils:
        for c in range(chunks):
            scr_ref[c] = hf[:, c * LANES:(c + 1) * LANES]
    for dil in dils:
        classes = [jnp.concatenate([scr_ref[c, pl.ds(r, tm // dil, stride=dil), :] for c in range(chunks)], axis=1)
                   for r in range(dil)]
        h_by_dil[dil] = jnp.concatenate(classes, axis=0).astype(BF16)

    for g, (_, dil) in enumerate(ATTN_GROUPS):
        hg = h_by_dil[dil]
        q = jnp.dot(hg, w_ref[:, _w_cols(0, g)], preferred_element_type=F32) * (HEAD_DIM ** -0.5)
        q_refs[g][0] = q.reshape(dil, tm // dil, ATTN_OUT).astype(BF16)
        for which in (1, 2):
            kv = jnp.dot(hg, w_ref[:, _w_cols(which, g)], preferred_element_type=F32)
            kv_refs[g][0, :, :, (which - 1) * ATTN_OUT:which * ATTN_OUT] = (
                kv.reshape(dil, tm // dil, ATTN_OUT).astype(BF16))

    for g, win in enumerate(wins):
        rows = min(win, tm)
        first_tile = n_tiles - max(win // tm, 1)

        @pl.when(i >= first_tile)
        def _(g=g, rows=rows):
            for which in (1, 2):
                start = ((which - 1) * N_GROUPS + g) * ATTN_OUT
                kvt = lax.dot_general(wt_ref[start:start + ATTN_OUT, :], h[tm - rows:, :], _NT,
                                      preferred_element_type=F32)
                kvt_refs[g][0, which - 1] = kvt.reshape(HEADS, HEAD_DIM, rows)


def _qkv_prompt(x, g, w, w_t, batch, seq):
    tm = min(ROW_TILE, seq)
    n_tiles = seq // tm
    wins = tuple(min(win, seq) for win, _ in ATTN_GROUPS)
    assert all(w_ % tm == 0 or tm % w_ == 0 for w_ in wins)
    split = lambda width: [
        (pl.BlockSpec((1, dil, tm // dil, width), lambda b, i: (b, 0, i, 0)),
         jax.ShapeDtypeStruct((batch, dil, seq // dil, width), BF16)) for _, dil in ATTN_GROUPS]
    kvt = [(pl.BlockSpec((1, 2, HEADS, HEAD_DIM, min(win, tm)),
                         lambda b, i, first=n_tiles - max(win // tm, 1): (b, 0, 0, 0, jnp.maximum(i - first, 0))),
            jax.ShapeDtypeStruct((batch, 2, HEADS, HEAD_DIM, win), F32)) for win in wins]
    outs = split(ATTN_OUT) + split(2 * ATTN_OUT) + kvt
    res = pl.pallas_call(
        functools.partial(_qkv_prompt_kernel, tm=tm, n_tiles=n_tiles, wins=wins),
        grid=(batch, n_tiles),
        in_specs=[
            pl.BlockSpec((1, tm, D_MODEL), lambda b, i: (b, i, 0)),
            _const_spec((1, D_MODEL)),
            _const_spec((D_MODEL, Q_COLS + KV_COLS)),
            _const_spec((KV_COLS, D_MODEL)),
        ],
        out_specs=[o[0] for o in outs],
        out_shape=[o[1] for o in outs],
        scratch_shapes=[pltpu.VMEM((D_MODEL // LANES, tm, LANES), F32)],
        compiler_params=_cparams("parallel", "arbitrary"),
        name="qkv_prompt",
    )(x.reshape(batch, seq, D_MODEL), g.reshape(1, D_MODEL), w, w_t)
    return res[0:3], res[3:6], res[6:9]


def _attn_prompt_kernel(q_ref, kvp_ref, kvc_ref, bias_ref, cap_ref, o_ref, st_ref, s_scr, p_scr, *,
                        classes, blocks):
    n = pl.program_id(2)
    lane = lax.broadcasted_iota(jnp.int32, (BAND, LANES), 1)
    low_half = lane < HEAD_DIM
    half_sel = [low_half.astype(F32).astype(BF16), (~low_half).astype(F32).astype(BF16)]

    for cls, blk in [(c, b) for c in range(classes) for b in range(blocks)]:
        rows = slice(blk * BAND, (blk + 1) * BAND)
        if blk == 0:
            first = jnp.where(n > 0, 0, 1)
            prev = lambda cols, cls=cls: kvp_ref[0, cls, :, cols]
        else:
            first = 0
            prev = lambda cols, cls=cls, blk=blk: kvc_ref[0, cls, (blk - 1) * BAND:blk * BAND, cols]
        keys = lambda cols, prev=prev, cls=cls, rows=rows: jnp.concatenate(
            [prev(cols), kvc_ref[0, cls, rows, cols]], axis=0)

        for pair in range(HEADS // 2):
            cols = slice(pair * LANES, (pair + 1) * LANES)
            q2 = q_ref[0, cls, rows, cols]
            k2 = keys(cols)
            for half in range(2):
                s_scr[2 * pair + half] = lax.dot_general(q2 * half_sel[half], k2, _NT,
                                                         preferred_element_type=F32)

        stats = jnp.zeros((BAND, LANES), F32)
        for h in range(HEADS):
            t = jnp.minimum(s_scr[h] + bias_ref[h], cap_ref[first])
            mx = jnp.max(t, axis=-1, keepdims=True)
            e = jnp.exp(t - mx)
            den = jnp.sum(e, axis=-1, keepdims=True)
            p_scr[h] = e.astype(BF16)
            stats = jnp.where(lane == h, mx, jnp.where(lane == HEADS + h, den, stats))
        st_ref[0, cls, rows, :] = stats

        for pair in range(HEADS // 2):
            cols = slice(pair * LANES, (pair + 1) * LANES)
            v2 = keys(slice(ATTN_OUT + cols.start, ATTN_OUT + cols.stop))
            o_lo = jnp.dot(p_scr[2 * pair], v2, preferred_element_type=F32)
            o_hi = jnp.dot(p_scr[2 * pair + 1], v2, preferred_element_type=F32)
            o_ref[0, cls, rows, cols] = jnp.where(low_half, o_lo, o_hi)


BLOCKS_PER_STEP = 4


def _attn_prompt(q, kv, bias, cap):
    batch, dil, sub, _ = q.shape
    blocks = min(BLOCKS_PER_STEP, sub // BAND)
    classes = min(BLOCKS_PER_STEP // blocks, dil)
    step = blocks * BAND
    assert sub % step == 0 and dil % classes == 0
    cur = lambda b, r, n: (b, r, n, 0)
    return pl.pallas_call(
        functools.partial(_attn_prompt_kernel, classes=classes, blocks=blocks),
        grid=(batch, dil // classes, sub // step),
        in_specs=[
            pl.BlockSpec((1, classes, step, ATTN_OUT), cur),
            pl.BlockSpec((1, classes, BAND, 2 * ATTN_OUT),
                         lambda b, r, n: (b, r, jnp.maximum(n * blocks - 1, 0), 0)),
            pl.BlockSpec((1, classes, step, 2 * ATTN_OUT), cur),
            _const_spec((HEADS, BAND, 2 * BAND)),
            _const_spec((2, BAND, 2 * BAND)),
        ],
        out_specs=[
            pl.BlockSpec((1, classes, step, ATTN_OUT), cur),
            pl.BlockSpec((1, classes, step, LANES), cur),
        ],
        out_shape=[
            jax.ShapeDtypeStruct((batch, dil, sub, ATTN_OUT), F32),
            jax.ShapeDtypeStruct((batch, dil, sub, LANES), F32),
        ],
        scratch_shapes=[pltpu.VMEM((HEADS, BAND, 2 * BAND), F32), pltpu.VMEM((HEADS, BAND, 2 * BAND), BF16)],
        compiler_params=_cparams("parallel", "parallel", "arbitrary"),
        name="attn_prompt_d%d" % dil,
    )(q, kv, kv, bias, cap)


def _prompt_bias(rel_bias_g, dil):
    tab = rel_bias_g.astype(F32)[_t5_bucket(np.arange(BAND, -1, -1) * dil)]
    ext = jnp.pad(tab.T, ((0, 0), (BAND, BAND)))
    tiled = jnp.tile(ext, (1, BAND))[:, :BAND * 3 * BAND].reshape(HEADS, BAND, 3 * BAND)
    return tiled[:, :, BAND:]


def _prompt_cap():
    m = BAND + np.arange(BAND)[:, None] - np.arange(2 * BAND)[None, :]
    band = (m >= 0) & (m <= BAND)
    no_prev = band & (np.arange(2 * BAND) >= BAND)[None, :]
    big = np.finfo(np.float32).max
    return jnp.asarray(np.where(np.stack([band, no_prev]), big, NEG_INF).astype(np.float32))


def _wo_merge_kernel(o0_ref, o1_ref, o2_ref, l0_ref, l1_ref, l2_ref, e_ref, w_ref, x_ref, out_ref,
                     oscr_ref, lscr_ref, *, tm):
    def token_order(ref, scr_ref, dil):
        if dil == 1:
            return ref[0, 0]
        chunks = ref.shape[-1] // LANES
        for r in range(dil):
            for c in range(chunks):
                scr_ref[c, pl.ds(r, tm // dil, stride=dil), :] = ref[0, r, :, c * LANES:(c + 1) * LANES]
        return jnp.concatenate([scr_ref[c] for c in range(chunks)], axis=1)

    dils = [dil for _, dil in ATTN_GROUPS]
    sts = [token_order(l_ref, lscr_ref.at[g:g + 1], dils[g]) for g, l_ref in enumerate((l0_ref, l1_ref, l2_ref))]
    mxs = [st[:, :HEADS] for st in sts]
    dens = [st[:, HEADS:2 * HEADS] for st in sts]
    mx = jnp.maximum(jnp.maximum(mxs[0], mxs[1]), mxs[2])
    es = [jnp.exp(m - mx) for m in mxs]
    inv = 1.0 / (es[0] * dens[0] + es[1] * dens[1] + es[2] * dens[2])
    acc = None
    for g, o_ref in enumerate((o0_ref, o1_ref, o2_ref)):
        wt = es[g] * inv
        hi = wt.astype(BF16)
        lo = (wt - hi.astype(F32)).astype(BF16)
        wexp = (jnp.dot(hi, e_ref[...], preferred_element_type=F32)
                + jnp.dot(lo, e_ref[...], preferred_element_type=F32))
        term = wexp * token_order(o_ref, oscr_ref, dils[g])
        acc = term if acc is None else acc + term
    out_ref[0] = x_ref[0] + jnp.dot(acc.astype(BF16), w_ref[...], preferred_element_type=F32)


def _wo_merge(os, lses, w_o, x, batch, seq):
    tm = min(ROW_TILE, seq)
    expand = jnp.asarray(np.repeat(np.eye(HEADS, dtype=np.float32), HEAD_DIM, axis=1), BF16)
    split = lambda width: [pl.BlockSpec((1, dil, tm // dil, width), lambda b, i: (b, 0, i, 0))
                           for _, dil in ATTN_GROUPS]
    row = pl.BlockSpec((1, tm, D_MODEL), lambda b, i: (b, i, 0))
    out = pl.pallas_call(
        functools.partial(_wo_merge_kernel, tm=tm),
        grid=(batch, seq // tm),
        in_specs=split(ATTN_OUT) + split(LANES) + [
            _const_spec((HEADS, ATTN_OUT)),
            _const_spec((ATTN_OUT, D_MODEL)),
            row,
        ],
        out_specs=row,
        out_shape=jax.ShapeDtypeStruct((batch, seq, D_MODEL), F32),
        scratch_shapes=[pltpu.VMEM((ATTN_OUT // LANES, tm, LANES), F32), pltpu.VMEM((N_GROUPS, tm, LANES), F32)],
        compiler_params=_cparams("parallel", "parallel"),
        name="wo_merge",
    )(*os, *lses, expand, w_o, x.reshape(batch, seq, D_MODEL))
    return out.reshape(batch * seq, D_MODEL)


def _wo_kernel(o_ref, w_ref, x_ref, out_ref):
    out_ref[...] = x_ref[...] + jnp.dot(o_ref[...].astype(BF16), w_ref[...], preferred_element_type=F32)


def _wo(o, w_o, x):
    m = x.shape[0]
    tm = min(ROW_TILE, m)
    row = lambda i: (i, 0)
    return pl.pallas_call(
        _wo_kernel,
        grid=(m // tm,),
        in_specs=[pl.BlockSpec((tm, ATTN_OUT), row), _const_spec((ATTN_OUT, D_MODEL)),
                  pl.BlockSpec((tm, D_MODEL), row)],
        out_specs=pl.BlockSpec((tm, D_MODEL), row),
        out_shape=jax.ShapeDtypeStruct((m, D_MODEL), F32),
        compiler_params=_cparams("parallel"),
        name="wo",
    )(o, w_o, x)


Q_PAD = 8
NEW_PAD = 16


def _sample_tables(rel_bias, dec_seq, buf_rows):
    tq = np.arange(Q_PAD)
    tables = []
    for g, (_, dil) in enumerate(ATTN_GROUPS):
        wb = buf_rows[g]
        by_dist = rel_bias[:, g].astype(F32)[_t5_bucket(np.arange(wb + Q_PAD))]
        rev = by_dist[::-1].T
        for kpos, real in ((np.arange(wb), np.ones(wb, bool)),
                           (wb + np.arange(NEW_PAD), np.arange(NEW_PAD) < dec_seq)):
            delta = (wb + tq)[:, None] - kpos[None, :]
            ok = (delta >= 0) & (delta % dil == 0) & (delta // dil <= BAND)
            ok &= (tq < dec_seq)[:, None] & real[None, :]
            if kpos.shape[0] == wb:
                bias = jnp.stack([rev[:, Q_PAD - 1 - t:Q_PAD - 1 - t + wb] for t in range(Q_PAD)], axis=1)
            else:
                bias = jnp.transpose(by_dist[np.maximum(delta, 0)], (2, 0, 1))
            tables.append(bias)
            tables.append(jnp.asarray(ok.astype(np.float32)))
    return tables


def _attn_sample_body(q_ref, kvn_ref, c_refs, tables, o_ref, s_scr):
    rows = [c.shape[-1] for c in c_refs]
    offs = [sum(rows[:g]) for g in range(len(rows))]
    head_rows = lambda h: slice(h * Q_PAD, (h + 1) * Q_PAD)
    new_scores = []
    for g, c_ref in enumerate(c_refs):
        bias_ref, valid_ref, nbias_ref, nvalid_ref = tables[4 * g:4 * g + 4]
        per_head = []
        for h in range(HEADS):
            q = q_ref[0, g, h]
            s = jnp.dot(q, c_ref[0, h].astype(BF16), preferred_element_type=F32)
            s_scr[head_rows(h), offs[g]:offs[g] + rows[g]] = jnp.where(valid_ref[...] > 0.0, s + bias_ref[h], NEG_INF)
            s = lax.dot_general(q, kvn_ref[0, g, 0, h], _NT, preferred_element_type=F32)
            per_head.append(jnp.where(nvalid_ref[...] > 0.0, s + nbias_ref[h], NEG_INF))
        new_scores.append(jnp.concatenate(per_head, axis=0))

    s = s_scr[...]
    mx = jnp.max(s, axis=-1, keepdims=True)
    for sn in new_scores:
        mx = jnp.maximum(mx, jnp.max(sn, axis=-1, keepdims=True))
    e = jnp.exp(s - mx)
    den = jnp.sum(e, axis=-1, keepdims=True)
    s_scr[...] = e
    new_probs = []
    for sn in new_scores:
        en = jnp.exp(sn - mx)
        den = den + jnp.sum(en, axis=-1, keepdims=True)
        new_probs.append(en)
    inv = 1.0 / den

    for h in range(HEADS):
        acc = jnp.zeros((Q_PAD, HEAD_DIM), F32)
        for g, c_ref in enumerate(c_refs):
            p = s_scr[head_rows(h), offs[g]:offs[g] + rows[g]].astype(BF16)
            acc = acc + lax.dot_general(p, c_ref[1, h].astype(BF16), _NT, preferred_element_type=F32)
            acc = acc + jnp.dot(new_probs[g][head_rows(h), :].astype(BF16), kvn_ref[0, g, 1, h],
                                preferred_element_type=F32)
        o_ref[0, h] = acc * inv[head_rows(h), :]


def _mlp_attn_kernel(x_ref, g_ref, wu_ref, wd_ref, q_ref, kvn_ref, c0_ref, c1_ref, c2_ref, *rest):
    tables, out_ref, o_ref, acc_ref, s_scr = rest[:-4], rest[-4], rest[-3], rest[-2], rest[-1]
    j = pl.program_id(1)

    @pl.when(j == 0)
    def _():
        acc_ref[...] = x_ref[...]

    h = _rms(x_ref[...], g_ref[...]).astype(BF16)
    u = jnp.dot(h, wu_ref[:, pl.ds(pl.multiple_of(j * FF_CHUNK, FF_CHUNK), FF_CHUNK)], preferred_element_type=F32)
    a = jnp.square(jnp.maximum(u, 0.0)).astype(BF16)
    acc = acc_ref[...] + jnp.dot(a, wd_ref[j], preferred_element_type=F32)
    acc_ref[...] = acc
    out_ref[...] = acc
    _attn_sample_body(q_ref, kvn_ref, (c0_ref, c1_ref, c2_ref), tables, o_ref, s_scr)


def _sample_qkv_by_head(q, kvb, n, dec_seq):
    q5 = q.reshape(n, dec_seq, N_GROUPS, HEADS, HEAD_DIM).transpose(0, 2, 3, 1, 4)
    q5 = jnp.pad(q5, ((0, 0), (0, 0), (0, 0), (0, Q_PAD - dec_seq), (0, 0)))
    kvn = kvb.reshape(n, dec_seq, N_GROUPS, 2, HEADS, HEAD_DIM).transpose(0, 2, 3, 4, 1, 5)
    kvn = jnp.pad(kvn, ((0, 0), (0, 0), (0, 0), (0, 0), (0, NEW_PAD - dec_seq), (0, 0)))
    return q5, kvn


def _mlp_with_sample_attn(x, g, w_up, w_down, layer, q, kvb, caches, li, tables, n, dec_seq):
    m = x.shape[0]
    tm = min(ROW_TILE, m)
    assert n % (m // tm) == 0 and n // (m // tm) == FF_CHUNKS
    q5, kvn = _sample_qkv_by_head(q, kvb, n, dec_seq)
    seq = lambda i, j: i * FF_CHUNKS + j
    cache_spec = lambda c: pl.BlockSpec((None, None) + c.shape[2:], lambda i, j: (li, seq(i, j), 0, 0, 0, 0))
    row = pl.BlockSpec((tm, D_MODEL), lambda i, j: (i, 0))
    out, o = pl.pallas_call(
        _mlp_attn_kernel,
        grid=(m // tm, FF_CHUNKS),
        in_specs=[
            row,
            _const_spec((1, D_MODEL)),
            _layer_spec((D_MODEL, D_FF), layer),
            _layer_spec((FF_CHUNKS, FF_CHUNK, D_MODEL), layer),
            pl.BlockSpec((1, N_GROUPS, HEADS, Q_PAD, HEAD_DIM), lambda i, j: (seq(i, j), 0, 0, 0, 0)),
            pl.BlockSpec((1, N_GROUPS, 2, HEADS, NEW_PAD, HEAD_DIM), lambda i, j: (seq(i, j), 0, 0, 0, 0, 0)),
        ] + [cache_spec(c) for c in caches] + [_const_spec(t.shape) for t in tables],
        out_specs=[row, pl.BlockSpec((1, HEADS, Q_PAD, HEAD_DIM), lambda i, j: (seq(i, j), 0, 0, 0))],
        out_shape=[jax.ShapeDtypeStruct((m, D_MODEL), F32),
                   jax.ShapeDtypeStruct((n, HEADS, Q_PAD, HEAD_DIM), F32)],
        scratch_shapes=[pltpu.VMEM((tm, D_MODEL), F32),
                        pltpu.VMEM((HEADS * Q_PAD, sum(c.shape[-1] for c in caches)), F32)],
        compiler_params=pltpu.CompilerParams(dimension_semantics=("parallel", "arbitrary"),
                                             vmem_limit_bytes=FUSED_VMEM_LIMIT_BYTES),
        name="mlp_attn",
    )(x, g.reshape(1, D_MODEL), w_up, w_down, q5, kvn, *caches, *tables)
    return out, o[:, :, :dec_seq].transpose(0, 2, 1, 3).reshape(n * dec_seq, ATTN_OUT)


HALO = 16


def _pool_prompt_kernel(x_ref, g_ref, w_ref, sc_ref, o_ref, st_ref, hp_ref, *, tt):
    i = pl.program_id(1)

    @pl.when(i == 0)
    def _():
        hp_ref[0:HALO, :] = jnp.zeros((HALO, D_MODEL), F32)

    x = x_ref[0]
    h = _rms(x, g_ref[...])
    hp_ref[HALO:HALO + tt, :] = h
    pos1 = (i * tt + 1 + lax.broadcasted_iota(jnp.int32, (tt, 1), 0)).astype(F32)
    ys = []
    for g, w in enumerate(POOL_WINDOWS):
        cols = slice(g * POOL_CH, (g + 1) * POOL_CH)
        hg = h[:, cols]
        win = hg
        for j in range(1, w):
            win = win + hp_ref[HALO - j:HALO - j + tt, cols]
        p = win / jnp.minimum(float(w), pos1) - hg
        ys.append(jnp.dot(p.astype(BF16), w_ref[g], preferred_element_type=F32))
    o_ref[0] = x + jnp.concatenate(ys, axis=1) * sc_ref[...]
    tail = hp_ref[tt:tt + HALO, :]
    hp_ref[0:HALO, :] = tail

    @pl.when(i == pl.num_programs(1) - 1)
    def _():
        st_ref[0] = tail


def _pool_prompt(x, g, w_pool, scale, batch, seq):
    tt = min(ROW_TILE, seq)
    out, st = pl.pallas_call(
        functools.partial(_pool_prompt_kernel, tt=tt),
        grid=(batch, seq // tt),
        in_specs=[
            pl.BlockSpec((1, tt, D_MODEL), lambda b, i: (b, i, 0)),
            _const_spec((1, D_MODEL)),
            _const_spec(w_pool.shape),
            _const_spec((1, D_MODEL)),
        ],
        out_specs=[
            pl.BlockSpec((1, tt, D_MODEL), lambda b, i: (b, i, 0)),
            pl.BlockSpec((1, HALO, D_MODEL), lambda b, i: (b, 0, 0)),
        ],
        out_shape=[
            jax.ShapeDtypeStruct((batch, seq, D_MODEL), F32),
            jax.ShapeDtypeStruct((batch, HALO, D_MODEL), F32),
        ],
        scratch_shapes=[pltpu.VMEM((HALO + tt, D_MODEL), F32)],
        compiler_params=_cparams("parallel", "arbitrary"),
        name="pool_prompt",
    )(x.reshape(batch, seq, D_MODEL), g.reshape(1, D_MODEL), w_pool, scale.reshape(1, D_MODEL))
    return out.reshape(batch * seq, D_MODEL), st[:, HALO - POOL_STATE:, :]


def _pool_sample_kernel(x_ref, st_ref, g_ref, w_ref, sc_ref, o_ref, nst_ref, *, dec_seq, past_len):
    chunk = lambda ref, k: ref[:, k * D_MODEL:(k + 1) * D_MODEL]
    xs = [chunk(x_ref, t) for t in range(dec_seq)]
    hs = [_rms(x, g_ref[...]) for x in xs]
    rows = [st_ref[k] for k in range(POOL_STATE)] + hs
    for t in range(dec_seq):
        ys = []
        for g, w in enumerate(POOL_WINDOWS):
            cols = slice(g * POOL_CH, (g + 1) * POOL_CH)
            win = rows[POOL_STATE + t][:, cols]
            for j in range(1, w):
                win = win + rows[POOL_STATE + t - j][:, cols]
            p = win / float(min(w, past_len + t + 1)) - hs[t][:, cols]
            ys.append(jnp.dot(p.astype(BF16), w_ref[g], preferred_element_type=F32))
        o_ref[:, t * D_MODEL:(t + 1) * D_MODEL] = xs[t] + jnp.concatenate(ys, axis=1) * sc_ref[...]
    new_rows = rows[-POOL_STATE:]
    for k in range(POOL_STATE):
        nst_ref[k] = new_rows[k]


def _pool_sample(x, state_t, li, g, w_pool, scale, n, dec_seq, past_len):
    bn = min(32, n)
    out, nst = pl.pallas_call(
        functools.partial(_pool_sample_kernel, dec_seq=dec_seq, past_len=past_len),
        grid=(n // bn,),
        in_specs=[
            pl.BlockSpec((bn, dec_seq * D_MODEL), lambda i: (i, 0)),
            pl.BlockSpec((None, POOL_STATE, bn, D_MODEL), lambda i: (li, 0, i, 0)),
            _const_spec((1, D_MODEL)),
            _const_spec(w_pool.shape),
            _const_spec((1, D_MODEL)),
        ],
        out_specs=[
            pl.BlockSpec((bn, dec_seq * D_MODEL), lambda i: (i, 0)),
            pl.BlockSpec((POOL_STATE, bn, D_MODEL), lambda i: (0, i, 0)),
        ],
        out_shape=[
            jax.ShapeDtypeStruct((n, dec_seq * D_MODEL), F32),
            jax.ShapeDtypeStruct((POOL_STATE, n, D_MODEL), F32),
        ],
        compiler_params=_cparams("parallel"),
        name="pool_sample",
    )(x.reshape(n, dec_seq * D_MODEL), state_t, g.reshape(1, D_MODEL), w_pool, scale.reshape(1, D_MODEL))
    return out.reshape(n * dec_seq, D_MODEL), nst


def kernel(x_prompt, x_sample, state_pool, cache_kv_w128, cache_kv_w512, cache_kv_w2048, rel_bias, norm_mix,
           norm_ffn, norm_final, w_pool, pool_scale, w_qkv, w_o, w_up, w_down):
    batch, seq, _ = x_prompt.shape
    n, dec_seq, _ = x_sample.shape
    depth = norm_mix.shape[0]
    caches = tuple(jnp.transpose(c, (0, 1, 3, 4, 5, 2)) for c in (cache_kv_w128, cache_kv_w512, cache_kv_w2048))
    state_t = jnp.transpose(state_pool, (0, 2, 1, 3))
    past_len = PAST_LEN

    xp = x_prompt.reshape(batch * seq, D_MODEL)
    xs = x_sample.reshape(n * dec_seq, D_MODEL)
    w_up_b = w_up.astype(BF16)
    w_down_b = w_down.astype(BF16).reshape(depth, FF_CHUNKS, FF_CHUNK, D_MODEL)
    w_pool_b = w_pool.astype(BF16)
    w_o_b = w_o.astype(BF16)
    w_qkv_b = w_qkv.astype(BF16)
    w_kv_t = jnp.transpose(w_qkv_b[:, :, Q_COLS:], (0, 2, 1))
    prompt_bias = [_prompt_bias(rel_bias[:, g], dil) for g, (_, dil) in enumerate(ATTN_GROUPS)]
    prompt_cap = _prompt_cap()
    sample_tables = _sample_tables(rel_bias, dec_seq, tuple(c.shape[-1] for c in caches))

    kv_rows_major = lambda per_layer: jnp.transpose(jnp.stack(per_layer), (0, 1, 5, 2, 3, 4))

    pool_p, pool_s = [], []
    kv_p = [[] for _ in ATTN_GROUPS]
    kv_s = [[] for _ in ATTN_GROUPS]
    ahead = None
    for i in range(depth):
        li = i // 2
        last = i == depth - 1
        if i % 2 == 0:
            xp, sp = _pool_prompt(xp, norm_mix[i], w_pool_b[li], pool_scale[li], batch, seq)
            xs, ss = _pool_sample(xs, state_t, li, norm_mix[i], w_pool_b[li], pool_scale[li], n, dec_seq,
                                  past_len)
            pool_p.append(sp)
            pool_s.append(ss)
        else:
            qps, kvps, kvts = _qkv_prompt(xp, norm_mix[i], w_qkv_b[li], w_kv_t[li], batch, seq)
            os, lses = zip(*[_attn_prompt(qps[g], kvps[g], prompt_bias[g], prompt_cap) for g in range(N_GROUPS)])
            xp = _wo_merge(os, lses, w_o_b[li], xp, batch, seq)
            assert ahead is not None, "every attention layer must follow a pooling layer"
            o_s, kvfs = ahead
            ahead = None
            xs = _wo(o_s, w_o_b[li], xs)
            kvfs = kvfs.reshape(n, dec_seq, N_GROUPS, 2, HEADS, HEAD_DIM)
            for g in range(N_GROUPS):
                kv_p[g].append(kvts[g])
                kv_s[g].append(kvfs[:, :, g])
        xs = _mlp(xs, norm_ffn[i], w_up_b, w_down_b, i, norm_final, last)
        if i % 2 == 0 and i + 1 < depth:
            nli = (i + 1) // 2
            qs, kvbs, kvfs = _qkv(xs, norm_mix[i + 1], w_qkv_b[nli])
            xp, o_s = _mlp_with_sample_attn(xp, norm_ffn[i], w_up_b, w_down_b, i, qs, kvbs, caches, nli,
                                            sample_tables, n, dec_seq)
            ahead = (o_s, kvfs)
        else:
            xp = _mlp(xp, norm_ffn[i], w_up_b, w_down_b, i, norm_final, last)
    return (xp.reshape(batch, seq, D_MODEL), xs.reshape(n, dec_seq, D_MODEL),
            jnp.stack(pool_p), jnp.transpose(jnp.stack(pool_s), (0, 2, 1, 3)),
            kv_rows_major(kv_p[0]), jnp.stack(kv_s[0]),
            kv_rows_major(kv_p[1]), jnp.stack(kv_s[1]),
            kv_rows_major(kv_p[2]), jnp.stack(kv_s[2]))
```

```python
import functools

import numpy as np
import jax
import jax.numpy as jnp
from jax import lax
from jax.experimental import pallas as pl
from jax.experimental.pallas import tpu as pltpu

F32 = jnp.float32
BF16 = jnp.bfloat16

D_MODEL = 1024
D_FF = 4 * D_MODEL
POOL_WINDOWS = (2, 4, 8, 16)
POOL_CH = D_MODEL // len(POOL_WINDOWS)
POOL_STATE = max(POOL_WINDOWS) - 1
ATTN_GROUPS = ((128, 1), (512, 4), (2048, 16))
N_GROUPS = len(ATTN_GROUPS)
HEADS = 8
HEAD_DIM = 64
ATTN_OUT = HEADS * HEAD_DIM
BAND = 128
N_BUCKETS = 32
MAX_EXACT = N_BUCKETS // 2
REL_MAX_DIST = 2048
PAST_LEN = 2048
RMS_EPS = 1e-6
NEG_INF = -1e30

VMEM_LIMIT_BYTES = 56 * 1024 * 1024
FUSED_VMEM_LIMIT_BYTES = 62 * 1024 * 1024
ROW_TILE = 512
FF_CHUNK = 1024
FF_CHUNKS = D_FF // FF_CHUNK
_NT = (((1,), (1,)), ((), ()))
LANES = 128


def _cparams(*sem):
    return pltpu.CompilerParams(dimension_semantics=sem, vmem_limit_bytes=VMEM_LIMIT_BYTES)


def _rms(x, g):
    ms = jnp.mean(x * x, axis=-1, keepdims=True)
    return x * lax.rsqrt(ms + RMS_EPS) * g


def _t5_bucket(dist):
    n = np.maximum(np.asarray(dist), 0)
    large = MAX_EXACT + (np.log(np.maximum(n, 1) / MAX_EXACT) / np.log(REL_MAX_DIST / MAX_EXACT)
                         * (N_BUCKETS - MAX_EXACT)).astype(np.int64)
    large = np.minimum(large, N_BUCKETS - 1)
    return np.where(n < MAX_EXACT, n, large).astype(np.int32)


def _const_spec(shape):
    nd = len(shape)
    return pl.BlockSpec(shape, lambda *_: (0,) * nd, pipeline_mode=pl.Buffered(1))


def _layer_spec(shape, layer):
    nd = len(shape)
    return pl.BlockSpec((None,) + tuple(shape), lambda *_: (layer,) + (0,) * nd, pipeline_mode=pl.Buffered(1))


def _mlp_kernel(x_ref, g_ref, wu_ref, wd_ref, gf_ref, o_ref, *, final_norm):
    x = x_ref[...]
    h = _rms(x, g_ref[...]).astype(BF16)
    acc = x
    for c in range(FF_CHUNKS):
        u = jnp.dot(h, wu_ref[:, c * FF_CHUNK:(c + 1) * FF_CHUNK], preferred_element_type=F32)
        a = jnp.square(jnp.maximum(u, 0.0)).astype(BF16)
        acc = acc + jnp.dot(a, wd_ref[c], preferred_element_type=F32)
    if final_norm:
        acc = _rms(acc, gf_ref[...])
    o_ref[...] = acc


def _mlp(x, g, w_up, w_down, layer, g_final, final_norm):
    m = x.shape[0]
    tm = min(ROW_TILE, m)
    return pl.pallas_call(
        functools.partial(_mlp_kernel, final_norm=final_norm),
        grid=(m // tm,),
        in_specs=[
            pl.BlockSpec((tm, D_MODEL), lambda i: (i, 0)),
            _const_spec((1, D_MODEL)),
            _layer_spec((D_MODEL, D_FF), layer),
            _layer_spec((FF_CHUNKS, FF_CHUNK, D_MODEL), layer),
            _const_spec((1, D_MODEL)),
        ],
        out_specs=pl.BlockSpec((tm, D_MODEL), lambda i: (i, 0)),
        out_shape=jax.ShapeDtypeStruct((m, D_MODEL), F32),
        compiler_params=_cparams("parallel"),
        name="mlp",
    )(x, g.reshape(1, D_MODEL), w_up, w_down, g_final.reshape(1, D_MODEL))


Q_COLS = N_GROUPS * ATTN_OUT
KV_COLS = 2 * N_GROUPS * ATTN_OUT


def _w_cols(which, g):
    start = (which * N_GROUPS + g) * ATTN_OUT
    return slice(start, start + ATTN_OUT)


def _qkv_kernel(x_ref, g_ref, w_ref, q_ref, kvb_ref, kvf_ref):
    h = _rms(x_ref[...], g_ref[...]).astype(BF16)
    q = jnp.dot(h, w_ref[:, :Q_COLS], preferred_element_type=F32)
    q_ref[...] = (q * (HEAD_DIM ** -0.5)).astype(BF16)
    for g in range(N_GROUPS):
        for which in (1, 2):
            out = slice((2 * g + which - 1) * ATTN_OUT, (2 * g + which) * ATTN_OUT)
            kv = jnp.dot(h, w_ref[:, _w_cols(which, g)], preferred_element_type=F32)
            kvf_ref[:, out] = kv
            kvb_ref[:, out] = kv.astype(BF16)


def _qkv(x, g, w):
    m = x.shape[0]
    tm = min(ROW_TILE, m)
    row = lambda i: (i, 0)
    return pl.pallas_call(
        _qkv_kernel,
        grid=(m // tm,),
        in_specs=[
            pl.BlockSpec((tm, D_MODEL), row),
            _const_spec((1, D_MODEL)),
            _const_spec((D_MODEL, Q_COLS + KV_COLS)),
        ],
        out_specs=[
            pl.BlockSpec((tm, Q_COLS), row),
            pl.BlockSpec((tm, KV_COLS), row),
            pl.BlockSpec((tm, KV_COLS), row),
        ],
        out_shape=[
            jax.ShapeDtypeStruct((m, Q_COLS), BF16),
            jax.ShapeDtypeStruct((m, KV_COLS), BF16),
            jax.ShapeDtypeStruct((m, KV_COLS), F32),
        ],
        compiler_params=_cparams("parallel"),
        name="qkv",
    )(x, g.reshape(1, D_MODEL), w)


def _qkv_prompt_kernel(x_ref, g_ref, w_ref, wt_ref, *rest, tm, n_tiles, wins, n_prev):
    rest = rest[n_prev:]
    q_refs, kv_refs, kvt_refs, scr_ref = rest[0:3], rest[3:6], rest[6:9], rest[9]
    i = pl.program_id(1)
    hf = _rms(x_ref[0], g_ref[...])
    h = hf.astype(BF16)

    h_by_dil = {1: h}
    dils = sorted({dil for _, dil in ATTN_GROUPS if dil > 1})
    chunks = D_MODEL // LANES
    if dils:
        for c in range(chunks):
            scr_ref[c] = hf[:, c * LANES:(c + 1) * LANES]
    for dil in dils:
        classes = [jnp.concatenate([scr_ref[c, pl.ds(r, tm // dil, stride=dil), :] for c in range(chunks)], axis=1)
                   for r in range(dil)]
        h_by_dil[dil] = jnp.concatenate(classes, axis=0).astype(BF16)

    for g, (_, dil) in enumerate(ATTN_GROUPS):
        hg = h_by_dil[dil]
        q = jnp.dot(hg, w_ref[:, _w_cols(0, g)], preferred_element_type=F32) * (HEAD_DIM ** -0.5)
        q_refs[g][0] = q.reshape(dil, tm // dil, ATTN_OUT).astype(BF16)
        for which in (1, 2):
            kv = jnp.dot(hg, w_ref[:, _w_cols(which, g)], preferred_element_type=F32)
            kv_refs[g][0, :, :, (which - 1) * ATTN_OUT:which * ATTN_OUT] = (
                kv.reshape(dil, tm // dil, ATTN_OUT).astype(BF16))

    for g, win in enumerate(wins):
        rows = min(win, tm)
        first_tile = n_tiles - max(win // tm, 1)

        @pl.when(i >= first_tile)
        def _(g=g, rows=rows):
            for which in (1, 2):
                start = ((which - 1) * N_GROUPS + g) * ATTN_OUT
                kvt = lax.dot_general(wt_ref[start:start + ATTN_OUT, :], h[tm - rows:, :], _NT,
                                      preferred_element_type=F32)
                kvt_refs[g][0, which - 1] = kvt.reshape(HEADS, HEAD_DIM, rows)


def _qkv_prompt(x, g, w, w_t, batch, seq, li, n_layers, kvt_prev):
    tm = min(ROW_TILE, seq)
    n_tiles = seq // tm
    wins = tuple(min(win, seq) for win, _ in ATTN_GROUPS)
    assert all(w_ % tm == 0 or tm % w_ == 0 for w_ in wins)
    split = lambda width: [
        (pl.BlockSpec((1, dil, tm // dil, width), lambda b, i: (b, 0, i, 0)),
         jax.ShapeDtypeStruct((batch, dil, seq // dil, width), BF16)) for _, dil in ATTN_GROUPS]
    kvt = [(pl.BlockSpec((None, 1, 2, HEADS, HEAD_DIM, min(win, tm)),
                         lambda b, i, first=n_tiles - max(win // tm, 1): (li, b, 0, 0, 0, jnp.maximum(i - first, 0))),
            jax.ShapeDtypeStruct((n_layers, batch, 2, HEADS, HEAD_DIM, win), F32)) for win in wins]
    outs = split(ATTN_OUT) + split(2 * ATTN_OUT) + kvt
    prev = [] if kvt_prev is None else list(kvt_prev)
    n_in = 4
    res = pl.pallas_call(
        functools.partial(_qkv_prompt_kernel, tm=tm, n_tiles=n_tiles, wins=wins, n_prev=len(prev)),
        grid=(batch, n_tiles),
        in_specs=[
            pl.BlockSpec((1, tm, D_MODEL), lambda b, i: (b, i, 0)),
            _const_spec((1, D_MODEL)),
            _const_spec((D_MODEL, Q_COLS + KV_COLS)),
            _const_spec((KV_COLS, D_MODEL)),
        ] + [pl.BlockSpec(memory_space=pl.ANY)] * len(prev),
        out_specs=[o[0] for o in outs],
        out_shape=[o[1] for o in outs],
        input_output_aliases={n_in + k: 2 * N_GROUPS + k for k in range(len(prev))},
        scratch_shapes=[pltpu.VMEM((D_MODEL // LANES, tm, LANES), F32)],
        compiler_params=_cparams("parallel", "arbitrary"),
        name="qkv_prompt",
    )(x.reshape(batch, seq, D_MODEL), g.reshape(1, D_MODEL), w, w_t, *prev)
    return res[0:3], res[3:6], res[6:9]


def _attn_prompt_kernel(q_ref, kvp_ref, kvc_ref, bias_ref, cap_ref, o_ref, st_ref, s_scr, p_scr, *,
                        classes, blocks):
    n = pl.program_id(2)
    lane = lax.broadcasted_iota(jnp.int32, (BAND, LANES), 1)
    low_half = lane < HEAD_DIM
    half_sel = [low_half.astype(F32).astype(BF16), (~low_half).astype(F32).astype(BF16)]

    for cls, blk in [(c, b) for c in range(classes) for b in range(blocks)]:
        rows = slice(blk * BAND, (blk + 1) * BAND)
        if blk == 0:
            first = jnp.where(n > 0, 0, 1)
            prev = lambda cols, cls=cls: kvp_ref[0, cls, :, cols]
        else:
            first = 0
            prev = lambda cols, cls=cls, blk=blk: kvc_ref[0, cls, (blk - 1) * BAND:blk * BAND, cols]
        keys = lambda cols, prev=prev, cls=cls, rows=rows: jnp.concatenate(
            [prev(cols), kvc_ref[0, cls, rows, cols]], axis=0)

        for pair in range(HEADS // 2):
            cols = slice(pair * LANES, (pair + 1) * LANES)
            q2 = q_ref[0, cls, rows, cols]
            k2 = keys(cols)
            for half in range(2):
                s_scr[2 * pair + half] = lax.dot_general(q2 * half_sel[half], k2, _NT,
                                                         preferred_element_type=F32)

        stats = jnp.zeros((BAND, LANES), F32)
        for h in range(HEADS):
            t = jnp.minimum(s_scr[h] + bias_ref[h], cap_ref[first])
            mx = jnp.max(t, axis=-1, keepdims=True)
            e = jnp.exp(t - mx)
            den = jnp.sum(e, axis=-1, keepdims=True)
            p_scr[h] = e.astype(BF16)
            stats = jnp.where(lane == h, mx, jnp.where(lane == HEADS + h, den, stats))
        st_ref[0, cls, rows, :] = stats

        for pair in range(HEADS // 2):
            cols = slice(pair * LANES, (pair + 1) * LANES)
            v2 = keys(slice(ATTN_OUT + cols.start, ATTN_OUT + cols.stop))
            o_lo = jnp.dot(p_scr[2 * pair], v2, preferred_element_type=F32)
            o_hi = jnp.dot(p_scr[2 * pair + 1], v2, preferred_element_type=F32)
            o_ref[0, cls, rows, cols] = jnp.where(low_half, o_lo, o_hi)


BLOCKS_PER_STEP = 4


def _attn_prompt(q, kv, bias, cap):
    batch, dil, sub, _ = q.shape
    blocks = min(BLOCKS_PER_STEP, sub // BAND)
    classes = min(BLOCKS_PER_STEP // blocks, dil)
    step = blocks * BAND
    assert sub % step == 0 and dil % classes == 0
    cur = lambda b, r, n: (b, r, n, 0)
    return pl.pallas_call(
        functools.partial(_attn_prompt_kernel, classes=classes, blocks=blocks),
        grid=(batch, dil // classes, sub // step),
        in_specs=[
            pl.BlockSpec((1, classes, step, ATTN_OUT), cur),
            pl.BlockSpec((1, classes, BAND, 2 * ATTN_OUT),
                         lambda b, r, n: (b, r, jnp.maximum(n * blocks - 1, 0), 0)),
            pl.BlockSpec((1, classes, step, 2 * ATTN_OUT), cur),
            _const_spec((HEADS, BAND, 2 * BAND)),
            _const_spec((2, BAND, 2 * BAND)),
        ],
        out_specs=[
            pl.BlockSpec((1, classes, step, ATTN_OUT), cur),
            pl.BlockSpec((1, classes, step, LANES), cur),
        ],
        out_shape=[
            jax.ShapeDtypeStruct((batch, dil, sub, ATTN_OUT), F32),
            jax.ShapeDtypeStruct((batch, dil, sub, LANES), F32),
        ],
        scratch_shapes=[pltpu.VMEM((HEADS, BAND, 2 * BAND), F32), pltpu.VMEM((HEADS, BAND, 2 * BAND), BF16)],
        compiler_params=_cparams("parallel", "parallel", "arbitrary"),
        name="attn_prompt_d%d" % dil,
    )(q, kv, kv, bias, cap)


def _prompt_bias(rel_bias_g, dil):
    tab = rel_bias_g.astype(F32)[_t5_bucket(np.arange(BAND, -1, -1) * dil)]
    ext = jnp.pad(tab.T, ((0, 0), (BAND, BAND)))
    tiled = jnp.tile(ext, (1, BAND))[:, :BAND * 3 * BAND].reshape(HEADS, BAND, 3 * BAND)
    return tiled[:, :, BAND:]


def _prompt_cap():
    m = BAND + np.arange(BAND)[:, None] - np.arange(2 * BAND)[None, :]
    band = (m >= 0) & (m <= BAND)
    no_prev = band & (np.arange(2 * BAND) >= BAND)[None, :]
    big = np.finfo(np.float32).max
    return jnp.asarray(np.where(np.stack([band, no_prev]), big, NEG_INF).astype(np.float32))


def _wo_merge_kernel(o0_ref, o1_ref, o2_ref, l0_ref, l1_ref, l2_ref, e_ref, w_ref, x_ref, out_ref,
                     oscr_ref, lscr_ref, *, tm):
    def token_order(ref, scr_ref, dil):
        if dil == 1:
            return ref[0, 0]
        chunks = ref.shape[-1] // LANES
        for r in range(dil):
            for c in range(chunks):
                scr_ref[c, pl.ds(r, tm // dil, stride=dil), :] = ref[0, r, :, c * LANES:(c + 1) * LANES]
        return jnp.concatenate([scr_ref[c] for c in range(chunks)], axis=1)

    dils = [dil for _, dil in ATTN_GROUPS]
    sts = [token_order(l_ref, lscr_ref.at[g:g + 1], dils[g]) for g, l_ref in enumerate((l0_ref, l1_ref, l2_ref))]
    mxs = [st[:, :HEADS] for st in sts]
    dens = [st[:, HEADS:2 * HEADS] for st in sts]
    mx = jnp.maximum(jnp.maximum(mxs[0], mxs[1]), mxs[2])
    es = [jnp.exp(m - mx) for m in mxs]
    inv = 1.0 / (es[0] * dens[0] + es[1] * dens[1] + es[2] * dens[2])
    acc = None
    for g, o_ref in enumerate((o0_ref, o1_ref, o2_ref)):
        wt = es[g] * inv
        hi = wt.astype(BF16)
        lo = (wt - hi.astype(F32)).astype(BF16)
        wexp = (jnp.dot(hi, e_ref[...], preferred_element_type=F32)
                + jnp.dot(lo, e_ref[...], preferred_element_type=F32))
        term = wexp * token_order(o_ref, oscr_ref, dils[g])
        acc = term if acc is None else acc + term
    out_ref[0] = x_ref[0] + jnp.dot(acc.astype(BF16), w_ref[...], preferred_element_type=F32)


def _wo_merge(os, lses, w_o, x, batch, seq):
    tm = min(ROW_TILE, seq)
    expand = jnp.asarray(np.repeat(np.eye(HEADS, dtype=np.float32), HEAD_DIM, axis=1), BF16)
    split = lambda width: [pl.BlockSpec((1, dil, tm // dil, width), lambda b, i: (b, 0, i, 0))
                           for _, dil in ATTN_GROUPS]
    row = pl.BlockSpec((1, tm, D_MODEL), lambda b, i: (b, i, 0))
    out = pl.pallas_call(
        functools.partial(_wo_merge_kernel, tm=tm),
        grid=(batch, seq // tm),
        in_specs=split(ATTN_OUT) + split(LANES) + [
            _const_spec((HEADS, ATTN_OUT)),
            _const_spec((ATTN_OUT, D_MODEL)),
            row,
        ],
        out_specs=row,
        out_shape=jax.ShapeDtypeStruct((batch, seq, D_MODEL), F32),
        scratch_shapes=[pltpu.VMEM((ATTN_OUT // LANES, tm, LANES), F32), pltpu.VMEM((N_GROUPS, tm, LANES), F32)],
        compiler_params=_cparams("parallel", "parallel"),
        name="wo_merge",
    )(*os, *lses, expand, w_o, x.reshape(batch, seq, D_MODEL))
    return out.reshape(batch * seq, D_MODEL)


def _wo_kernel(o_ref, w_ref, x_ref, out_ref):
    out_ref[...] = x_ref[...] + jnp.dot(o_ref[...].astype(BF16), w_ref[...], preferred_element_type=F32)


def _wo(o, w_o, x):
    m = x.shape[0]
    tm = min(ROW_TILE, m)
    row = lambda i: (i, 0)
    return pl.pallas_call(
        _wo_kernel,
        grid=(m // tm,),
        in_specs=[pl.BlockSpec((tm, ATTN_OUT), row), _const_spec((ATTN_OUT, D_MODEL)),
                  pl.BlockSpec((tm, D_MODEL), row)],
        out_specs=pl.BlockSpec((tm, D_MODEL), row),
        out_shape=jax.ShapeDtypeStruct((m, D_MODEL), F32),
        compiler_params=_cparams("parallel"),
        name="wo",
    )(o, w_o, x)


Q_PAD = 8
NEW_PAD = 16


def _sample_tables(rel_bias, dec_seq, buf_rows):
    tq = np.arange(Q_PAD)
    tables = []
    for g, (_, dil) in enumerate(ATTN_GROUPS):
        wb = buf_rows[g]
        by_dist = rel_bias[:, g].astype(F32)[_t5_bucket(np.arange(wb + Q_PAD))]
        rev = by_dist[::-1].T
        for kpos, real in ((np.arange(wb), np.ones(wb, bool)),
                           (wb + np.arange(NEW_PAD), np.arange(NEW_PAD) < dec_seq)):
            delta = (wb + tq)[:, None] - kpos[None, :]
            ok = (delta >= 0) & (delta % dil == 0) & (delta // dil <= BAND)
            ok &= (tq < dec_seq)[:, None] & real[None, :]
            if kpos.shape[0] == wb:
                bias = jnp.stack([rev[:, Q_PAD - 1 - t:Q_PAD - 1 - t + wb] for t in range(Q_PAD)], axis=1)
            else:
                bias = jnp.transpose(by_dist[np.maximum(delta, 0)], (2, 0, 1))
            tables.append(bias)
            tables.append(jnp.asarray(ok.astype(np.float32)))
    return tables


def _attn_sample_body(q_ref, kvn_ref, c_refs, tables, o_ref, s_scr):
    rows = [c.shape[-1] for c in c_refs]
    offs = [sum(rows[:g]) for g in range(len(rows))]
    head_rows = lambda h: slice(h * Q_PAD, (h + 1) * Q_PAD)
    new_scores = []
    for g, c_ref in enumerate(c_refs):
        bias_ref, valid_ref, nbias_ref, nvalid_ref = tables[4 * g:4 * g + 4]
        per_head = []
        for h in range(HEADS):
            q = q_ref[0, g, h]
            s = jnp.dot(q, c_ref[0, h].astype(BF16), preferred_element_type=F32)
            s_scr[head_rows(h), offs[g]:offs[g] + rows[g]] = jnp.where(valid_ref[...] > 0.0, s + bias_ref[h], NEG_INF)
            s = lax.dot_general(q, kvn_ref[0, g, 0, h], _NT, preferred_element_type=F32)
            per_head.append(jnp.where(nvalid_ref[...] > 0.0, s + nbias_ref[h], NEG_INF))
        new_scores.append(jnp.concatenate(per_head, axis=0))

    s = s_scr[...]
    mx = jnp.max(s, axis=-1, keepdims=True)
    for sn in new_scores:
        mx = jnp.maximum(mx, jnp.max(sn, axis=-1, keepdims=True))
    e = jnp.exp(s - mx)
    den = jnp.sum(e, axis=-1, keepdims=True)
    s_scr[...] = e
    new_probs = []
    for sn in new_scores:
        en = jnp.exp(sn - mx)
        den = den + jnp.sum(en, axis=-1, keepdims=True)
        new_probs.append(en)
    inv = 1.0 / den

    for h in range(HEADS):
        acc = jnp.zeros((Q_PAD, HEAD_DIM), F32)
        for g, c_ref in enumerate(c_refs):
            p = s_scr[head_rows(h), offs[g]:offs[g] + rows[g]].astype(BF16)
            acc = acc + lax.dot_general(p, c_ref[1, h].astype(BF16), _NT, preferred_element_type=F32)
            acc = acc + jnp.dot(new_probs[g][head_rows(h), :].astype(BF16), kvn_ref[0, g, 1, h],
                                preferred_element_type=F32)
        o_ref[0, h] = acc * inv[head_rows(h), :]


def _mlp_attn_kernel(x_ref, g_ref, wu_ref, wd_ref, q_ref, kvn_ref, c0_ref, c1_ref, c2_ref, *rest):
    tables, out_ref, o_ref, acc_ref, h_ref, s_scr = rest[:-5], rest[-5], rest[-4], rest[-3], rest[-2], rest[-1]
    j = pl.program_id(1)

    @pl.when(j == 0)
    def _():
        acc_ref[...] = x_ref[...]
        h_ref[...] = _rms(x_ref[...], g_ref[...]).astype(BF16)

    h = h_ref[...]
    u = jnp.dot(h, wu_ref[:, pl.ds(pl.multiple_of(j * FF_CHUNK, FF_CHUNK), FF_CHUNK)], preferred_element_type=F32)
    a = jnp.square(jnp.maximum(u, 0.0)).astype(BF16)
    acc = acc_ref[...] + jnp.dot(a, wd_ref[j], preferred_element_type=F32)
    acc_ref[...] = acc
    _attn_sample_body(q_ref, kvn_ref, (c0_ref, c1_ref, c2_ref), tables, o_ref, s_scr)
    out_ref[...] = acc_ref[...]


def _sample_qkv_by_head(q, kvb, n, dec_seq):
    q5 = q.reshape(n, dec_seq, N_GROUPS, HEADS, HEAD_DIM).transpose(0, 2, 3, 1, 4)
    q5 = jnp.pad(q5, ((0, 0), (0, 0), (0, 0), (0, Q_PAD - dec_seq), (0, 0)))
    kvn = kvb.reshape(n, dec_seq, N_GROUPS, 2, HEADS, HEAD_DIM).transpose(0, 2, 3, 4, 1, 5)
    kvn = jnp.pad(kvn, ((0, 0), (0, 0), (0, 0), (0, 0), (0, NEW_PAD - dec_seq), (0, 0)))
    return q5, kvn


def _mlp_with_sample_attn(x, g, w_up, w_down, layer, q, kvb, caches, li, tables, n, dec_seq):
    m = x.shape[0]
    tm = min(ROW_TILE, m)
    assert n % (m // tm) == 0 and n // (m // tm) == FF_CHUNKS
    q5, kvn = _sample_qkv_by_head(q, kvb, n, dec_seq)
    seq = lambda i, j: i * FF_CHUNKS + j
    cache_spec = lambda c: pl.BlockSpec((None, None) + c.shape[2:], lambda i, j: (li, seq(i, j), 0, 0, 0, 0))
    row = pl.BlockSpec((tm, D_MODEL), lambda i, j: (i, 0))
    out, o = pl.pallas_call(
        _mlp_attn_kernel,
        grid=(m // tm, FF_CHUNKS),
        in_specs=[
            row,
            _const_spec((1, D_MODEL)),
            _layer_spec((D_MODEL, D_FF), layer),
            _layer_spec((FF_CHUNKS, FF_CHUNK, D_MODEL), layer),
            pl.BlockSpec((1, N_GROUPS, HEADS, Q_PAD, HEAD_DIM), lambda i, j: (seq(i, j), 0, 0, 0, 0)),
            pl.BlockSpec((1, N_GROUPS, 2, HEADS, NEW_PAD, HEAD_DIM), lambda i, j: (seq(i, j), 0, 0, 0, 0, 0)),
        ] + [cache_spec(c) for c in caches] + [_const_spec(t.shape) for t in tables],
        out_specs=[row, pl.BlockSpec((1, HEADS, Q_PAD, HEAD_DIM), lambda i, j: (seq(i, j), 0, 0, 0))],
        out_shape=[jax.ShapeDtypeStruct((m, D_MODEL), F32),
                   jax.ShapeDtypeStruct((n, HEADS, Q_PAD, HEAD_DIM), F32)],
        scratch_shapes=[pltpu.VMEM((tm, D_MODEL), F32), pltpu.VMEM((tm, D_MODEL), BF16),
                        pltpu.VMEM((HEADS * Q_PAD, sum(c.shape[-1] for c in caches)), F32)],
        compiler_params=pltpu.CompilerParams(dimension_semantics=("parallel", "arbitrary"),
                                             vmem_limit_bytes=FUSED_VMEM_LIMIT_BYTES),
        name="mlp_attn",
    )(x, g.reshape(1, D_MODEL), w_up, w_down, q5, kvn, *caches, *tables)
    return out, o[:, :, :dec_seq].transpose(0, 2, 1, 3).reshape(n * dec_seq, ATTN_OUT)


HALO = 16


def _pool_prompt_kernel(x_ref, g_ref, w_ref, sc_ref, o_ref, st_ref, hp_ref, *, tt):
    i = pl.program_id(1)

    @pl.when(i == 0)
    def _():
        hp_ref[0:HALO, :] = jnp.zeros((HALO, D_MODEL), F32)

    x = x_ref[0]
    h = _rms(x, g_ref[...])
    hp_ref[HALO:HALO + tt, :] = h
    pos1 = (i * tt + 1 + lax.broadcasted_iota(jnp.int32, (tt, 1), 0)).astype(F32)
    ys = []
    for g, w in enumerate(POOL_WINDOWS):
        cols = slice(g * POOL_CH, (g + 1) * POOL_CH)
        hg = h[:, cols]
        win = hg
        for j in range(1, w):
            win = win + hp_ref[HALO - j:HALO - j + tt, cols]
        p = win / jnp.minimum(float(w), pos1) - hg
        ys.append(jnp.dot(p.astype(BF16), w_ref[g], preferred_element_type=F32))
    o_ref[0] = x + jnp.concatenate(ys, axis=1) * sc_ref[...]
    tail = hp_ref[tt:tt + HALO, :]
    hp_ref[0:HALO, :] = tail

    @pl.when(i == pl.num_programs(1) - 1)
    def _():
        st_ref[0] = tail


def _pool_prompt(x, g, w_pool, scale, batch, seq):
    tt = min(ROW_TILE, seq)
    out, st = pl.pallas_call(
        functools.partial(_pool_prompt_kernel, tt=tt),
        grid=(batch, seq // tt),
        in_specs=[
            pl.BlockSpec((1, tt, D_MODEL), lambda b, i: (b, i, 0)),
            _const_spec((1, D_MODEL)),
            _const_spec(w_pool.shape),
            _const_spec((1, D_MODEL)),
        ],
        out_specs=[
            pl.BlockSpec((1, tt, D_MODEL), lambda b, i: (b, i, 0)),
            pl.BlockSpec((1, HALO, D_MODEL), lambda b, i: (b, 0, 0)),
        ],
        out_shape=[
            jax.ShapeDtypeStruct((batch, seq, D_MODEL), F32),
            jax.ShapeDtypeStruct((batch, HALO, D_MODEL), F32),
        ],
        scratch_shapes=[pltpu.VMEM((HALO + tt, D_MODEL), F32)],
        compiler_params=_cparams("parallel", "arbitrary"),
        name="pool_prompt",
    )(x.reshape(batch, seq, D_MODEL), g.reshape(1, D_MODEL), w_pool, scale.reshape(1, D_MODEL))
    return out.reshape(batch * seq, D_MODEL), st[:, HALO - POOL_STATE:, :]


def _pool_sample_kernel(x_ref, st_ref, g_ref, w_ref, sc_ref, o_ref, nst_ref, *, dec_seq, past_len):
    chunk = lambda ref, k: ref[:, k * D_MODEL:(k + 1) * D_MODEL]
    xs = [chunk(x_ref, t) for t in range(dec_seq)]
    hs = [_rms(x, g_ref[...]) for x in xs]
    rows = [st_ref[k] for k in range(POOL_STATE)] + hs
    for t in range(dec_seq):
        ys = []
        for g, w in enumerate(POOL_WINDOWS):
            cols = slice(g * POOL_CH, (g + 1) * POOL_CH)
            win = rows[POOL_STATE + t][:, cols]
            for j in range(1, w):
                win = win + rows[POOL_STATE + t - j][:, cols]
            p = win / float(min(w, past_len + t + 1)) - hs[t][:, cols]
            ys.append(jnp.dot(p.astype(BF16), w_ref[g], preferred_element_type=F32))
        o_ref[:, t * D_MODEL:(t + 1) * D_MODEL] = xs[t] + jnp.concatenate(ys, axis=1) * sc_ref[...]
    new_rows = rows[-POOL_STATE:]
    for k in range(POOL_STATE):
        nst_ref[k] = new_rows[k]


def _pool_sample(x, state_t, li, g, w_pool, scale, n, dec_seq, past_len):
    bn = min(32, n)
    out, nst = pl.pallas_call(
        functools.partial(_pool_sample_kernel, dec_seq=dec_seq, past_len=past_len),
        grid=(n // bn,),
        in_specs=[
            pl.BlockSpec((bn, dec_seq * D_MODEL), lambda i: (i, 0)),
            pl.BlockSpec((None, POOL_STATE, bn, D_MODEL), lambda i: (li, 0, i, 0)),
            _const_spec((1, D_MODEL)),
            _const_spec(w_pool.shape),
            _const_spec((1, D_MODEL)),
        ],
        out_specs=[
            pl.BlockSpec((bn, dec_seq * D_MODEL), lambda i: (i, 0)),
            pl.BlockSpec((POOL_STATE, bn, D_MODEL), lambda i: (0, i, 0)),
        ],
        out_shape=[
            jax.ShapeDtypeStruct((n, dec_seq * D_MODEL), F32),
            jax.ShapeDtypeStruct((POOL_STATE, n, D_MODEL), F32),
        ],
        compiler_params=_cparams("parallel"),
        name="pool_sample",
    )(x.reshape(n, dec_seq * D_MODEL), state_t, g.reshape(1, D_MODEL), w_pool, scale.reshape(1, D_MODEL))
    return out.reshape(n * dec_seq, D_MODEL), nst


def kernel(x_prompt, x_sample, state_pool, cache_kv_w128, cache_kv_w512, cache_kv_w2048, rel_bias, norm_mix,
           norm_ffn, norm_final, w_pool, pool_scale, w_qkv, w_o, w_up, w_down):
    batch, seq, _ = x_prompt.shape
    n, dec_seq, _ = x_sample.shape
    depth = norm_mix.shape[0]
    caches = tuple(jnp.transpose(c, (0, 1, 3, 4, 5, 2)) for c in (cache_kv_w128, cache_kv_w512, cache_kv_w2048))
    state_t = jnp.transpose(state_pool, (0, 2, 1, 3))
    past_len = PAST_LEN

    xp = x_prompt.reshape(batch * seq, D_MODEL)
    xs = x_sample.reshape(n * dec_seq, D_MODEL)
    w_up_b = w_up.astype(BF16)
    w_down_b = w_down.astype(BF16).reshape(depth, FF_CHUNKS, FF_CHUNK, D_MODEL)
    w_pool_b = w_pool.astype(BF16)
    w_o_b = w_o.astype(BF16)
    w_qkv_b = w_qkv.astype(BF16)
    w_kv_t = jnp.transpose(w_qkv_b[:, :, Q_COLS:], (0, 2, 1))
    prompt_bias = [_prompt_bias(rel_bias[:, g], dil) for g, (_, dil) in enumerate(ATTN_GROUPS)]
    prompt_cap = _prompt_cap()
    sample_tables = _sample_tables(rel_bias, dec_seq, tuple(c.shape[-1] for c in caches))

    kv_rows_major = lambda kvt: jnp.transpose(kvt, (0, 1, 5, 2, 3, 4))

    pool_p, pool_s = [], []
    kvts = None
    kv_s = [[] for _ in ATTN_GROUPS]
    ahead = None
    for i in range(depth):
        li = i // 2
        last = i == depth - 1
        if i % 2 == 0:
            xp, sp = _pool_prompt(xp, norm_mix[i], w_pool_b[li], pool_scale[li], batch, seq)
            xs, ss = _pool_sample(xs, state_t, li, norm_mix[i], w_pool_b[li], pool_scale[li], n, dec_seq,
                                  past_len)
            pool_p.append(sp)
            pool_s.append(ss)
        else:
            qps, kvps, kvts = _qkv_prompt(xp, norm_mix[i], w_qkv_b[li], w_kv_t[li], batch, seq, li, depth // 2, kvts)
            os, lses = zip(*[_attn_prompt(qps[g], kvps[g], prompt_bias[g], prompt_cap) for g in range(N_GROUPS)])
            xp = _wo_merge(os, lses, w_o_b[li], xp, batch, seq)
            assert ahead is not None, "every attention layer must follow a pooling layer"
            o_s, kvfs = ahead
            ahead = None
            xs = _wo(o_s, w_o_b[li], xs)
            kvfs = kvfs.reshape(n, dec_seq, N_GROUPS, 2, HEADS, HEAD_DIM)
            for g in range(N_GROUPS):
                kv_s[g].append(kvfs[:, :, g])
        xs = _mlp(xs, norm_ffn[i], w_up_b, w_down_b, i, norm_final, last)
        if i % 2 == 0 and i + 1 < depth:
            nli = (i + 1) // 2
            qs, kvbs, kvfs = _qkv(xs, norm_mix[i + 1], w_qkv_b[nli])
            xp, o_s = _mlp_with_sample_attn(xp, norm_ffn[i], w_up_b, w_down_b, i, qs, kvbs, caches, nli,
                                            sample_tables, n, dec_seq)
            ahead = (o_s, kvfs)
        else:
            xp = _mlp(xp, norm_ffn[i], w_up_b, w_down_b, i, norm_final, last)
    return (xp.reshape(batch, seq, D_MODEL), xs.reshape(n, dec_seq, D_MODEL),
            jnp.stack(pool_p), jnp.transpose(jnp.stack(pool_s), (0, 2, 1, 3)),
            kv_rows_major(kvts[0]), jnp.stack(kv_s[0]),
            kv_rows_major(kvts[1]), jnp.stack(kv_s[1]),
            kv_rows_major(kvts[2]), jnp.stack(kv_s[2]))
```

```python
import functools

import numpy as np
import jax
import jax.numpy as jnp
from jax import lax
from jax.experimental import pallas as pl
from jax.experimental.pallas import tpu as pltpu

F32 = jnp.float32
BF16 = jnp.bfloat16

D_MODEL = 1024
D_FF = 4 * D_MODEL
POOL_WINDOWS = (2, 4, 8, 16)
POOL_CH = D_MODEL // len(POOL_WINDOWS)
POOL_STATE = max(POOL_WINDOWS) - 1
ATTN_GROUPS = ((128, 1), (512, 4), (2048, 16))
N_GROUPS = len(ATTN_GROUPS)
HEADS = 8
HEAD_DIM = 64
ATTN_OUT = HEADS * HEAD_DIM
BAND = 128
N_BUCKETS = 32
MAX_EXACT = N_BUCKETS // 2
REL_MAX_DIST = 2048
PAST_LEN = 2048
RMS_EPS = 1e-6
NEG_INF = -1e30

VMEM_LIMIT_BYTES = 56 * 1024 * 1024
FUSED_VMEM_LIMIT_BYTES = 62 * 1024 * 1024
ROW_TILE = 512
FF_CHUNK = 1024
FF_CHUNKS = D_FF // FF_CHUNK
_NT = (((1,), (1,)), ((), ()))
LANES = 128


def _cparams(*sem):
    return pltpu.CompilerParams(dimension_semantics=sem, vmem_limit_bytes=VMEM_LIMIT_BYTES)


def _rms(x, g):
    ms = jnp.mean(x * x, axis=-1, keepdims=True)
    return x * lax.rsqrt(ms + RMS_EPS) * g


def _t5_bucket(dist):
    n = np.maximum(np.asarray(dist), 0)
    large = MAX_EXACT + (np.log(np.maximum(n, 1) / MAX_EXACT) / np.log(REL_MAX_DIST / MAX_EXACT)
                         * (N_BUCKETS - MAX_EXACT)).astype(np.int64)
    large = np.minimum(large, N_BUCKETS - 1)
    return np.where(n < MAX_EXACT, n, large).astype(np.int32)


def _const_spec(shape):
    nd = len(shape)
    return pl.BlockSpec(shape, lambda *_: (0,) * nd, pipeline_mode=pl.Buffered(1))


def _layer_spec(shape, layer):
    nd = len(shape)
    return pl.BlockSpec((None,) + tuple(shape), lambda *_: (layer,) + (0,) * nd, pipeline_mode=pl.Buffered(1))


def _mlp_kernel(x_ref, g_ref, wu_ref, wd_ref, gf_ref, o_ref, *, final_norm):
    x = x_ref[...]
    h = _rms(x, g_ref[...]).astype(BF16)
    acc = x
    for c in range(FF_CHUNKS):
        u = jnp.dot(h, wu_ref[:, c * FF_CHUNK:(c + 1) * FF_CHUNK], preferred_element_type=F32)
        a = jnp.square(jnp.maximum(u, 0.0)).astype(BF16)
        acc = acc + jnp.dot(a, wd_ref[c], preferred_element_type=F32)
    if final_norm:
        acc = _rms(acc, gf_ref[...])
    o_ref[...] = acc


def _mlp(x, g, w_up, w_down, layer, g_final, final_norm):
    m = x.shape[0]
    tm = min(ROW_TILE, m)
    return pl.pallas_call(
        functools.partial(_mlp_kernel, final_norm=final_norm),
        grid=(m // tm,),
        in_specs=[
            pl.BlockSpec((tm, D_MODEL), lambda i: (i, 0)),
            _const_spec((1, D_MODEL)),
            _layer_spec((D_MODEL, D_FF), layer),
            _layer_spec((FF_CHUNKS, FF_CHUNK, D_MODEL), layer),
            _const_spec((1, D_MODEL)),
        ],
        out_specs=pl.BlockSpec((tm, D_MODEL), lambda i: (i, 0)),
        out_shape=jax.ShapeDtypeStruct((m, D_MODEL), F32),
        compiler_params=_cparams("parallel"),
        name="mlp",
    )(x, g.reshape(1, D_MODEL), w_up, w_down, g_final.reshape(1, D_MODEL))


Q_COLS = N_GROUPS * ATTN_OUT
KV_COLS = 2 * N_GROUPS * ATTN_OUT


def _w_cols(which, g):
    start = (which * N_GROUPS + g) * ATTN_OUT
    return slice(start, start + ATTN_OUT)


def _qkv_kernel(x_ref, g_ref, w_ref, q_ref, kvb_ref, kvf_ref):
    h = _rms(x_ref[...], g_ref[...]).astype(BF16)
    q = jnp.dot(h, w_ref[:, :Q_COLS], preferred_element_type=F32)
    q_ref[...] = (q * (HEAD_DIM ** -0.5)).astype(BF16)
    for g in range(N_GROUPS):
        for which in (1, 2):
            out = slice((2 * g + which - 1) * ATTN_OUT, (2 * g + which) * ATTN_OUT)
            kv = jnp.dot(h, w_ref[:, _w_cols(which, g)], preferred_element_type=F32)
            kvf_ref[:, out] = kv
            kvb_ref[:, out] = kv.astype(BF16)


def _qkv(x, g, w):
    m = x.shape[0]
    tm = min(ROW_TILE, m)
    row = lambda i: (i, 0)
    return pl.pallas_call(
        _qkv_kernel,
        grid=(m // tm,),
        in_specs=[
            pl.BlockSpec((tm, D_MODEL), row),
            _const_spec((1, D_MODEL)),
            _const_spec((D_MODEL, Q_COLS + KV_COLS)),
        ],
        out_specs=[
            pl.BlockSpec((tm, Q_COLS), row),
            pl.BlockSpec((tm, KV_COLS), row),
            pl.BlockSpec((tm, KV_COLS), row),
        ],
        out_shape=[
            jax.ShapeDtypeStruct((m, Q_COLS), BF16),
            jax.ShapeDtypeStruct((m, KV_COLS), BF16),
            jax.ShapeDtypeStruct((m, KV_COLS), F32),
        ],
        compiler_params=_cparams("parallel"),
        name="qkv",
    )(x, g.reshape(1, D_MODEL), w)


def _qkv_prompt_kernel(x_ref, g_ref, w_ref, wt_ref, *rest, tm, n_tiles, wins, n_prev):
    rest = rest[n_prev:]
    q_refs, kv_refs, kvt_refs, scr_ref = rest[0:3], rest[3:6], rest[6:9], rest[9]
    i = pl.program_id(1)
    hf = _rms(x_ref[0], g_ref[...])
    h = hf.astype(BF16)

    h_by_dil = {1: h}
    dils = sorted({dil for _, dil in ATTN_GROUPS if dil > 1})
    chunks = D_MODEL // LANES
    if dils:
        for c in range(chunks):
            scr_ref[c] = hf[:, c * LANES:(c + 1) * LANES]
    for dil in dils:
        classes = [jnp.concatenate([scr_ref[c, pl.ds(r, tm // dil, stride=dil), :] for c in range(chunks)], axis=1)
                   for r in range(dil)]
        h_by_dil[dil] = jnp.concatenate(classes, axis=0).astype(BF16)

    for g, (_, dil) in enumerate(ATTN_GROUPS):
        hg = h_by_dil[dil]
        q = jnp.dot(hg, w_ref[:, _w_cols(0, g)], preferred_element_type=F32) * (HEAD_DIM ** -0.5)
        q_refs[g][0] = q.reshape(dil, tm // dil, ATTN_OUT).astype(BF16)
        for which in (1, 2):
            kv = jnp.dot(hg, w_ref[:, _w_cols(which, g)], preferred_element_type=F32)
            kv_refs[g][0, :, :, (which - 1) * ATTN_OUT:which * ATTN_OUT] = (
                kv.reshape(dil, tm // dil, ATTN_OUT).astype(BF16))

    for g, win in enumerate(wins):
        rows = min(win, tm)
        first_tile = n_tiles - max(win // tm, 1)

        @pl.when(i >= first_tile)
        def _(g=g, rows=rows):
            for which in (1, 2):
                start = ((which - 1) * N_GROUPS + g) * ATTN_OUT
                kvt = lax.dot_general(wt_ref[start:start + ATTN_OUT, :], h[tm - rows:, :], _NT,
                                      preferred_element_type=F32)
                kvt_refs[g][0, which - 1] = kvt.reshape(HEADS, HEAD_DIM, rows)


def _qkv_prompt(x, g, w, w_t, batch, seq, li, n_layers, kvt_prev):
    tm = min(ROW_TILE, seq)
    n_tiles = seq // tm
    wins = tuple(min(win, seq) for win, _ in ATTN_GROUPS)
    assert all(w_ % tm == 0 or tm % w_ == 0 for w_ in wins)
    split = lambda width: [
        (pl.BlockSpec((1, dil, tm // dil, width), lambda b, i: (b, 0, i, 0)),
         jax.ShapeDtypeStruct((batch, dil, seq // dil, width), BF16)) for _, dil in ATTN_GROUPS]
    kvt = [(pl.BlockSpec((None, 1, 2, HEADS, HEAD_DIM, min(win, tm)),
                         lambda b, i, first=n_tiles - max(win // tm, 1): (li, b, 0, 0, 0, jnp.maximum(i - first, 0))),
            jax.ShapeDtypeStruct((n_layers, batch, 2, HEADS, HEAD_DIM, win), F32)) for win in wins]
    outs = split(ATTN_OUT) + split(2 * ATTN_OUT) + kvt
    prev = [] if kvt_prev is None else list(kvt_prev)
    n_in = 4
    res = pl.pallas_call(
        functools.partial(_qkv_prompt_kernel, tm=tm, n_tiles=n_tiles, wins=wins, n_prev=len(prev)),
        grid=(batch, n_tiles),
        in_specs=[
            pl.BlockSpec((1, tm, D_MODEL), lambda b, i: (b, i, 0)),
            _const_spec((1, D_MODEL)),
            _const_spec((D_MODEL, Q_COLS + KV_COLS)),
            _const_spec((KV_COLS, D_MODEL)),
        ] + [pl.BlockSpec(memory_space=pl.ANY)] * len(prev),
        out_specs=[o[0] for o in outs],
        out_shape=[o[1] for o in outs],
        input_output_aliases={n_in + k: 2 * N_GROUPS + k for k in range(len(prev))},
        scratch_shapes=[pltpu.VMEM((D_MODEL // LANES, tm, LANES), F32)],
        compiler_params=_cparams("parallel", "arbitrary"),
        name="qkv_prompt",
    )(x.reshape(batch, seq, D_MODEL), g.reshape(1, D_MODEL), w, w_t, *prev)
    return res[0:3], res[3:6], res[6:9]


def _attn_prompt_kernel(q_ref, kvp_ref, kvc_ref, bias_ref, cap_ref, o_ref, st_ref, s_scr, p_scr, *,
                        classes, blocks):
    n = pl.program_id(2)
    lane = lax.broadcasted_iota(jnp.int32, (BAND, LANES), 1)
    low_half = lane < HEAD_DIM
    half_sel = [low_half.astype(F32).astype(BF16), (~low_half).astype(F32).astype(BF16)]

    for cls, blk in [(c, b) for c in range(classes) for b in range(blocks)]:
        rows = slice(blk * BAND, (blk + 1) * BAND)
        if blk == 0:
            first = jnp.where(n > 0, 0, 1)
            prev = lambda cols, cls=cls: kvp_ref[0, cls, :, cols]
        else:
            first = 0
            prev = lambda cols, cls=cls, blk=blk: kvc_ref[0, cls, (blk - 1) * BAND:blk * BAND, cols]
        keys = lambda cols, prev=prev, cls=cls, rows=rows: jnp.concatenate(
            [prev(cols), kvc_ref[0, cls, rows, cols]], axis=0)

        for pair in range(HEADS // 2):
            cols = slice(pair * LANES, (pair + 1) * LANES)
            q2 = q_ref[0, cls, rows, cols]
            k2 = keys(cols)
            for half in range(2):
                s_scr[2 * pair + half] = lax.dot_general(q2 * half_sel[half], k2, _NT,
                                                         preferred_element_type=F32)

        stats = jnp.zeros((BAND, LANES), F32)
        for h in range(HEADS):
            t = jnp.minimum(s_scr[h] + bias_ref[h], cap_ref[first])
            mx = jnp.max(t, axis=-1, keepdims=True)
            e = jnp.exp(t - mx)
            den = jnp.sum(e, axis=-1, keepdims=True)
            p_scr[h] = e.astype(BF16)
            stats = jnp.where(lane == h, mx, jnp.where(lane == HEADS + h, den, stats))
        st_ref[0, cls, rows, :] = stats

        for pair in range(HEADS // 2):
            cols = slice(pair * LANES, (pair + 1) * LANES)
            v2 = keys(slice(ATTN_OUT + cols.start, ATTN_OUT + cols.stop))
            o_lo = jnp.dot(p_scr[2 * pair], v2, preferred_element_type=F32)
            o_hi = jnp.dot(p_scr[2 * pair + 1], v2, preferred_element_type=F32)
            o_ref[0, cls, rows, cols] = jnp.where(low_half, o_lo, o_hi)


BLOCKS_PER_STEP = 4

def _attn_prompt(q, kv, bias, cap):
    batch, dil, sub, _ = q.shape
    blocks = min(BLOCKS_PER_STEP, sub // BAND)
    classes = min(BLOCKS_PER_STEP // blocks, dil)
    step = blocks * BAND
    assert sub % step == 0 and dil % classes == 0
    cur = lambda b, r, n: (b, r, n, 0)
    return pl.pallas_call(
        functools.partial(_attn_prompt_kernel, classes=classes, blocks=blocks),
        grid=(batch, dil // classes, sub // step),
        in_specs=[
            pl.BlockSpec((1, classes, step, ATTN_OUT), cur),
            pl.BlockSpec((1, classes, BAND, 2 * ATTN_OUT),
                         lambda b, r, n: (b, r, jnp.maximum(n * blocks - 1, 0), 0)),
            pl.BlockSpec((1, classes, step, 2 * ATTN_OUT), cur),
            _const_spec((HEADS, BAND, 2 * BAND)),
            _const_spec((2, BAND, 2 * BAND)),
        ],
        out_specs=[
            pl.BlockSpec((1, classes, step, ATTN_OUT), cur),
            pl.BlockSpec((1, classes, step, LANES), cur),
        ],
        out_shape=[
            jax.ShapeDtypeStruct((batch, dil, sub, ATTN_OUT), F32),
            jax.ShapeDtypeStruct((batch, dil, sub, LANES), F32),
        ],
        scratch_shapes=[pltpu.VMEM((HEADS, BAND, 2 * BAND), F32), pltpu.VMEM((HEADS, BAND, 2 * BAND), BF16)],
        compiler_params=_cparams("parallel", "parallel", "arbitrary"),
        name="attn_prompt_d%d" % dil,
    )(q, kv, kv, bias, cap)


def _prompt_bias(rel_bias_g, dil):
    tab = rel_bias_g.astype(F32)[_t5_bucket(np.arange(BAND, -1, -1) * dil)]
    ext = jnp.pad(tab.T, ((0, 0), (BAND, BAND)))
    tiled = jnp.tile(ext, (1, BAND))[:, :BAND * 3 * BAND].reshape(HEADS, BAND, 3 * BAND)
    return tiled[:, :, BAND:]


def _prompt_cap():
    m = BAND + np.arange(BAND)[:, None] - np.arange(2 * BAND)[None, :]
    band = (m >= 0) & (m <= BAND)
    no_prev = band & (np.arange(2 * BAND) >= BAND)[None, :]
    big = np.finfo(np.float32).max
    return jnp.asarray(np.where(np.stack([band, no_prev]), big, NEG_INF).astype(np.float32))


def _wo_merge_kernel(o0_ref, o1_ref, o2_ref, l0_ref, l1_ref, l2_ref, e_ref, w_ref, x_ref, out_ref,
                     oscr_ref, lscr_ref, *, tm):
    def token_order(ref, scr_ref, dil):
        if dil == 1:
            return ref[0, 0]
        chunks = ref.shape[-1] // LANES
        for r in range(dil):
            for c in range(chunks):
                scr_ref[c, pl.ds(r, tm // dil, stride=dil), :] = ref[0, r, :, c * LANES:(c + 1) * LANES]
        return jnp.concatenate([scr_ref[c] for c in range(chunks)], axis=1)

    dils = [dil for _, dil in ATTN_GROUPS]
    sts = [token_order(l_ref, lscr_ref.at[g:g + 1], dils[g]) for g, l_ref in enumerate((l0_ref, l1_ref, l2_ref))]
    mxs = [st[:, :HEADS] for st in sts]
    dens = [st[:, HEADS:2 * HEADS] for st in sts]
    mx = jnp.maximum(jnp.maximum(mxs[0], mxs[1]), mxs[2])
    es = [jnp.exp(m - mx) for m in mxs]
    inv = 1.0 / (es[0] * dens[0] + es[1] * dens[1] + es[2] * dens[2])
    acc = None
    for g, o_ref in enumerate((o0_ref, o1_ref, o2_ref)):
        wt = es[g] * inv
        hi = wt.astype(BF16).astype(F32)
        parts = jnp.concatenate([hi, wt - hi], axis=1).astype(BF16)
        wexp = jnp.dot(parts, e_ref[...], preferred_element_type=F32)
        term = wexp * token_order(o_ref, oscr_ref, dils[g])
        acc = term if acc is None else acc + term
    out_ref[0] = x_ref[0] + jnp.dot(acc.astype(BF16), w_ref[...], preferred_element_type=F32)


def _wo_merge(os, lses, w_o, x, batch, seq):
    tm = min(ROW_TILE, seq)
    expand = jnp.asarray(np.tile(np.repeat(np.eye(HEADS, dtype=np.float32), HEAD_DIM, axis=1), (2, 1)), BF16)
    split = lambda width: [pl.BlockSpec((1, dil, tm // dil, width), lambda b, i: (b, 0, i, 0))
                           for _, dil in ATTN_GROUPS]
    row = pl.BlockSpec((1, tm, D_MODEL), lambda b, i: (b, i, 0))
    out = pl.pallas_call(
        functools.partial(_wo_merge_kernel, tm=tm),
        grid=(batch, seq // tm),
        in_specs=split(ATTN_OUT) + split(LANES) + [
            _const_spec((2 * HEADS, ATTN_OUT)),
            _const_spec((ATTN_OUT, D_MODEL)),
            row,
        ],
        out_specs=row,
        out_shape=jax.ShapeDtypeStruct((batch, seq, D_MODEL), F32),
        scratch_shapes=[pltpu.VMEM((ATTN_OUT // LANES, tm, LANES), F32), pltpu.VMEM((N_GROUPS, tm, LANES), F32)],
        compiler_params=_cparams("parallel", "parallel"),
        name="wo_merge",
    )(*os, *lses, expand, w_o, x.reshape(batch, seq, D_MODEL))
    return out.reshape(batch * seq, D_MODEL)


def _wo_kernel(o_ref, w_ref, x_ref, out_ref):
    out_ref[...] = x_ref[...] + jnp.dot(o_ref[...].astype(BF16), w_ref[...], preferred_element_type=F32)


def _wo(o, w_o, x):
    m = x.shape[0]
    tm = min(ROW_TILE, m)
    row = lambda i: (i, 0)
    return pl.pallas_call(
        _wo_kernel,
        grid=(m // tm,),
        in_specs=[pl.BlockSpec((tm, ATTN_OUT), row), _const_spec((ATTN_OUT, D_MODEL)),
                  pl.BlockSpec((tm, D_MODEL), row)],
        out_specs=pl.BlockSpec((tm, D_MODEL), row),
        out_shape=jax.ShapeDtypeStruct((m, D_MODEL), F32),
        compiler_params=_cparams("parallel"),
        name="wo",
    )(o, w_o, x)


Q_PAD = 8
NEW_PAD = 16


def _sample_tables(rel_bias, dec_seq, buf_rows):
    tq = np.arange(Q_PAD)
    tables = []
    for g, (_, dil) in enumerate(ATTN_GROUPS):
        wb = buf_rows[g]
        by_dist = rel_bias[:, g].astype(F32)[_t5_bucket(np.arange(wb + Q_PAD))]
        rev = by_dist[::-1].T
        for kpos, real in ((np.arange(wb), np.ones(wb, bool)),
                           (wb + np.arange(NEW_PAD), np.arange(NEW_PAD) < dec_seq)):
            delta = (wb + tq)[:, None] - kpos[None, :]
            ok = (delta >= 0) & (delta % dil == 0) & (delta // dil <= BAND)
            ok &= (tq < dec_seq)[:, None] & real[None, :]
            if kpos.shape[0] == wb:
                bias = jnp.stack([rev[:, Q_PAD - 1 - t:Q_PAD - 1 - t + wb] for t in range(Q_PAD)], axis=1)
            else:
                bias = jnp.transpose(by_dist[np.maximum(delta, 0)], (2, 0, 1))
            tables.append(bias)
            tables.append(jnp.asarray(ok.astype(np.float32)))
    return tables


def _attn_sample_body(q_ref, kvn_ref, c_refs, tables, o_ref, s_scr):
    rows = [c.shape[-1] for c in c_refs]
    offs = [sum(rows[:g]) for g in range(len(rows))]
    head_rows = lambda h: slice(h * Q_PAD, (h + 1) * Q_PAD)
    new_scores = []
    for g, c_ref in enumerate(c_refs):
        bias_ref, valid_ref, nbias_ref, nvalid_ref = tables[4 * g:4 * g + 4]
        per_head = []
        for h in range(HEADS):
            q = q_ref[0, g, h]
            s = jnp.dot(q, c_ref[0, h].astype(BF16), preferred_element_type=F32)
            s_scr[head_rows(h), offs[g]:offs[g] + rows[g]] = jnp.where(valid_ref[...] > 0.0, s + bias_ref[h], NEG_INF)
            s = lax.dot_general(q, kvn_ref[0, g, 0, h], _NT, preferred_element_type=F32)
            per_head.append(jnp.where(nvalid_ref[...] > 0.0, s + nbias_ref[h], NEG_INF))
        new_scores.append(jnp.concatenate(per_head, axis=0))

    s = s_scr[...]
    mx = jnp.max(s, axis=-1, keepdims=True)
    for sn in new_scores:
        mx = jnp.maximum(mx, jnp.max(sn, axis=-1, keepdims=True))
    e = jnp.exp(s - mx)
    den = jnp.sum(e, axis=-1, keepdims=True)
    s_scr[...] = e
    new_probs = []
    for sn in new_scores:
        en = jnp.exp(sn - mx)
        den = den + jnp.sum(en, axis=-1, keepdims=True)
        new_probs.append(en)
    inv = 1.0 / den

    for h in range(HEADS):
        acc = jnp.zeros((Q_PAD, HEAD_DIM), F32)
        for g, c_ref in enumerate(c_refs):
            p = s_scr[head_rows(h), offs[g]:offs[g] + rows[g]].astype(BF16)
            acc = acc + lax.dot_general(p, c_ref[1, h].astype(BF16), _NT, preferred_element_type=F32)
            acc = acc + jnp.dot(new_probs[g][head_rows(h), :].astype(BF16), kvn_ref[0, g, 1, h],
                                preferred_element_type=F32)
        o_ref[0, h] = acc * inv[head_rows(h), :]


def _mlp_attn_kernel(x_ref, g_ref, wu_ref, wd_ref, q_ref, kvn_ref, c0_ref, c1_ref, c2_ref, *rest):
    tables, out_ref, o_ref, acc_ref, h_ref, s_scr = rest[:-5], rest[-5], rest[-4], rest[-3], rest[-2], rest[-1]
    j = pl.program_id(1)

    @pl.when(j == 0)
    def _():
        acc_ref[...] = x_ref[...]
        h_ref[...] = _rms(x_ref[...], g_ref[...]).astype(BF16)

    h = h_ref[...]
    u = jnp.dot(h, wu_ref[:, pl.ds(pl.multiple_of(j * FF_CHUNK, FF_CHUNK), FF_CHUNK)], preferred_element_type=F32)
    a = jnp.square(jnp.maximum(u, 0.0)).astype(BF16)
    acc = acc_ref[...] + jnp.dot(a, wd_ref[j], preferred_element_type=F32)
    acc_ref[...] = acc
    _attn_sample_body(q_ref, kvn_ref, (c0_ref, c1_ref, c2_ref), tables, o_ref, s_scr)
    out_ref[...] = acc_ref[...]


def _sample_qkv_by_head(q, kvb, n, dec_seq):
    q5 = q.reshape(n, dec_seq, N_GROUPS, HEADS, HEAD_DIM).transpose(0, 2, 3, 1, 4)
    q5 = jnp.pad(q5, ((0, 0), (0, 0), (0, 0), (0, Q_PAD - dec_seq), (0, 0)))
    kvn = kvb.reshape(n, dec_seq, N_GROUPS, 2, HEADS, HEAD_DIM).transpose(0, 2, 3, 4, 1, 5)
    kvn = jnp.pad(kvn, ((0, 0), (0, 0), (0, 0), (0, 0), (0, NEW_PAD - dec_seq), (0, 0)))
    return q5, kvn


def _mlp_with_sample_attn(x, g, w_up, w_down, layer, q, kvb, caches, li, tables, n, dec_seq):
    m = x.shape[0]
    tm = min(ROW_TILE, m)
    assert n % (m // tm) == 0 and n // (m // tm) == FF_CHUNKS
    q5, kvn = _sample_qkv_by_head(q, kvb, n, dec_seq)
    seq = lambda i, j: i * FF_CHUNKS + j
    cache_spec = lambda c: pl.BlockSpec((None, None) + c.shape[2:], lambda i, j: (li, seq(i, j), 0, 0, 0, 0))
    row = pl.BlockSpec((tm, D_MODEL), lambda i, j: (i, 0))
    out, o = pl.pallas_call(
        _mlp_attn_kernel,
        grid=(m // tm, FF_CHUNKS),
        in_specs=[
            row,
            _const_spec((1, D_MODEL)),
            _layer_spec((D_MODEL, D_FF), layer),
            _layer_spec((FF_CHUNKS, FF_CHUNK, D_MODEL), layer),
            pl.BlockSpec((1, N_GROUPS, HEADS, Q_PAD, HEAD_DIM), lambda i, j: (seq(i, j), 0, 0, 0, 0)),
            pl.BlockSpec((1, N_GROUPS, 2, HEADS, NEW_PAD, HEAD_DIM), lambda i, j: (seq(i, j), 0, 0, 0, 0, 0)),
        ] + [cache_spec(c) for c in caches] + [_const_spec(t.shape) for t in tables],
        out_specs=[row, pl.BlockSpec((1, HEADS, Q_PAD, HEAD_DIM), lambda i, j: (seq(i, j), 0, 0, 0))],
        out_shape=[jax.ShapeDtypeStruct((m, D_MODEL), F32),
                   jax.ShapeDtypeStruct((n, HEADS, Q_PAD, HEAD_DIM), F32)],
        scratch_shapes=[pltpu.VMEM((tm, D_MODEL), F32), pltpu.VMEM((tm, D_MODEL), BF16),
                        pltpu.VMEM((HEADS * Q_PAD, sum(c.shape[-1] for c in caches)), F32)],
        compiler_params=pltpu.CompilerParams(dimension_semantics=("parallel", "arbitrary"),
                                             vmem_limit_bytes=FUSED_VMEM_LIMIT_BYTES),
        name="mlp_attn",
    )(x, g.reshape(1, D_MODEL), w_up, w_down, q5, kvn, *caches, *tables)
    return out, o[:, :, :dec_seq].transpose(0, 2, 1, 3).reshape(n * dec_seq, ATTN_OUT)


HALO = 16
POOL_PAD = 2 * HALO
SUBLANES = 8


def _pool_prompt_kernel(x_ref, g_ref, w_ref, sc_ref, o_ref, st_ref, a_ref, b_ref, *, tt):
    i = pl.program_id(1)
    n = POOL_PAD + tt

    @pl.when(i == 0)
    def _():
        a_ref[0:POOL_PAD, :] = jnp.zeros((POOL_PAD, D_MODEL), F32)
        b_ref[0:SUBLANES, :] = jnp.zeros((SUBLANES, D_MODEL), F32)

    x = x_ref[0]
    h = _rms(x, g_ref[...])
    a_ref[POOL_PAD:n, :] = h
    pos1 = (i * tt + 1 + lax.broadcasted_iota(jnp.int32, (tt, 1), 0)).astype(F32)
    ys = []
    for g, w in enumerate(POOL_WINDOWS):
        cols = slice(g * POOL_CH, (g + 1) * POOL_CH)
        src, dst = a_ref, b_ref
        span = 1
        while span < w:
            dst[SUBLANES:n, cols] = src[SUBLANES:n, cols] + src[SUBLANES - span:n - span, cols]
            src, dst = dst, src
            span *= 2
        p = src[POOL_PAD:n, cols] / jnp.minimum(float(w), pos1) - h[:, cols]
        ys.append(jnp.dot(p.astype(BF16), w_ref[g], preferred_element_type=F32))
    o_ref[0] = x + jnp.concatenate(ys, axis=1) * sc_ref[...]
    tail = h[tt - HALO:, :]
    a_ref[HALO:POOL_PAD, :] = tail

    @pl.when(i == pl.num_programs(1) - 1)
    def _():
        st_ref[0] = tail


def _pool_prompt(x, g, w_pool, scale, batch, seq):
    tt = min(ROW_TILE, seq)
    out, st = pl.pallas_call(
        functools.partial(_pool_prompt_kernel, tt=tt),
        grid=(batch, seq // tt),
        in_specs=[
            pl.BlockSpec((1, tt, D_MODEL), lambda b, i: (b, i, 0)),
            _const_spec((1, D_MODEL)),
            _const_spec(w_pool.shape),
            _const_spec((1, D_MODEL)),
        ],
        out_specs=[
            pl.BlockSpec((1, tt, D_MODEL), lambda b, i: (b, i, 0)),
            pl.BlockSpec((1, HALO, D_MODEL), lambda b, i: (b, 0, 0)),
        ],
        out_shape=[
            jax.ShapeDtypeStruct((batch, seq, D_MODEL), F32),
            jax.ShapeDtypeStruct((batch, HALO, D_MODEL), F32),
        ],
        scratch_shapes=[pltpu.VMEM((POOL_PAD + tt, D_MODEL), F32)] * 2,
        compiler_params=_cparams("parallel", "arbitrary"),
        name="pool_prompt",
    )(x.reshape(batch, seq, D_MODEL), g.reshape(1, D_MODEL), w_pool, scale.reshape(1, D_MODEL))
    return out.reshape(batch * seq, D_MODEL), st[:, HALO - POOL_STATE:, :]


def _pool_sample_kernel(x_ref, st_ref, g_ref, w_ref, sc_ref, o_ref, nst_ref, *, dec_seq, past_len):
    chunk = lambda ref, k: ref[:, k * D_MODEL:(k + 1) * D_MODEL]
    xs = [chunk(x_ref, t) for t in range(dec_seq)]
    hs = [_rms(x, g_ref[...]) for x in xs]
    rows = [st_ref[k] for k in range(POOL_STATE)] + hs
    for t in range(dec_seq):
        ys = []
        for g, w in enumerate(POOL_WINDOWS):
            cols = slice(g * POOL_CH, (g + 1) * POOL_CH)
            win = rows[POOL_STATE + t][:, cols]
            for j in range(1, w):
                win = win + rows[POOL_STATE + t - j][:, cols]
            p = win / float(min(w, past_len + t + 1)) - hs[t][:, cols]
            ys.append(jnp.dot(p.astype(BF16), w_ref[g], preferred_element_type=F32))
        o_ref[:, t * D_MODEL:(t + 1) * D_MODEL] = xs[t] + jnp.concatenate(ys, axis=1) * sc_ref[...]
    new_rows = rows[-POOL_STATE:]
    for k in range(POOL_STATE):
        nst_ref[k] = new_rows[k]


def _pool_sample(x, state_t, li, g, w_pool, scale, n, dec_seq, past_len):
    bn = min(32, n)
    out, nst = pl.pallas_call(
        functools.partial(_pool_sample_kernel, dec_seq=dec_seq, past_len=past_len),
        grid=(n // bn,),
        in_specs=[
            pl.BlockSpec((bn, dec_seq * D_MODEL), lambda i: (i, 0)),
            pl.BlockSpec((None, POOL_STATE, bn, D_MODEL), lambda i: (li, 0, i, 0)),
            _const_spec((1, D_MODEL)),
            _const_spec(w_pool.shape),
            _const_spec((1, D_MODEL)),
        ],
        out_specs=[
            pl.BlockSpec((bn, dec_seq * D_MODEL), lambda i: (i, 0)),
            pl.BlockSpec((POOL_STATE, bn, D_MODEL), lambda i: (0, i, 0)),
        ],
        out_shape=[
            jax.ShapeDtypeStruct((n, dec_seq * D_MODEL), F32),
            jax.ShapeDtypeStruct((POOL_STATE, n, D_MODEL), F32),
        ],
        compiler_params=_cparams("parallel"),
        name="pool_sample",
    )(x.reshape(n, dec_seq * D_MODEL), state_t, g.reshape(1, D_MODEL), w_pool, scale.reshape(1, D_MODEL))
    return out.reshape(n * dec_seq, D_MODEL), nst


def kernel(x_prompt, x_sample, state_pool, cache_kv_w128, cache_kv_w512, cache_kv_w2048, rel_bias, norm_mix,
           norm_ffn, norm_final, w_pool, pool_scale, w_qkv, w_o, w_up, w_down):
    batch, seq, _ = x_prompt.shape
    n, dec_seq, _ = x_sample.shape
    depth = norm_mix.shape[0]
    caches = tuple(jnp.transpose(c, (0, 1, 3, 4, 5, 2)) for c in (cache_kv_w128, cache_kv_w512, cache_kv_w2048))
    state_t = jnp.transpose(state_pool, (0, 2, 1, 3))
    past_len = PAST_LEN

    xp = x_prompt.reshape(batch * seq, D_MODEL)
    xs = x_sample.reshape(n * dec_seq, D_MODEL)
    w_up_b = w_up.astype(BF16)
    w_down_b = w_down.astype(BF16).reshape(depth, FF_CHUNKS, FF_CHUNK, D_MODEL)
    w_pool_b = w_pool.astype(BF16)
    w_o_b = w_o.astype(BF16)
    w_qkv_b = w_qkv.astype(BF16)
    w_kv_t = jnp.transpose(w_qkv_b[:, :, Q_COLS:], (0, 2, 1))
    prompt_bias = [_prompt_bias(rel_bias[:, g], dil) for g, (_, dil) in enumerate(ATTN_GROUPS)]
    prompt_cap = _prompt_cap()
    sample_tables = _sample_tables(rel_bias, dec_seq, tuple(c.shape[-1] for c in caches))

    kv_rows_major = lambda kvt: jnp.transpose(kvt, (0, 1, 5, 2, 3, 4))

    pool_p, pool_s = [], []
    kvts = None
    kv_s = [[] for _ in ATTN_GROUPS]
    ahead = None
    for i in range(depth):
        li = i // 2
        last = i == depth - 1
        if i % 2 == 0:
            xp, sp = _pool_prompt(xp, norm_mix[i], w_pool_b[li], pool_scale[li], batch, seq)
            xs, ss = _pool_sample(xs, state_t, li, norm_mix[i], w_pool_b[li], pool_scale[li], n, dec_seq,
                                  past_len)
            pool_p.append(sp)
            pool_s.append(ss)
        else:
            qps, kvps, kvts = _qkv_prompt(xp, norm_mix[i], w_qkv_b[li], w_kv_t[li], batch, seq, li, depth // 2, kvts)
            os, lses = zip(*[_attn_prompt(qps[g], kvps[g], prompt_bias[g], prompt_cap) for g in range(N_GROUPS)])
            xp = _wo_merge(os, lses, w_o_b[li], xp, batch, seq)
            assert ahead is not None, "every attention layer must follow a pooling layer"
            o_s, kvfs = ahead
            ahead = None
            xs = _wo(o_s, w_o_b[li], xs)
            kvfs = kvfs.reshape(n, dec_seq, N_GROUPS, 2, HEADS, HEAD_DIM)
            for g in range(N_GROUPS):
                kv_s[g].append(kvfs[:, :, g])
        xs = _mlp(xs, norm_ffn[i], w_up_b, w_down_b, i, norm_final, last)
        if i % 2 == 0 and i + 1 < depth:
            nli = (i + 1) // 2
            qs, kvbs, kvfs = _qkv(xs, norm_mix[i + 1], w_qkv_b[nli])
            xp, o_s = _mlp_with_sample_attn(xp, norm_ffn[i], w_up_b, w_down_b, i, qs, kvbs, caches, nli,
                                            sample_tables, n, dec_seq)
            ahead = (o_s, kvfs)
        else:
            xp = _mlp(xp, norm_ffn[i], w_up_b, w_down_b, i, norm_final, last)
    return (xp.reshape(batch, seq, D_MODEL), xs.reshape(n, dec_seq, D_MODEL),
            jnp.stack(pool_p), jnp.transpose(jnp.stack(pool_s), (0, 2, 1, 3)),
            kv_rows_major(kvts[0]), jnp.stack(kv_s[0]),
            kv_rows_major(kvts[1]), jnp.stack(kv_s[1]),
            kv_rows_major(kvts[2]), jnp.stack(kv_s[2]))
```

```python
import functools

import numpy as np
import jax
import jax.numpy as jnp
from jax import lax
from jax.experimental import pallas as pl
from jax.experimental.pallas import tpu as pltpu

F32 = jnp.float32
BF16 = jnp.bfloat16

D_MODEL = 1024
D_FF = 4 * D_MODEL
POOL_WINDOWS = (2, 4, 8, 16)
POOL_CH = D_MODEL // len(POOL_WINDOWS)
POOL_STATE = max(POOL_WINDOWS) - 1
ATTN_GROUPS = ((128, 1), (512, 4), (2048, 16))
N_GROUPS = len(ATTN_GROUPS)
HEADS = 8
HEAD_DIM = 64
ATTN_OUT = HEADS * HEAD_DIM
BAND = 128
N_BUCKETS = 32
MAX_EXACT = N_BUCKETS // 2
REL_MAX_DIST = 2048
PAST_LEN = 2048
RMS_EPS = 1e-6
NEG_INF = -1e30

VMEM_LIMIT_BYTES = 56 * 1024 * 1024
FUSED_VMEM_LIMIT_BYTES = 62 * 1024 * 1024
ROW_TILE = 512
FF_CHUNK = 1024
FF_CHUNKS = D_FF // FF_CHUNK
_NT = (((1,), (1,)), ((), ()))
LANES = 128


def _cparams(*sem):
    return pltpu.CompilerParams(dimension_semantics=sem, vmem_limit_bytes=VMEM_LIMIT_BYTES)


def _rms(x, g):
    ms = jnp.mean(x * x, axis=-1, keepdims=True)
    return x * lax.rsqrt(ms + RMS_EPS) * g


def _t5_bucket(dist):
    n = np.maximum(np.asarray(dist), 0)
    large = MAX_EXACT + (np.log(np.maximum(n, 1) / MAX_EXACT) / np.log(REL_MAX_DIST / MAX_EXACT)
                         * (N_BUCKETS - MAX_EXACT)).astype(np.int64)
    large = np.minimum(large, N_BUCKETS - 1)
    return np.where(n < MAX_EXACT, n, large).astype(np.int32)


def _const_spec(shape):
    nd = len(shape)
    return pl.BlockSpec(shape, lambda *_: (0,) * nd, pipeline_mode=pl.Buffered(1))


def _layer_spec(shape, layer):
    nd = len(shape)
    return pl.BlockSpec((None,) + tuple(shape), lambda *_: (layer,) + (0,) * nd, pipeline_mode=pl.Buffered(1))


def _mlp_kernel(x_ref, g_ref, wu_ref, wd_ref, gf_ref, o_ref, *, final_norm):
    x = x_ref[...]
    h = _rms(x, g_ref[...]).astype(BF16)
    acc = x
    for c in range(FF_CHUNKS):
        u = jnp.dot(h, wu_ref[:, c * FF_CHUNK:(c + 1) * FF_CHUNK], preferred_element_type=F32)
        a = jnp.square(jnp.maximum(u, 0.0)).astype(BF16)
        acc = acc + jnp.dot(a, wd_ref[c], preferred_element_type=F32)
    if final_norm:
        acc = _rms(acc, gf_ref[...])
    o_ref[...] = acc


def _mlp(x, g, w_up, w_down, layer, g_final, final_norm):
    m = x.shape[0]
    tm = min(ROW_TILE, m)
    return pl.pallas_call(
        functools.partial(_mlp_kernel, final_norm=final_norm),
        grid=(m // tm,),
        in_specs=[
            pl.BlockSpec((tm, D_MODEL), lambda i: (i, 0)),
            _const_spec((1, D_MODEL)),
            _layer_spec((D_MODEL, D_FF), layer),
            _layer_spec((FF_CHUNKS, FF_CHUNK, D_MODEL), layer),
            _const_spec((1, D_MODEL)),
        ],
        out_specs=pl.BlockSpec((tm, D_MODEL), lambda i: (i, 0)),
        out_shape=jax.ShapeDtypeStruct((m, D_MODEL), F32),
        compiler_params=_cparams("parallel"),
        name="mlp",
    )(x, g.reshape(1, D_MODEL), w_up, w_down, g_final.reshape(1, D_MODEL))


Q_COLS = N_GROUPS * ATTN_OUT
KV_COLS = 2 * N_GROUPS * ATTN_OUT


def _w_cols(which, g):
    start = (which * N_GROUPS + g) * ATTN_OUT
    return slice(start, start + ATTN_OUT)


def _qkv_kernel(x_ref, g_ref, w_ref, q_ref, kvb_ref, kvf_ref):
    h = _rms(x_ref[...], g_ref[...]).astype(BF16)
    q = jnp.dot(h, w_ref[:, :Q_COLS], preferred_element_type=F32)
    q_ref[...] = (q * (HEAD_DIM ** -0.5)).astype(BF16)
    for g in range(N_GROUPS):
        for which in (1, 2):
            out = slice((2 * g + which - 1) * ATTN_OUT, (2 * g + which) * ATTN_OUT)
            kv = jnp.dot(h, w_ref[:, _w_cols(which, g)], preferred_element_type=F32)
            kvf_ref[:, out] = kv
            kvb_ref[:, out] = kv.astype(BF16)


def _qkv(x, g, w):
    m = x.shape[0]
    tm = min(ROW_TILE, m)
    row = lambda i: (i, 0)
    return pl.pallas_call(
        _qkv_kernel,
        grid=(m // tm,),
        in_specs=[
            pl.BlockSpec((tm, D_MODEL), row),
            _const_spec((1, D_MODEL)),
            _const_spec((D_MODEL, Q_COLS + KV_COLS)),
        ],
        out_specs=[
            pl.BlockSpec((tm, Q_COLS), row),
            pl.BlockSpec((tm, KV_COLS), row),
            pl.BlockSpec((tm, KV_COLS), row),
        ],
        out_shape=[
            jax.ShapeDtypeStruct((m, Q_COLS), BF16),
            jax.ShapeDtypeStruct((m, KV_COLS), BF16),
            jax.ShapeDtypeStruct((m, KV_COLS), F32),
        ],
        compiler_params=_cparams("parallel"),
        name="qkv",
    )(x, g.reshape(1, D_MODEL), w)


def _qkv_prompt_kernel(x_ref, g_ref, w_ref, wt_ref, *rest, tm, n_tiles, wins, n_prev):
    rest = rest[n_prev:]
    q_refs, kv_refs, kvt_refs, scr_ref = rest[0:3], rest[3:6], rest[6:9], rest[9]
    i = pl.program_id(1)
    hf = _rms(x_ref[0], g_ref[...])
    h = hf.astype(BF16)

    h_by_dil = {1: h}
    dils = sorted({dil for _, dil in ATTN_GROUPS if dil > 1})
    chunks = D_MODEL // LANES
    if dils:
        for c in range(chunks):
            scr_ref[c] = hf[:, c * LANES:(c + 1) * LANES]
    for dil in dils:
        classes = [jnp.concatenate([scr_ref[c, pl.ds(r, tm // dil, stride=dil), :] for c in range(chunks)], axis=1)
                   for r in range(dil)]
        h_by_dil[dil] = jnp.concatenate(classes, axis=0).astype(BF16)

    for g, (_, dil) in enumerate(ATTN_GROUPS):
        hg = h_by_dil[dil]
        q = jnp.dot(hg, w_ref[:, _w_cols(0, g)], preferred_element_type=F32) * (HEAD_DIM ** -0.5)
        q_refs[g][0] = q.reshape(dil, tm // dil, ATTN_OUT).astype(BF16)
        for which in (1, 2):
            kv = jnp.dot(hg, w_ref[:, _w_cols(which, g)], preferred_element_type=F32)
            kv_refs[g][0, :, :, (which - 1) * ATTN_OUT:which * ATTN_OUT] = (
                kv.reshape(dil, tm // dil, ATTN_OUT).astype(BF16))

    for g, win in enumerate(wins):
        rows = min(win, tm)
        first_tile = n_tiles - max(win // tm, 1)

        @pl.when(i >= first_tile)
        def _(g=g, rows=rows):
            for which in (1, 2):
                start = ((which - 1) * N_GROUPS + g) * ATTN_OUT
                kvt = lax.dot_general(wt_ref[start:start + ATTN_OUT, :], h[tm - rows:, :], _NT,
                                      preferred_element_type=F32)
                kvt_refs[g][0, which - 1] = kvt.reshape(HEADS, HEAD_DIM, rows)


def _qkv_prompt(x, g, w, w_t, batch, seq, li, n_layers, kvt_prev):
    tm = min(ROW_TILE, seq)
    n_tiles = seq // tm
    wins = tuple(min(win, seq) for win, _ in ATTN_GROUPS)
    assert all(w_ % tm == 0 or tm % w_ == 0 for w_ in wins)
    split = lambda width: [
        (pl.BlockSpec((1, dil, tm // dil, width), lambda b, i: (b, 0, i, 0)),
         jax.ShapeDtypeStruct((batch, dil, seq // dil, width), BF16)) for _, dil in ATTN_GROUPS]
    kvt = [(pl.BlockSpec((None, 1, 2, HEADS, HEAD_DIM, min(win, tm)),
                         lambda b, i, first=n_tiles - max(win // tm, 1): (li, b, 0, 0, 0, jnp.maximum(i - first, 0))),
            jax.ShapeDtypeStruct((n_layers, batch, 2, HEADS, HEAD_DIM, win), F32)) for win in wins]
    outs = split(ATTN_OUT) + split(2 * ATTN_OUT) + kvt
    prev = [jnp.zeros(o[1].shape, F32) for o in kvt] if kvt_prev is None else list(kvt_prev)
    n_in = 4
    res = pl.pallas_call(
        functools.partial(_qkv_prompt_kernel, tm=tm, n_tiles=n_tiles, wins=wins, n_prev=len(prev)),
        grid=(batch, n_tiles),
        in_specs=[
            pl.BlockSpec((1, tm, D_MODEL), lambda b, i: (b, i, 0)),
            _const_spec((1, D_MODEL)),
            _const_spec((D_MODEL, Q_COLS + KV_COLS)),
            _const_spec((KV_COLS, D_MODEL)),
        ] + [pl.BlockSpec(memory_space=pl.ANY)] * len(prev),
        out_specs=[o[0] for o in outs],
        out_shape=[o[1] for o in outs],
        input_output_aliases={n_in + k: 2 * N_GROUPS + k for k in range(len(prev))},
        scratch_shapes=[pltpu.VMEM((D_MODEL // LANES, tm, LANES), F32)],
        compiler_params=_cparams("parallel", "arbitrary"),
        name="qkv_prompt",
    )(x.reshape(batch, seq, D_MODEL), g.reshape(1, D_MODEL), w, w_t, *prev)
    return res[0:3], res[3:6], res[6:9]


def _attn_prompt_kernel(q_ref, kvp_ref, kvc_ref, bias_ref, cap_ref, o_ref, st_ref, s_scr, p_scr, *,
                        classes, blocks):
    n = pl.program_id(2)
    lane = lax.broadcasted_iota(jnp.int32, (BAND, LANES), 1)
    low_half = lane < HEAD_DIM
    half_sel = [low_half.astype(F32).astype(BF16), (~low_half).astype(F32).astype(BF16)]

    for cls, blk in [(c, b) for c in range(classes) for b in range(blocks)]:
        rows = slice(blk * BAND, (blk + 1) * BAND)
        if blk == 0:
            first = jnp.where(n > 0, 0, 1)
            prev = lambda cols, cls=cls: kvp_ref[0, cls, :, cols]
        else:
            first = 0
            prev = lambda cols, cls=cls, blk=blk: kvc_ref[0, cls, (blk - 1) * BAND:blk * BAND, cols]
        keys = lambda cols, prev=prev, cls=cls, rows=rows: jnp.concatenate(
            [prev(cols), kvc_ref[0, cls, rows, cols]], axis=0)

        for pair in range(HEADS // 2):
            cols = slice(pair * LANES, (pair + 1) * LANES)
            q2 = q_ref[0, cls, rows, cols]
            k2 = keys(cols)
            for half in range(2):
                s_scr[2 * pair + half] = lax.dot_general(q2 * half_sel[half], k2, _NT,
                                                         preferred_element_type=F32)

        stats = jnp.zeros((BAND, LANES), F32)
        for h in range(HEADS):
            t = jnp.minimum(s_scr[h] + bias_ref[h], cap_ref[first])
            mx = jnp.max(t, axis=-1, keepdims=True)
            e = jnp.exp(t - mx)
            den = jnp.sum(e, axis=-1, keepdims=True)
            p_scr[h] = e.astype(BF16)
            stats = jnp.where(lane == h, mx, jnp.where(lane == HEADS + h, den, stats))
        st_ref[0, cls, rows, :] = stats

        for pair in range(HEADS // 2):
            cols = slice(pair * LANES, (pair + 1) * LANES)
            v2 = keys(slice(ATTN_OUT + cols.start, ATTN_OUT + cols.stop))
            o_lo = jnp.dot(p_scr[2 * pair], v2, preferred_element_type=F32)
            o_hi = jnp.dot(p_scr[2 * pair + 1], v2, preferred_element_type=F32)
            o_ref[0, cls, rows, cols] = jnp.where(low_half, o_lo, o_hi)


BLOCKS_PER_STEP = 4

def _attn_prompt(q, kv, bias, cap):
    batch, dil, sub, _ = q.shape
    blocks = min(BLOCKS_PER_STEP, sub // BAND)
    classes = min(BLOCKS_PER_STEP // blocks, dil)
    step = blocks * BAND
    assert sub % step == 0 and dil % classes == 0
    cur = lambda b, r, n: (b, r, n, 0)
    return pl.pallas_call(
        functools.partial(_attn_prompt_kernel, classes=classes, blocks=blocks),
        grid=(batch, dil // classes, sub // step),
        in_specs=[
            pl.BlockSpec((1, classes, step, ATTN_OUT), cur),
            pl.BlockSpec((1, classes, BAND, 2 * ATTN_OUT),
                         lambda b, r, n: (b, r, jnp.maximum(n * blocks - 1, 0), 0)),
            pl.BlockSpec((1, classes, step, 2 * ATTN_OUT), cur),
            _const_spec((HEADS, BAND, 2 * BAND)),
            _const_spec((2, BAND, 2 * BAND)),
        ],
        out_specs=[
            pl.BlockSpec((1, classes, step, ATTN_OUT), cur),
            pl.BlockSpec((1, classes, step, LANES), cur),
        ],
        out_shape=[
            jax.ShapeDtypeStruct((batch, dil, sub, ATTN_OUT), F32),
            jax.ShapeDtypeStruct((batch, dil, sub, LANES), F32),
        ],
        scratch_shapes=[pltpu.VMEM((HEADS, BAND, 2 * BAND), F32), pltpu.VMEM((HEADS, BAND, 2 * BAND), BF16)],
        compiler_params=_cparams("parallel", "parallel", "arbitrary"),
        name="attn_prompt_d%d" % dil,
    )(q, kv, kv, bias, cap)


def _prompt_bias(rel_bias_g, dil):
    tab = rel_bias_g.astype(F32)[_t5_bucket(np.arange(BAND, -1, -1) * dil)]
    ext = jnp.pad(tab.T, ((0, 0), (BAND, BAND)))
    tiled = jnp.tile(ext, (1, BAND))[:, :BAND * 3 * BAND].reshape(HEADS, BAND, 3 * BAND)
    return tiled[:, :, BAND:]


def _prompt_cap():
    m = BAND + np.arange(BAND)[:, None] - np.arange(2 * BAND)[None, :]
    band = (m >= 0) & (m <= BAND)
    no_prev = band & (np.arange(2 * BAND) >= BAND)[None, :]
    big = np.finfo(np.float32).max
    return jnp.asarray(np.where(np.stack([band, no_prev]), big, NEG_INF).astype(np.float32))


def _wo_merge_kernel(o0_ref, o1_ref, o2_ref, l0_ref, l1_ref, l2_ref, e_ref, w_ref, x_ref, out_ref,
                     oscr_ref, lscr_ref, *, tm):
    def token_order(ref, scr_ref, dil):
        if dil == 1:
            return ref[0, 0]
        chunks = ref.shape[-1] // LANES
        for r in range(dil):
            for c in range(chunks):
                scr_ref[c, pl.ds(r, tm // dil, stride=dil), :] = ref[0, r, :, c * LANES:(c + 1) * LANES]
        return jnp.concatenate([scr_ref[c] for c in range(chunks)], axis=1)

    dils = [dil for _, dil in ATTN_GROUPS]
    sts = [token_order(l_ref, lscr_ref.at[g:g + 1], dils[g]) for g, l_ref in enumerate((l0_ref, l1_ref, l2_ref))]
    mxs = [st[:, :HEADS] for st in sts]
    dens = [st[:, HEADS:2 * HEADS] for st in sts]
    mx = jnp.maximum(jnp.maximum(mxs[0], mxs[1]), mxs[2])
    es = [jnp.exp(m - mx) for m in mxs]
    inv = 1.0 / (es[0] * dens[0] + es[1] * dens[1] + es[2] * dens[2])
    acc = None
    for g, o_ref in enumerate((o0_ref, o1_ref, o2_ref)):
        wt = es[g] * inv
        hi = wt.astype(BF16).astype(F32)
        parts = jnp.concatenate([hi, wt - hi], axis=1).astype(BF16)
        wexp = jnp.dot(parts, e_ref[...], preferred_element_type=F32)
        term = wexp * token_order(o_ref, oscr_ref, dils[g])
        acc = term if acc is None else acc + term
    out_ref[0] = x_ref[0] + jnp.dot(acc.astype(BF16), w_ref[...], preferred_element_type=F32)


def _wo_merge(os, lses, w_o, x, batch, seq):
    tm = min(ROW_TILE, seq)
    expand = jnp.asarray(np.tile(np.repeat(np.eye(HEADS, dtype=np.float32), HEAD_DIM, axis=1), (2, 1)), BF16)
    split = lambda width: [pl.BlockSpec((1, dil, tm // dil, width), lambda b, i: (b, 0, i, 0))
                           for _, dil in ATTN_GROUPS]
    row = pl.BlockSpec((1, tm, D_MODEL), lambda b, i: (b, i, 0))
    out = pl.pallas_call(
        functools.partial(_wo_merge_kernel, tm=tm),
        grid=(batch, seq // tm),
        in_specs=split(ATTN_OUT) + split(LANES) + [
            _const_spec((2 * HEADS, ATTN_OUT)),
            _const_spec((ATTN_OUT, D_MODEL)),
            row,
        ],
        out_specs=row,
        out_shape=jax.ShapeDtypeStruct((batch, seq, D_MODEL), F32),
        scratch_shapes=[pltpu.VMEM((ATTN_OUT // LANES, tm, LANES), F32), pltpu.VMEM((N_GROUPS, tm, LANES), F32)],
        compiler_params=_cparams("parallel", "parallel"),
        name="wo_merge",
    )(*os, *lses, expand, w_o, x.reshape(batch, seq, D_MODEL))
    return out.reshape(batch * seq, D_MODEL)


def _wo_kernel(o_ref, w_ref, x_ref, out_ref):
    out_ref[...] = x_ref[...] + jnp.dot(o_ref[...].astype(BF16), w_ref[...], preferred_element_type=F32)


def _wo(o, w_o, x):
    m = x.shape[0]
    tm = min(ROW_TILE, m)
    row = lambda i: (i, 0)
    return pl.pallas_call(
        _wo_kernel,
        grid=(m // tm,),
        in_specs=[pl.BlockSpec((tm, ATTN_OUT), row), _const_spec((ATTN_OUT, D_MODEL)),
                  pl.BlockSpec((tm, D_MODEL), row)],
        out_specs=pl.BlockSpec((tm, D_MODEL), row),
        out_shape=jax.ShapeDtypeStruct((m, D_MODEL), F32),
        compiler_params=_cparams("parallel"),
        name="wo",
    )(o, w_o, x)


Q_PAD = 8
NEW_PAD = 16


def _sample_tables(rel_bias, dec_seq, buf_rows):
    tq = np.arange(Q_PAD)
    tables = []
    for g, (_, dil) in enumerate(ATTN_GROUPS):
        wb = buf_rows[g]
        by_dist = rel_bias[:, g].astype(F32)[_t5_bucket(np.arange(wb + Q_PAD))]
        rev = by_dist[::-1].T
        for kpos, real in ((np.arange(wb), np.ones(wb, bool)),
                           (wb + np.arange(NEW_PAD), np.arange(NEW_PAD) < dec_seq)):
            delta = (wb + tq)[:, None] - kpos[None, :]
            ok = (delta >= 0) & (delta % dil == 0) & (delta // dil <= BAND)
            ok &= (tq < dec_seq)[:, None] & real[None, :]
            if kpos.shape[0] == wb:
                bias = jnp.stack([rev[:, Q_PAD - 1 - t:Q_PAD - 1 - t + wb] for t in range(Q_PAD)], axis=1)
            else:
                bias = jnp.transpose(by_dist[np.maximum(delta, 0)], (2, 0, 1))
            tables.append(bias)
            tables.append(jnp.asarray(ok.astype(np.float32)))
    return tables


def _attn_sample_body(q_ref, kvn_ref, c_refs, tables, o_ref, s_scr):
    rows = [c.shape[-1] for c in c_refs]
    offs = [sum(rows[:g]) for g in range(len(rows))]
    head_rows = lambda h: slice(h * Q_PAD, (h + 1) * Q_PAD)
    new_scores = []
    for g, c_ref in enumerate(c_refs):
        bias_ref, valid_ref, nbias_ref, nvalid_ref = tables[4 * g:4 * g + 4]
        per_head = []
        for h in range(HEADS):
            q = q_ref[0, g, h]
            s = jnp.dot(q, c_ref[0, h].astype(BF16), preferred_element_type=F32)
            s_scr[head_rows(h), offs[g]:offs[g] + rows[g]] = jnp.where(valid_ref[...] > 0.0, s + bias_ref[h], NEG_INF)
            s = lax.dot_general(q, kvn_ref[0, g, 0, h], _NT, preferred_element_type=F32)
            per_head.append(jnp.where(nvalid_ref[...] > 0.0, s + nbias_ref[h], NEG_INF))
        new_scores.append(jnp.concatenate(per_head, axis=0))

    s = s_scr[...]
    mx = jnp.max(s, axis=-1, keepdims=True)
    for sn in new_scores:
        mx = jnp.maximum(mx, jnp.max(sn, axis=-1, keepdims=True))
    e = jnp.exp(s - mx)
    den = jnp.sum(e, axis=-1, keepdims=True)
    s_scr[...] = e
    new_probs = []
    for sn in new_scores:
        en = jnp.exp(sn - mx)
        den = den + jnp.sum(en, axis=-1, keepdims=True)
        new_probs.append(en)
    inv = 1.0 / den

    for h in range(HEADS):
        acc = jnp.zeros((Q_PAD, HEAD_DIM), F32)
        for g, c_ref in enumerate(c_refs):
            p = s_scr[head_rows(h), offs[g]:offs[g] + rows[g]].astype(BF16)
            acc = acc + lax.dot_general(p, c_ref[1, h].astype(BF16), _NT, preferred_element_type=F32)
            acc = acc + jnp.dot(new_probs[g][head_rows(h), :].astype(BF16), kvn_ref[0, g, 1, h],
                                preferred_element_type=F32)
        o_ref[0, h] = acc * inv[head_rows(h), :]


def _mlp_attn_kernel(x_ref, g_ref, wu_ref, wd_ref, q_ref, kvn_ref, c0_ref, c1_ref, c2_ref, *rest):
    tables, out_ref, o_ref, acc_ref, h_ref, s_scr = rest[:-5], rest[-5], rest[-4], rest[-3], rest[-2], rest[-1]
    j = pl.program_id(1)

    @pl.when(j == 0)
    def _():
        acc_ref[...] = x_ref[...]
        h_ref[...] = _rms(x_ref[...], g_ref[...]).astype(BF16)

    h = h_ref[...]
    u = jnp.dot(h, wu_ref[:, pl.ds(pl.multiple_of(j * FF_CHUNK, FF_CHUNK), FF_CHUNK)], preferred_element_type=F32)
    a = jnp.square(jnp.maximum(u, 0.0)).astype(BF16)
    acc = acc_ref[...] + jnp.dot(a, wd_ref[j], preferred_element_type=F32)
    acc_ref[...] = acc
    _attn_sample_body(q_ref, kvn_ref, (c0_ref, c1_ref, c2_ref), tables, o_ref, s_scr)
    out_ref[...] = acc_ref[...]


def _sample_qkv_by_head(q, kvb, n, dec_seq):
    q5 = q.reshape(n, dec_seq, N_GROUPS, HEADS, HEAD_DIM).transpose(0, 2, 3, 1, 4)
    q5 = jnp.pad(q5, ((0, 0), (0, 0), (0, 0), (0, Q_PAD - dec_seq), (0, 0)))
    kvn = kvb.reshape(n, dec_seq, N_GROUPS, 2, HEADS, HEAD_DIM).transpose(0, 2, 3, 4, 1, 5)
    kvn = jnp.pad(kvn, ((0, 0), (0, 0), (0, 0), (0, 0), (0, NEW_PAD - dec_seq), (0, 0)))
    return q5, kvn


def _mlp_with_sample_attn(x, g, w_up, w_down, layer, q, kvb, caches, li, tables, n, dec_seq):
    m = x.shape[0]
    tm = min(ROW_TILE, m)
    assert n % (m // tm) == 0 and n // (m // tm) == FF_CHUNKS
    q5, kvn = _sample_qkv_by_head(q, kvb, n, dec_seq)
    seq = lambda i, j: i * FF_CHUNKS + j
    cache_spec = lambda c: pl.BlockSpec((None, None) + c.shape[2:], lambda i, j: (li, seq(i, j), 0, 0, 0, 0))
    row = pl.BlockSpec((tm, D_MODEL), lambda i, j: (i, 0))
    out, o = pl.pallas_call(
        _mlp_attn_kernel,
        grid=(m // tm, FF_CHUNKS),
        in_specs=[
            row,
            _const_spec((1, D_MODEL)),
            _layer_spec((D_MODEL, D_FF), layer),
            _layer_spec((FF_CHUNKS, FF_CHUNK, D_MODEL), layer),
            pl.BlockSpec((1, N_GROUPS, HEADS, Q_PAD, HEAD_DIM), lambda i, j: (seq(i, j), 0, 0, 0, 0)),
            pl.BlockSpec((1, N_GROUPS, 2, HEADS, NEW_PAD, HEAD_DIM), lambda i, j: (seq(i, j), 0, 0, 0, 0, 0)),
        ] + [cache_spec(c) for c in caches] + [_const_spec(t.shape) for t in tables],
        out_specs=[row, pl.BlockSpec((1, HEADS, Q_PAD, HEAD_DIM), lambda i, j: (seq(i, j), 0, 0, 0))],
        out_shape=[jax.ShapeDtypeStruct((m, D_MODEL), F32),
                   jax.ShapeDtypeStruct((n, HEADS, Q_PAD, HEAD_DIM), F32)],
        scratch_shapes=[pltpu.VMEM((tm, D_MODEL), F32), pltpu.VMEM((tm, D_MODEL), BF16),
                        pltpu.VMEM((HEADS * Q_PAD, sum(c.shape[-1] for c in caches)), F32)],
        compiler_params=pltpu.CompilerParams(dimension_semantics=("parallel", "arbitrary"),
                                             vmem_limit_bytes=FUSED_VMEM_LIMIT_BYTES),
        name="mlp_attn",
    )(x, g.reshape(1, D_MODEL), w_up, w_down, q5, kvn, *caches, *tables)
    return out, o[:, :, :dec_seq].transpose(0, 2, 1, 3).reshape(n * dec_seq, ATTN_OUT)


HALO = 16
POOL_PAD = 2 * HALO
SUBLANES = 8


def _pool_prompt_kernel(x_ref, g_ref, w_ref, sc_ref, o_ref, st_ref, a_ref, b_ref, *, tt):
    i = pl.program_id(1)
    n = POOL_PAD + tt

    @pl.when(i == 0)
    def _():
        a_ref[0:POOL_PAD, :] = jnp.zeros((POOL_PAD, D_MODEL), F32)
        b_ref[0:SUBLANES, :] = jnp.zeros((SUBLANES, D_MODEL), F32)

    x = x_ref[0]
    h = _rms(x, g_ref[...])
    a_ref[POOL_PAD:n, :] = h
    pos1 = (i * tt + 1 + lax.broadcasted_iota(jnp.int32, (tt, 1), 0)).astype(F32)
    ys = []
    for g, w in enumerate(POOL_WINDOWS):
        cols = slice(g * POOL_CH, (g + 1) * POOL_CH)
        src, dst = a_ref, b_ref
        span = 1
        while span < w:
            dst[SUBLANES:n, cols] = src[SUBLANES:n, cols] + src[SUBLANES - span:n - span, cols]
            src, dst = dst, src
            span *= 2
        p = src[POOL_PAD:n, cols] / jnp.minimum(float(w), pos1) - h[:, cols]
        ys.append(jnp.dot(p.astype(BF16), w_ref[g], preferred_element_type=F32))
    o_ref[0] = x + jnp.concatenate(ys, axis=1) * sc_ref[...]
    tail = h[tt - HALO:, :]
    a_ref[HALO:POOL_PAD, :] = tail

    @pl.when(i == pl.num_programs(1) - 1)
    def _():
        st_ref[0] = tail


def _pool_prompt(x, g, w_pool, scale, batch, seq):
    tt = min(ROW_TILE, seq)
    out, st = pl.pallas_call(
        functools.partial(_pool_prompt_kernel, tt=tt),
        grid=(batch, seq // tt),
        in_specs=[
            pl.BlockSpec((1, tt, D_MODEL), lambda b, i: (b, i, 0)),
            _const_spec((1, D_MODEL)),
            _const_spec(w_pool.shape),
            _const_spec((1, D_MODEL)),
        ],
        out_specs=[
            pl.BlockSpec((1, tt, D_MODEL), lambda b, i: (b, i, 0)),
            pl.BlockSpec((1, HALO, D_MODEL), lambda b, i: (b, 0, 0)),
        ],
        out_shape=[
            jax.ShapeDtypeStruct((batch, seq, D_MODEL), F32),
            jax.ShapeDtypeStruct((batch, HALO, D_MODEL), F32),
        ],
        scratch_shapes=[pltpu.VMEM((POOL_PAD + tt, D_MODEL), F32)] * 2,
        compiler_params=_cparams("parallel", "arbitrary"),
        name="pool_prompt",
    )(x.reshape(batch, seq, D_MODEL), g.reshape(1, D_MODEL), w_pool, scale.reshape(1, D_MODEL))
    return out.reshape(batch * seq, D_MODEL), st[:, HALO - POOL_STATE:, :]


def _pool_sample_kernel(x_ref, st_ref, g_ref, w_ref, sc_ref, o_ref, nst_ref, *, dec_seq, past_len):
    chunk = lambda ref, k: ref[:, k * D_MODEL:(k + 1) * D_MODEL]
    xs = [chunk(x_ref, t) for t in range(dec_seq)]
    hs = [_rms(x, g_ref[...]) for x in xs]
    rows = [st_ref[k] for k in range(POOL_STATE)] + hs
    for t in range(dec_seq):
        ys = []
        for g, w in enumerate(POOL_WINDOWS):
            cols = slice(g * POOL_CH, (g + 1) * POOL_CH)
            win = rows[POOL_STATE + t][:, cols]
            for j in range(1, w):
                win = win + rows[POOL_STATE + t - j][:, cols]
            p = win / float(min(w, past_len + t + 1)) - hs[t][:, cols]
            ys.append(jnp.dot(p.astype(BF16), w_ref[g], preferred_element_type=F32))
        o_ref[:, t * D_MODEL:(t + 1) * D_MODEL] = xs[t] + jnp.concatenate(ys, axis=1) * sc_ref[...]
    new_rows = rows[-POOL_STATE:]
    for k in range(POOL_STATE):
        nst_ref[k] = new_rows[k]


def _pool_sample(x, state_t, li, g, w_pool, scale, n, dec_seq, past_len):
    bn = min(32, n)
    out, nst = pl.pallas_call(
        functools.partial(_pool_sample_kernel, dec_seq=dec_seq, past_len=past_len),
        grid=(n // bn,),
        in_specs=[
            pl.BlockSpec((bn, dec_seq * D_MODEL), lambda i: (i, 0)),
            pl.BlockSpec((None, POOL_STATE, bn, D_MODEL), lambda i: (li, 0, i, 0)),
            _const_spec((1, D_MODEL)),
            _const_spec(w_pool.shape),
            _const_spec((1, D_MODEL)),
        ],
        out_specs=[
            pl.BlockSpec((bn, dec_seq * D_MODEL), lambda i: (i, 0)),
            pl.BlockSpec((POOL_STATE, bn, D_MODEL), lambda i: (0, i, 0)),
        ],
        out_shape=[
            jax.ShapeDtypeStruct((n, dec_seq * D_MODEL), F32),
            jax.ShapeDtypeStruct((POOL_STATE, n, D_MODEL), F32),
        ],
        compiler_params=_cparams("parallel"),
        name="pool_sample",
    )(x.reshape(n, dec_seq * D_MODEL), state_t, g.reshape(1, D_MODEL), w_pool, scale.reshape(1, D_MODEL))
    return out.reshape(n * dec_seq, D_MODEL), nst


def kernel(x_prompt, x_sample, state_pool, cache_kv_w128, cache_kv_w512, cache_kv_w2048, rel_bias, norm_mix,
           norm_ffn, norm_final, w_pool, pool_scale, w_qkv, w_o, w_up, w_down):
    batch, seq, _ = x_prompt.shape
    n, dec_seq, _ = x_sample.shape
    depth = norm_mix.shape[0]
    caches = tuple(jnp.transpose(c, (0, 1, 3, 4, 5, 2)) for c in (cache_kv_w128, cache_kv_w512, cache_kv_w2048))
    state_t = jnp.transpose(state_pool, (0, 2, 1, 3))
    past_len = PAST_LEN

    xp = x_prompt.reshape(batch * seq, D_MODEL)
    xs = x_sample.reshape(n * dec_seq, D_MODEL)
    w_up_b = w_up.astype(BF16)
    w_down_b = w_down.astype(BF16).reshape(depth, FF_CHUNKS, FF_CHUNK, D_MODEL)
    w_pool_b = w_pool.astype(BF16)
    w_o_b = w_o.astype(BF16)
    w_qkv_b = w_qkv.astype(BF16)
    w_kv_t = jnp.transpose(w_qkv_b[:, :, Q_COLS:], (0, 2, 1))
    prompt_bias = [_prompt_bias(rel_bias[:, g], dil) for g, (_, dil) in enumerate(ATTN_GROUPS)]
    prompt_cap = _prompt_cap()
    sample_tables = _sample_tables(rel_bias, dec_seq, tuple(c.shape[-1] for c in caches))

    kv_rows_major = lambda kvt: jnp.transpose(kvt, (0, 1, 5, 2, 3, 4))

    pool_p, pool_s = [], []
    kvts = None
    kv_s = [[] for _ in ATTN_GROUPS]
    ahead = None
    for i in range(depth):
        li = i // 2
        last = i == depth - 1
        if i % 2 == 0:
            xp, sp = _pool_prompt(xp, norm_mix[i], w_pool_b[li], pool_scale[li], batch, seq)
            xs, ss = _pool_sample(xs, state_t, li, norm_mix[i], w_pool_b[li], pool_scale[li], n, dec_seq,
                                  past_len)
            pool_p.append(sp)
            pool_s.append(ss)
        else:
            qps, kvps, kvts = _qkv_prompt(xp, norm_mix[i], w_qkv_b[li], w_kv_t[li], batch, seq, li, depth // 2, kvts)
            os, lses = zip(*[_attn_prompt(qps[g], kvps[g], prompt_bias[g], prompt_cap) for g in range(N_GROUPS)])
            xp = _wo_merge(os, lses, w_o_b[li], xp, batch, seq)
            assert ahead is not None, "every attention layer must follow a pooling layer"
            o_s, kvfs = ahead
            ahead = None
            xs = _wo(o_s, w_o_b[li], xs)
            kvfs = kvfs.reshape(n, dec_seq, N_GROUPS, 2, HEADS, HEAD_DIM)
            for g in range(N_GROUPS):
                kv_s[g].append(kvfs[:, :, g])
        xs = _mlp(xs, norm_ffn[i], w_up_b, w_down_b, i, norm_final, last)
        if i % 2 == 0 and i + 1 < depth:
            nli = (i + 1) // 2
            qs, kvbs, kvfs = _qkv(xs, norm_mix[i + 1], w_qkv_b[nli])
            xp, o_s = _mlp_with_sample_attn(xp, norm_ffn[i], w_up_b, w_down_b, i, qs, kvbs, caches, nli,
                                            sample_tables, n, dec_seq)
            ahead = (o_s, kvfs)
        else:
            xp = _mlp(xp, norm_ffn[i], w_up_b, w_down_b, i, norm_final, last)
    return (xp.reshape(batch, seq, D_MODEL), xs.reshape(n, dec_seq, D_MODEL),
            jnp.stack(pool_p), jnp.transpose(jnp.stack(pool_s), (0, 2, 1, 3)),
            kv_rows_major(kvts[0]), jnp.stack(kv_s[0]),
            kv_rows_major(kvts[1]), jnp.stack(kv_s[1]),
            kv_rows_major(kvts[2]), jnp.stack(kv_s[2]))
```

```python
import functools

import numpy as np
import jax
import jax.numpy as jnp
from jax import lax
from jax.experimental import pallas as pl
from jax.experimental.pallas import tpu as pltpu

F32 = jnp.float32
BF16 = jnp.bfloat16

D_MODEL = 1024
D_FF = 4 * D_MODEL
POOL_WINDOWS = (2, 4, 8, 16)
POOL_CH = D_MODEL // len(POOL_WINDOWS)
POOL_STATE = max(POOL_WINDOWS) - 1
ATTN_GROUPS = ((128, 1), (512, 4), (2048, 16))
N_GROUPS = len(ATTN_GROUPS)
HEADS = 8
HEAD_DIM = 64
ATTN_OUT = HEADS * HEAD_DIM
BAND = 128
N_BUCKETS = 32
MAX_EXACT = N_BUCKETS // 2
REL_MAX_DIST = 2048
PAST_LEN = 2048
RMS_EPS = 1e-6
NEG_INF = -1e30

VMEM_LIMIT_BYTES = 56 * 1024 * 1024
FUSED_VMEM_LIMIT_BYTES = 62 * 1024 * 1024
ROW_TILE = 512
FF_CHUNK = 1024
FF_CHUNKS = D_FF // FF_CHUNK
_NT = (((1,), (1,)), ((), ()))
LANES = 128


def _cparams(*sem):
    return pltpu.CompilerParams(dimension_semantics=sem, vmem_limit_bytes=VMEM_LIMIT_BYTES)


def _rms(x, g):
    ms = jnp.mean(x * x, axis=-1, keepdims=True)
    return x * lax.rsqrt(ms + RMS_EPS) * g


def _t5_bucket(dist):
    n = np.maximum(np.asarray(dist), 0)
    large = MAX_EXACT + (np.log(np.maximum(n, 1) / MAX_EXACT) / np.log(REL_MAX_DIST / MAX_EXACT)
                         * (N_BUCKETS - MAX_EXACT)).astype(np.int64)
    large = np.minimum(large, N_BUCKETS - 1)
    return np.where(n < MAX_EXACT, n, large).astype(np.int32)


def _const_spec(shape):
    nd = len(shape)
    return pl.BlockSpec(shape, lambda *_: (0,) * nd, pipeline_mode=pl.Buffered(1))


def _layer_spec(shape, layer):
    nd = len(shape)
    return pl.BlockSpec((None,) + tuple(shape), lambda *_: (layer,) + (0,) * nd, pipeline_mode=pl.Buffered(1))


def _mlp_kernel(x_ref, g_ref, wu_ref, wd_ref, gf_ref, o_ref, *, final_norm):
    x = x_ref[...]
    h = _rms(x, g_ref[...]).astype(BF16)
    acc = x
    for c in range(FF_CHUNKS):
        u = jnp.dot(h, wu_ref[:, c * FF_CHUNK:(c + 1) * FF_CHUNK], preferred_element_type=F32)
        a = jnp.square(jnp.maximum(u, 0.0)).astype(BF16)
        acc = acc + jnp.dot(a, wd_ref[c], preferred_element_type=F32)
    if final_norm:
        acc = _rms(acc, gf_ref[...])
    o_ref[...] = acc


def _mlp(x, g, w_up, w_down, layer, g_final, final_norm):
    m = x.shape[0]
    tm = min(ROW_TILE, m)
    return pl.pallas_call(
        functools.partial(_mlp_kernel, final_norm=final_norm),
        grid=(m // tm,),
        in_specs=[
            pl.BlockSpec((tm, D_MODEL), lambda i: (i, 0)),
            _const_spec((1, D_MODEL)),
            _layer_spec((D_MODEL, D_FF), layer),
            _layer_spec((FF_CHUNKS, FF_CHUNK, D_MODEL), layer),
            _const_spec((1, D_MODEL)),
        ],
        out_specs=pl.BlockSpec((tm, D_MODEL), lambda i: (i, 0)),
        out_shape=jax.ShapeDtypeStruct((m, D_MODEL), F32),
        compiler_params=_cparams("parallel"),
        name="mlp",
    )(x, g.reshape(1, D_MODEL), w_up, w_down, g_final.reshape(1, D_MODEL))


Q_COLS = N_GROUPS * ATTN_OUT
KV_COLS = 2 * N_GROUPS * ATTN_OUT


def _w_cols(which, g):
    start = (which * N_GROUPS + g) * ATTN_OUT
    return slice(start, start + ATTN_OUT)


def _qkv_kernel(x_ref, g_ref, w_ref, q_ref, kvb_ref, kvf_ref):
    h = _rms(x_ref[...], g_ref[...]).astype(BF16)
    q = jnp.dot(h, w_ref[:, :Q_COLS], preferred_element_type=F32)
    q_ref[...] = (q * (HEAD_DIM ** -0.5)).astype(BF16)
    for g in range(N_GROUPS):
        for which in (1, 2):
            out = slice((2 * g + which - 1) * ATTN_OUT, (2 * g + which) * ATTN_OUT)
            kv = jnp.dot(h, w_ref[:, _w_cols(which, g)], preferred_element_type=F32)
            kvf_ref[:, out] = kv
            kvb_ref[:, out] = kv.astype(BF16)


def _qkv(x, g, w):
    m = x.shape[0]
    tm = min(ROW_TILE, m)
    row = lambda i: (i, 0)
    return pl.pallas_call(
        _qkv_kernel,
        grid=(m // tm,),
        in_specs=[
            pl.BlockSpec((tm, D_MODEL), row),
            _const_spec((1, D_MODEL)),
            _const_spec((D_MODEL, Q_COLS + KV_COLS)),
        ],
        out_specs=[
            pl.BlockSpec((tm, Q_COLS), row),
            pl.BlockSpec((tm, KV_COLS), row),
            pl.BlockSpec((tm, KV_COLS), row),
        ],
        out_shape=[
            jax.ShapeDtypeStruct((m, Q_COLS), BF16),
            jax.ShapeDtypeStruct((m, KV_COLS), BF16),
            jax.ShapeDtypeStruct((m, KV_COLS), F32),
        ],
        compiler_params=_cparams("parallel"),
        name="qkv",
    )(x, g.reshape(1, D_MODEL), w)


def _qkv_prompt_kernel(x_ref, g_ref, w_ref, wt_ref, *rest, tm, n_tiles, wins, n_prev):
    rest = rest[n_prev:]
    q_refs, kv_refs, kvt_refs, scr_ref = rest[0:3], rest[3:6], rest[6:9], rest[9]
    i = pl.program_id(1)
    hf = _rms(x_ref[0], g_ref[...])
    h = hf.astype(BF16)

    h_by_dil = {1: h}
    dils = sorted({dil for _, dil in ATTN_GROUPS if dil > 1})
    chunks = D_MODEL // LANES
    if dils:
        for c in range(chunks):
            scr_ref[c] = hf[:, c * LANES:(c + 1) * LANES]
    for dil in dils:
        classes = [jnp.concatenate([scr_ref[c, pl.ds(r, tm // dil, stride=dil), :] for c in range(chunks)], axis=1)
                   for r in range(dil)]
        h_by_dil[dil] = jnp.concatenate(classes, axis=0).astype(BF16)

    for g, (_, dil) in enumerate(ATTN_GROUPS):
        hg = h_by_dil[dil]
        q = jnp.dot(hg, w_ref[:, _w_cols(0, g)], preferred_element_type=F32) * (HEAD_DIM ** -0.5)
        q_refs[g][0] = q.reshape(dil, tm // dil, ATTN_OUT).astype(BF16)
        for which in (1, 2):
            kv = jnp.dot(hg, w_ref[:, _w_cols(which, g)], preferred_element_type=F32)
            kv_refs[g][0, :, :, (which - 1) * ATTN_OUT:which * ATTN_OUT] = (
                kv.reshape(dil, tm // dil, ATTN_OUT).astype(BF16))

    for g, win in enumerate(wins):
        rows = min(win, tm)
        first_tile = n_tiles - max(win // tm, 1)

        @pl.when(i >= first_tile)
        def _(g=g, rows=rows):
            for which in (1, 2):
                start = ((which - 1) * N_GROUPS + g) * ATTN_OUT
                kvt = lax.dot_general(wt_ref[start:start + ATTN_OUT, :], h[tm - rows:, :], _NT,
                                      preferred_element_type=F32)
                kvt_refs[g][0, which - 1] = kvt.reshape(HEADS, HEAD_DIM, rows)


def _qkv_prompt(x, g, w, w_t, batch, seq, li, n_layers, kvt_prev):
    tm = min(ROW_TILE, seq)
    n_tiles = seq // tm
    wins = tuple(min(win, seq) for win, _ in ATTN_GROUPS)
    assert all(w_ % tm == 0 or tm % w_ == 0 for w_ in wins)
    split = lambda width: [
        (pl.BlockSpec((1, dil, tm // dil, width), lambda b, i: (b, 0, i, 0)),
         jax.ShapeDtypeStruct((batch, dil, seq // dil, width), BF16)) for _, dil in ATTN_GROUPS]
    kvt = [(pl.BlockSpec((None, 1, 2, HEADS, HEAD_DIM, min(win, tm)),
                         lambda b, i, first=n_tiles - max(win // tm, 1): (li, b, 0, 0, 0, jnp.maximum(i - first, 0))),
            jax.ShapeDtypeStruct((n_layers, batch, 2, HEADS, HEAD_DIM, win), F32)) for win in wins]
    outs = split(ATTN_OUT) + split(2 * ATTN_OUT) + kvt
    prev = [jnp.zeros(o[1].shape, F32) for o in kvt] if kvt_prev is None else list(kvt_prev)
    n_in = 4
    res = pl.pallas_call(
        functools.partial(_qkv_prompt_kernel, tm=tm, n_tiles=n_tiles, wins=wins, n_prev=len(prev)),
        grid=(batch, n_tiles),
        in_specs=[
            pl.BlockSpec((1, tm, D_MODEL), lambda b, i: (b, i, 0)),
            _const_spec((1, D_MODEL)),
            _const_spec((D_MODEL, Q_COLS + KV_COLS)),
            _const_spec((KV_COLS, D_MODEL)),
        ] + [pl.BlockSpec(memory_space=pl.ANY)] * len(prev),
        out_specs=[o[0] for o in outs],
        out_shape=[o[1] for o in outs],
        input_output_aliases={n_in + k: 2 * N_GROUPS + k for k in range(len(prev))},
        scratch_shapes=[pltpu.VMEM((D_MODEL // LANES, tm, LANES), F32)],
        compiler_params=_cparams("parallel", "arbitrary"),
        name="qkv_prompt",
    )(x.reshape(batch, seq, D_MODEL), g.reshape(1, D_MODEL), w, w_t, *prev)
    return res[0:3], res[3:6], res[6:9]


def _attn_prompt_kernel(q_ref, kvp_ref, kvc_ref, bias_ref, cap_ref, o_ref, st_ref, s_scr, p_scr, *,
                        classes, blocks):
    n = pl.program_id(2)
    lane = lax.broadcasted_iota(jnp.int32, (BAND, LANES), 1)
    low_half = lane < HEAD_DIM
    half_sel = [low_half.astype(F32).astype(BF16), (~low_half).astype(F32).astype(BF16)]

    for cls, blk in [(c, b) for c in range(classes) for b in range(blocks)]:
        rows = slice(blk * BAND, (blk + 1) * BAND)
        if blk == 0:
            first = jnp.where(n > 0, 0, 1)
            prev = lambda cols, cls=cls: kvp_ref[0, cls, :, cols]
        else:
            first = 0
            prev = lambda cols, cls=cls, blk=blk: kvc_ref[0, cls, (blk - 1) * BAND:blk * BAND, cols]
        keys = lambda cols, prev=prev, cls=cls, rows=rows: jnp.concatenate(
            [prev(cols), kvc_ref[0, cls, rows, cols]], axis=0)

        for pair in range(HEADS // 2):
            cols = slice(pair * LANES, (pair + 1) * LANES)
            q2 = q_ref[0, cls, rows, cols]
            k2 = keys(cols)
            for half in range(2):
                s_scr[2 * pair + half] = lax.dot_general(q2 * half_sel[half], k2, _NT,
                                                         preferred_element_type=F32)

        stats = jnp.zeros((BAND, LANES), F32)
        for h in range(HEADS):
            t = jnp.minimum(s_scr[h] + bias_ref[h], cap_ref[first])
            mx = jnp.max(t, axis=-1, keepdims=True)
            e = jnp.exp(t - mx)
            den = jnp.sum(e, axis=-1, keepdims=True)
            p_scr[h] = e.astype(BF16)
            stats = jnp.where(lane == h, mx, jnp.where(lane == HEADS + h, den, stats))
        st_ref[0, cls, rows, :] = stats

        for pair in range(HEADS // 2):
            cols = slice(pair * LANES, (pair + 1) * LANES)
            v2 = keys(slice(ATTN_OUT + cols.start, ATTN_OUT + cols.stop))
            o_lo = jnp.dot(p_scr[2 * pair], v2, preferred_element_type=F32)
            o_hi = jnp.dot(p_scr[2 * pair + 1], v2, preferred_element_type=F32)
            o_ref[0, cls, rows, cols] = jnp.where(low_half, o_lo, o_hi).astype(o_ref.dtype)


BLOCKS_PER_STEP = 4

def _attn_prompt(q, kv, bias, cap):
    batch, dil, sub, _ = q.shape
    blocks = min(BLOCKS_PER_STEP, sub // BAND)
    classes = min(BLOCKS_PER_STEP // blocks, dil)
    step = blocks * BAND
    assert sub % step == 0 and dil % classes == 0
    cur = lambda b, r, n: (b, r, n, 0)
    return pl.pallas_call(
        functools.partial(_attn_prompt_kernel, classes=classes, blocks=blocks),
        grid=(batch, dil // classes, sub // step),
        in_specs=[
            pl.BlockSpec((1, classes, step, ATTN_OUT), cur),
            pl.BlockSpec((1, classes, BAND, 2 * ATTN_OUT),
                         lambda b, r, n: (b, r, jnp.maximum(n * blocks - 1, 0), 0)),
            pl.BlockSpec((1, classes, step, 2 * ATTN_OUT), cur),
            _const_spec((HEADS, BAND, 2 * BAND)),
            _const_spec((2, BAND, 2 * BAND)),
        ],
        out_specs=[
            pl.BlockSpec((1, classes, step, ATTN_OUT), cur),
            pl.BlockSpec((1, classes, step, LANES), cur),
        ],
        out_shape=[
            jax.ShapeDtypeStruct((batch, dil, sub, ATTN_OUT), BF16),
            jax.ShapeDtypeStruct((batch, dil, sub, LANES), F32),
        ],
        scratch_shapes=[pltpu.VMEM((HEADS, BAND, 2 * BAND), F32), pltpu.VMEM((HEADS, BAND, 2 * BAND), BF16)],
        compiler_params=_cparams("parallel", "parallel", "arbitrary"),
        name="attn_prompt_d%d" % dil,
    )(q, kv, kv, bias, cap)


def _prompt_bias(rel_bias_g, dil):
    tab = rel_bias_g.astype(F32)[_t5_bucket(np.arange(BAND, -1, -1) * dil)]
    ext = jnp.pad(tab.T, ((0, 0), (BAND, BAND)))
    tiled = jnp.tile(ext, (1, BAND))[:, :BAND * 3 * BAND].reshape(HEADS, BAND, 3 * BAND)
    return tiled[:, :, BAND:]


def _prompt_cap():
    m = BAND + np.arange(BAND)[:, None] - np.arange(2 * BAND)[None, :]
    band = (m >= 0) & (m <= BAND)
    no_prev = band & (np.arange(2 * BAND) >= BAND)[None, :]
    big = np.finfo(np.float32).max
    return jnp.asarray(np.where(np.stack([band, no_prev]), big, NEG_INF).astype(np.float32))


def _wo_merge_kernel(o0_ref, o1_ref, o2_ref, l0_ref, l1_ref, l2_ref, e_ref, w_ref, x_ref, out_ref,
                     oscr_ref, lscr_ref, *, tm):
    def token_order(ref, scr_ref, dil):
        if dil == 1:
            return ref[0, 0].astype(F32)
        chunks = ref.shape[-1] // LANES
        for r in range(dil):
            for c in range(chunks):
                scr_ref[c, pl.ds(r, tm // dil, stride=dil), :] = ref[0, r, :, c * LANES:(c + 1) * LANES].astype(F32)
        return jnp.concatenate([scr_ref[c] for c in range(chunks)], axis=1)

    dils = [dil for _, dil in ATTN_GROUPS]
    sts = [token_order(l_ref, lscr_ref.at[g:g + 1], dils[g]) for g, l_ref in enumerate((l0_ref, l1_ref, l2_ref))]
    mxs = [st[:, :HEADS] for st in sts]
    dens = [st[:, HEADS:2 * HEADS] for st in sts]
    mx = jnp.maximum(jnp.maximum(mxs[0], mxs[1]), mxs[2])
    es = [jnp.exp(m - mx) for m in mxs]
    inv = 1.0 / (es[0] * dens[0] + es[1] * dens[1] + es[2] * dens[2])
    acc = None
    for g, o_ref in enumerate((o0_ref, o1_ref, o2_ref)):
        wt = es[g] * inv
        hi = wt.astype(BF16).astype(F32)
        parts = jnp.concatenate([hi, wt - hi], axis=1).astype(BF16)
        wexp = jnp.dot(parts, e_ref[...], preferred_element_type=F32)
        term = wexp * token_order(o_ref, oscr_ref, dils[g])
        acc = term if acc is None else acc + term
    out_ref[0] = x_ref[0] + jnp.dot(acc.astype(BF16), w_ref[...], preferred_element_type=F32)


def _wo_merge(os, lses, w_o, x, batch, seq):
    tm = min(ROW_TILE, seq)
    expand = jnp.asarray(np.tile(np.repeat(np.eye(HEADS, dtype=np.float32), HEAD_DIM, axis=1), (2, 1)), BF16)
    split = lambda width: [pl.BlockSpec((1, dil, tm // dil, width), lambda b, i: (b, 0, i, 0))
                           for _, dil in ATTN_GROUPS]
    row = pl.BlockSpec((1, tm, D_MODEL), lambda b, i: (b, i, 0))
    out = pl.pallas_call(
        functools.partial(_wo_merge_kernel, tm=tm),
        grid=(batch, seq // tm),
        in_specs=split(ATTN_OUT) + split(LANES) + [
            _const_spec((2 * HEADS, ATTN_OUT)),
            _const_spec((ATTN_OUT, D_MODEL)),
            row,
        ],
        out_specs=row,
        out_shape=jax.ShapeDtypeStruct((batch, seq, D_MODEL), F32),
        scratch_shapes=[pltpu.VMEM((ATTN_OUT // LANES, tm, LANES), F32), pltpu.VMEM((N_GROUPS, tm, LANES), F32)],
        compiler_params=_cparams("parallel", "parallel"),
        name="wo_merge",
    )(*os, *lses, expand, w_o, x.reshape(batch, seq, D_MODEL))
    return out.reshape(batch * seq, D_MODEL)


def _wo_kernel(o_ref, w_ref, x_ref, out_ref):
    out_ref[...] = x_ref[...] + jnp.dot(o_ref[...].astype(BF16), w_ref[...], preferred_element_type=F32)


def _wo(o, w_o, x):
    m = x.shape[0]
    tm = min(ROW_TILE, m)
    row = lambda i: (i, 0)
    return pl.pallas_call(
        _wo_kernel,
        grid=(m // tm,),
        in_specs=[pl.BlockSpec((tm, ATTN_OUT), row), _const_spec((ATTN_OUT, D_MODEL)),
                  pl.BlockSpec((tm, D_MODEL), row)],
        out_specs=pl.BlockSpec((tm, D_MODEL), row),
        out_shape=jax.ShapeDtypeStruct((m, D_MODEL), F32),
        compiler_params=_cparams("parallel"),
        name="wo",
    )(o, w_o, x)


Q_PAD = 8
NEW_PAD = 16


def _sample_tables(rel_bias, dec_seq, buf_rows):
    tq = np.arange(Q_PAD)
    tables = []
    for g, (_, dil) in enumerate(ATTN_GROUPS):
        wb = buf_rows[g]
        by_dist = rel_bias[:, g].astype(F32)[_t5_bucket(np.arange(wb + Q_PAD))]
        rev = by_dist[::-1].T
        for kpos, real in ((np.arange(wb), np.ones(wb, bool)),
                           (wb + np.arange(NEW_PAD), np.arange(NEW_PAD) < dec_seq)):
            delta = (wb + tq)[:, None] - kpos[None, :]
            ok = (delta >= 0) & (delta % dil == 0) & (delta // dil <= BAND)
            ok &= (tq < dec_seq)[:, None] & real[None, :]
            if kpos.shape[0] == wb:
                bias = jnp.stack([rev[:, Q_PAD - 1 - t:Q_PAD - 1 - t + wb] for t in range(Q_PAD)], axis=1)
            else:
                bias = jnp.transpose(by_dist[np.maximum(delta, 0)], (2, 0, 1))
            tables.append(bias)
            tables.append(jnp.asarray(ok.astype(np.float32)))
    return tables


def _attn_sample_body(q_ref, kvn_ref, c_refs, tables, o_ref, s_scr):
    rows = [c.shape[-1] for c in c_refs]
    offs = [sum(rows[:g]) for g in range(len(rows))]
    head_rows = lambda h: slice(h * Q_PAD, (h + 1) * Q_PAD)
    new_scores = []
    for g, c_ref in enumerate(c_refs):
        bias_ref, valid_ref, nbias_ref, nvalid_ref = tables[4 * g:4 * g + 4]
        per_head = []
        for h in range(HEADS):
            q = q_ref[0, g, h]
            s = jnp.dot(q, c_ref[0, h].astype(BF16), preferred_element_type=F32)
            s_scr[head_rows(h), offs[g]:offs[g] + rows[g]] = jnp.where(valid_ref[...] > 0.0, s + bias_ref[h], NEG_INF)
            s = lax.dot_general(q, kvn_ref[0, g, 0, h], _NT, preferred_element_type=F32)
            per_head.append(jnp.where(nvalid_ref[...] > 0.0, s + nbias_ref[h], NEG_INF))
        new_scores.append(jnp.concatenate(per_head, axis=0))

    s = s_scr[...]
    mx = jnp.max(s, axis=-1, keepdims=True)
    for sn in new_scores:
        mx = jnp.maximum(mx, jnp.max(sn, axis=-1, keepdims=True))
    e = jnp.exp(s - mx)
    den = jnp.sum(e, axis=-1, keepdims=True)
    s_scr[...] = e
    new_probs = []
    for sn in new_scores:
        en = jnp.exp(sn - mx)
        den = den + jnp.sum(en, axis=-1, keepdims=True)
        new_probs.append(en)
    inv = 1.0 / den

    for h in range(HEADS):
        acc = jnp.zeros((Q_PAD, HEAD_DIM), F32)
        for g, c_ref in enumerate(c_refs):
            p = s_scr[head_rows(h), offs[g]:offs[g] + rows[g]].astype(BF16)
            acc = acc + lax.dot_general(p, c_ref[1, h].astype(BF16), _NT, preferred_element_type=F32)
            acc = acc + jnp.dot(new_probs[g][head_rows(h), :].astype(BF16), kvn_ref[0, g, 1, h],
                                preferred_element_type=F32)
        o_ref[0, h] = acc * inv[head_rows(h), :]


def _mlp_attn_kernel(x_ref, g_ref, wu_ref, wd_ref, q_ref, kvn_ref, c0_ref, c1_ref, c2_ref, *rest):
    tables, out_ref, o_ref, acc_ref, h_ref, s_scr = rest[:-5], rest[-5], rest[-4], rest[-3], rest[-2], rest[-1]
    j = pl.program_id(1)

    @pl.when(j == 0)
    def _():
        acc_ref[...] = x_ref[...]
        h_ref[...] = _rms(x_ref[...], g_ref[...]).astype(BF16)

    h = h_ref[...]
    u = jnp.dot(h, wu_ref[:, pl.ds(pl.multiple_of(j * FF_CHUNK, FF_CHUNK), FF_CHUNK)], preferred_element_type=F32)
    a = jnp.square(jnp.maximum(u, 0.0)).astype(BF16)
    acc = acc_ref[...] + jnp.dot(a, wd_ref[j], preferred_element_type=F32)
    acc_ref[...] = acc
    _attn_sample_body(q_ref, kvn_ref, (c0_ref, c1_ref, c2_ref), tables, o_ref, s_scr)
    out_ref[...] = acc_ref[...]


def _sample_qkv_by_head(q, kvb, n, dec_seq):
    q5 = q.reshape(n, dec_seq, N_GROUPS, HEADS, HEAD_DIM).transpose(0, 2, 3, 1, 4)
    q5 = jnp.pad(q5, ((0, 0), (0, 0), (0, 0), (0, Q_PAD - dec_seq), (0, 0)))
    kvn = kvb.reshape(n, dec_seq, N_GROUPS, 2, HEADS, HEAD_DIM).transpose(0, 2, 3, 4, 1, 5)
    kvn = jnp.pad(kvn, ((0, 0), (0, 0), (0, 0), (0, 0), (0, NEW_PAD - dec_seq), (0, 0)))
    return q5, kvn


def _mlp_with_sample_attn(x, g, w_up, w_down, layer, q, kvb, caches, li, tables, n, dec_seq):
    m = x.shape[0]
    tm = min(ROW_TILE, m)
    assert n % (m // tm) == 0 and n // (m // tm) == FF_CHUNKS
    q5, kvn = _sample_qkv_by_head(q, kvb, n, dec_seq)
    seq = lambda i, j: i * FF_CHUNKS + j
    cache_spec = lambda c: pl.BlockSpec((None, None) + c.shape[2:], lambda i, j: (li, seq(i, j), 0, 0, 0, 0))
    row = pl.BlockSpec((tm, D_MODEL), lambda i, j: (i, 0))
    out, o = pl.pallas_call(
        _mlp_attn_kernel,
        grid=(m // tm, FF_CHUNKS),
        in_specs=[
            row,
            _const_spec((1, D_MODEL)),
            _layer_spec((D_MODEL, D_FF), layer),
            _layer_spec((FF_CHUNKS, FF_CHUNK, D_MODEL), layer),
            pl.BlockSpec((1, N_GROUPS, HEADS, Q_PAD, HEAD_DIM), lambda i, j: (seq(i, j), 0, 0, 0, 0)),
            pl.BlockSpec((1, N_GROUPS, 2, HEADS, NEW_PAD, HEAD_DIM), lambda i, j: (seq(i, j), 0, 0, 0, 0, 0)),
        ] + [cache_spec(c) for c in caches] + [_const_spec(t.shape) for t in tables],
        out_specs=[row, pl.BlockSpec((1, HEADS, Q_PAD, HEAD_DIM), lambda i, j: (seq(i, j), 0, 0, 0))],
        out_shape=[jax.ShapeDtypeStruct((m, D_MODEL), F32),
                   jax.ShapeDtypeStruct((n, HEADS, Q_PAD, HEAD_DIM), F32)],
        scratch_shapes=[pltpu.VMEM((tm, D_MODEL), F32), pltpu.VMEM((tm, D_MODEL), BF16),
                        pltpu.VMEM((HEADS * Q_PAD, sum(c.shape[-1] for c in caches)), F32)],
        compiler_params=pltpu.CompilerParams(dimension_semantics=("parallel", "arbitrary"),
                                             vmem_limit_bytes=FUSED_VMEM_LIMIT_BYTES),
        name="mlp_attn",
    )(x, g.reshape(1, D_MODEL), w_up, w_down, q5, kvn, *caches, *tables)
    return out, o[:, :, :dec_seq].transpose(0, 2, 1, 3).reshape(n * dec_seq, ATTN_OUT)


HALO = 16
POOL_PAD = 2 * HALO
SUBLANES = 8


def _pool_prompt_kernel(x_ref, g_ref, w_ref, sc_ref, o_ref, st_ref, a_ref, b_ref, *, tt):
    i = pl.program_id(1)
    n = POOL_PAD + tt

    @pl.when(i == 0)
    def _():
        a_ref[0:POOL_PAD, :] = jnp.zeros((POOL_PAD, D_MODEL), F32)
        b_ref[0:SUBLANES, :] = jnp.zeros((SUBLANES, D_MODEL), F32)

    x = x_ref[0]
    h = _rms(x, g_ref[...])
    a_ref[POOL_PAD:n, :] = h
    pos1 = (i * tt + 1 + lax.broadcasted_iota(jnp.int32, (tt, 1), 0)).astype(F32)
    ys = []
    for g, w in enumerate(POOL_WINDOWS):
        cols = slice(g * POOL_CH, (g + 1) * POOL_CH)
        src, dst = a_ref, b_ref
        span = 1
        while span < w:
            dst[SUBLANES:n, cols] = src[SUBLANES:n, cols] + src[SUBLANES - span:n - span, cols]
            src, dst = dst, src
            span *= 2
        p = src[POOL_PAD:n, cols] / jnp.minimum(float(w), pos1) - h[:, cols]
        ys.append(jnp.dot(p.astype(BF16), w_ref[g], preferred_element_type=F32))
    o_ref[0] = x + jnp.concatenate(ys, axis=1) * sc_ref[...]
    tail = h[tt - HALO:, :]
    a_ref[HALO:POOL_PAD, :] = tail

    @pl.when(i == pl.num_programs(1) - 1)
    def _():
        st_ref[0] = tail


def _pool_prompt(x, g, w_pool, scale, batch, seq):
    tt = min(ROW_TILE, seq)
    out, st = pl.pallas_call(
        functools.partial(_pool_prompt_kernel, tt=tt),
        grid=(batch, seq // tt),
        in_specs=[
            pl.BlockSpec((1, tt, D_MODEL), lambda b, i: (b, i, 0)),
            _const_spec((1, D_MODEL)),
            _const_spec(w_pool.shape),
            _const_spec((1, D_MODEL)),
        ],
        out_specs=[
            pl.BlockSpec((1, tt, D_MODEL), lambda b, i: (b, i, 0)),
            pl.BlockSpec((1, HALO, D_MODEL), lambda b, i: (b, 0, 0)),
        ],
        out_shape=[
            jax.ShapeDtypeStruct((batch, seq, D_MODEL), F32),
            jax.ShapeDtypeStruct((batch, HALO, D_MODEL), F32),
        ],
        scratch_shapes=[pltpu.VMEM((POOL_PAD + tt, D_MODEL), F32)] * 2,
        compiler_params=_cparams("parallel", "arbitrary"),
        name="pool_prompt",
    )(x.reshape(batch, seq, D_MODEL), g.reshape(1, D_MODEL), w_pool, scale.reshape(1, D_MODEL))
    return out.reshape(batch * seq, D_MODEL), st[:, HALO - POOL_STATE:, :]


def _pool_sample_kernel(x_ref, st_ref, g_ref, w_ref, sc_ref, o_ref, nst_ref, *, dec_seq, past_len):
    chunk = lambda ref, k: ref[:, k * D_MODEL:(k + 1) * D_MODEL]
    xs = [chunk(x_ref, t) for t in range(dec_seq)]
    hs = [_rms(x, g_ref[...]) for x in xs]
    rows = [st_ref[k] for k in range(POOL_STATE)] + hs
    for t in range(dec_seq):
        ys = []
        for g, w in enumerate(POOL_WINDOWS):
            cols = slice(g * POOL_CH, (g + 1) * POOL_CH)
            win = rows[POOL_STATE + t][:, cols]
            for j in range(1, w):
                win = win + rows[POOL_STATE + t - j][:, cols]
            p = win / float(min(w, past_len + t + 1)) - hs[t][:, cols]
            ys.append(jnp.dot(p.astype(BF16), w_ref[g], preferred_element_type=F32))
        o_ref[:, t * D_MODEL:(t + 1) * D_MODEL] = xs[t] + jnp.concatenate(ys, axis=1) * sc_ref[...]
    new_rows = rows[-POOL_STATE:]
    for k in range(POOL_STATE):
        nst_ref[k] = new_rows[k]


def _pool_sample(x, state_t, li, g, w_pool, scale, n, dec_seq, past_len):
    bn = min(32, n)
    out, nst = pl.pallas_call(
        functools.partial(_pool_sample_kernel, dec_seq=dec_seq, past_len=past_len),
        grid=(n // bn,),
        in_specs=[
            pl.BlockSpec((bn, dec_seq * D_MODEL), lambda i: (i, 0)),
            pl.BlockSpec((None, POOL_STATE, bn, D_MODEL), lambda i: (li, 0, i, 0)),
            _const_spec((1, D_MODEL)),
            _const_spec(w_pool.shape),
            _const_spec((1, D_MODEL)),
        ],
        out_specs=[
            pl.BlockSpec((bn, dec_seq * D_MODEL), lambda i: (i, 0)),
            pl.BlockSpec((POOL_STATE, bn, D_MODEL), lambda i: (0, i, 0)),
        ],
        out_shape=[
            jax.ShapeDtypeStruct((n, dec_seq * D_MODEL), F32),
            jax.ShapeDtypeStruct((POOL_STATE, n, D_MODEL), F32),
        ],
        compiler_params=_cparams("parallel"),
        name="pool_sample",
    )(x.reshape(n, dec_seq * D_MODEL), state_t, g.reshape(1, D_MODEL), w_pool, scale.reshape(1, D_MODEL))
    return out.reshape(n * dec_seq, D_MODEL), nst


def kernel(x_prompt, x_sample, state_pool, cache_kv_w128, cache_kv_w512, cache_kv_w2048, rel_bias, norm_mix,
           norm_ffn, norm_final, w_pool, pool_scale, w_qkv, w_o, w_up, w_down):
    batch, seq, _ = x_prompt.shape
    n, dec_seq, _ = x_sample.shape
    depth = norm_mix.shape[0]
    caches = tuple(jnp.transpose(c, (0, 1, 3, 4, 5, 2)) for c in (cache_kv_w128, cache_kv_w512, cache_kv_w2048))
    state_t = jnp.transpose(state_pool, (0, 2, 1, 3))
    past_len = PAST_LEN

    xp = x_prompt.reshape(batch * seq, D_MODEL)
    xs = x_sample.reshape(n * dec_seq, D_MODEL)
    w_up_b = w_up.astype(BF16)
    w_down_b = w_down.astype(BF16).reshape(depth, FF_CHUNKS, FF_CHUNK, D_MODEL)
    w_pool_b = w_pool.astype(BF16)
    w_o_b = w_o.astype(BF16)
    w_qkv_b = w_qkv.astype(BF16)
    w_kv_t = jnp.transpose(w_qkv_b[:, :, Q_COLS:], (0, 2, 1))
    prompt_bias = [_prompt_bias(rel_bias[:, g], dil) for g, (_, dil) in enumerate(ATTN_GROUPS)]
    prompt_cap = _prompt_cap()
    sample_tables = _sample_tables(rel_bias, dec_seq, tuple(c.shape[-1] for c in caches))

    kv_rows_major = lambda kvt: jnp.transpose(kvt, (0, 1, 5, 2, 3, 4))

    pool_p, pool_s = [], []
    kvts = None
    kv_s = [[] for _ in ATTN_GROUPS]
    ahead = None
    for i in range(depth):
        li = i // 2
        last = i == depth - 1
        if i % 2 == 0:
            xp, sp = _pool_prompt(xp, norm_mix[i], w_pool_b[li], pool_scale[li], batch, seq)
            xs, ss = _pool_sample(xs, state_t, li, norm_mix[i], w_pool_b[li], pool_scale[li], n, dec_seq,
                                  past_len)
            pool_p.append(sp)
            pool_s.append(ss)
        else:
            qps, kvps, kvts = _qkv_prompt(xp, norm_mix[i], w_qkv_b[li], w_kv_t[li], batch, seq, li, depth // 2, kvts)
            os, lses = zip(*[_attn_prompt(qps[g], kvps[g], prompt_bias[g], prompt_cap) for g in range(N_GROUPS)])
            xp = _wo_merge(os, lses, w_o_b[li], xp, batch, seq)
            assert ahead is not None, "every attention layer must follow a pooling layer"
            o_s, kvfs = ahead
            ahead = None
            xs = _wo(o_s, w_o_b[li], xs)
            kvfs = kvfs.reshape(n, dec_seq, N_GROUPS, 2, HEADS, HEAD_DIM)
            for g in range(N_GROUPS):
                kv_s[g].append(kvfs[:, :, g])
        xs = _mlp(xs, norm_ffn[i], w_up_b, w_down_b, i, norm_final, last)
        if i % 2 == 0 and i + 1 < depth:
            nli = (i + 1) // 2
            qs, kvbs, kvfs = _qkv(xs, norm_mix[i + 1], w_qkv_b[nli])
            xp, o_s = _mlp_with_sample_attn(xp, norm_ffn[i], w_up_b, w_down_b, i, qs, kvbs, caches, nli,
                                            sample_tables, n, dec_seq)
            ahead = (o_s, kvfs)
        else:
            xp = _mlp(xp, norm_ffn[i], w_up_b, w_down_b, i, norm_final, last)
    return (xp.reshape(batch, seq, D_MODEL), xs.reshape(n, dec_seq, D_MODEL),
            jnp.stack(pool_p), jnp.transpose(jnp.stack(pool_s), (0, 2, 1, 3)),
            kv_rows_major(kvts[0]), jnp.stack(kv_s[0]),
            kv_rows_major(kvts[1]), jnp.stack(kv_s[1]),
            kv_rows_major(kvts[2]), jnp.stack(kv_s[2]))
```
